```python
import jax, jax.numpy as jnp
from jax import lax
import numpy as np

D_MODEL = 1024
BATCH = 1
SEQ = 16384
DEPTH = 1
DEC_BATCH = 8
DEC_SEQ = 4096
PAST_LEN = 128

D_MIX = 1024
POOL_WINDOWS = (2, 4, 8, 16)
POOL_WIDTH = 512
POOL_GROUP = POOL_WIDTH // len(POOL_WINDOWS)
N_HEADS = 8
N_KV_HEADS = 2
HEAD_DIM = 64
Q_PER_KV = N_HEADS // N_KV_HEADS
ATTN_WIDTH = N_HEADS * HEAD_DIM
KV_WIDTH = N_KV_HEADS * HEAD_DIM
D_IN_PROJ = POOL_WIDTH + ATTN_WIDTH + 2 * KV_WIDTH
WINDOW = 128
ATT_BLOCK = 128
N_EXPERTS = 256
TOP_K = 8
N_EXPERT_GROUPS = 8
TOPK_GROUPS = 4
D_EXPERT = 256
D_SHARED = 256
ROUTED_SCALE = 2.5
MOE_BLOCK = 128
EPS = 1e-6
NEG_INF = -1e30

kernel_name = "hybrid_pool_swa_moe_adaln_encoder"


def rmsnorm(x, g):
    x32 = x.astype(jnp.float32)
    y = x32 * lax.rsqrt(jnp.mean(x32 * x32, axis=-1, keepdims=True) + EPS)
    return (y * g.astype(jnp.float32)).astype(x.dtype)


def alibi_slopes():
    return jnp.asarray(np.array([2.0 ** (-8.0 * (h + 1) / N_HEADS) for h in range(N_HEADS)], np.float32))


def pool_mixer(u, w_pool, b_pool, pool_scale):
    B, S, _ = u.shape
    u32 = u.astype(jnp.float32)
    cs = jnp.concatenate([jnp.zeros((B, 1, POOL_WIDTH), jnp.float32), jnp.cumsum(u32, axis=1)], axis=1)
    t = jnp.arange(S)
    outs = []
    for gi, w in enumerate(POOL_WINDOWS):
        sl = slice(gi * POOL_GROUP, (gi + 1) * POOL_GROUP)
        lo = jnp.clip(t - w // 2, 0, S)
        hi = jnp.clip(t + w // 2, 0, S)
        cnt = (hi - lo).astype(jnp.float32)
        csg = cs[..., sl]
        mean = (csg[:, hi] - csg[:, lo]) / cnt[None, :, None]
        outs.append(mean - u32[..., sl])
    d = jnp.stack(outs, axis=2)
    y = jnp.einsum('bsgc,gcd->bsgd', d, w_pool.astype(jnp.float32)) + b_pool.astype(jnp.float32)
    return (y.reshape(B, S, POOL_WIDTH) * pool_scale.astype(jnp.float32)).astype(u.dtype)


def banded_attention(q, k, v, sink):
    B, S = q.shape[0], q.shape[1]
    nb = S // ATT_BLOCK
    qb = q.reshape(B, nb, ATT_BLOCK, N_KV_HEADS, Q_PER_KV, HEAD_DIM)

    def neighbours(t):
        tp = jnp.pad(t, ((0, 0), (ATT_BLOCK, ATT_BLOCK), (0, 0), (0, 0)))
        tp = tp.reshape(B, nb + 2, ATT_BLOCK, N_KV_HEADS, HEAD_DIM)
        return jnp.concatenate([tp[:, :-2], tp[:, 1:-1], tp[:, 2:]], axis=2)

    kw, vw = neighbours(k), neighbours(v)
    s = jnp.einsum('bnqkgd,bnjkd->bnkgqj', qb, kw, preferred_element_type=jnp.float32) * (HEAD_DIM ** -0.5)
    rel = jnp.arange(ATT_BLOCK)[:, None] - (jnp.arange(3 * ATT_BLOCK)[None, :] - ATT_BLOCK)
    dist = jnp.abs(rel)
    key_pos = jnp.arange(nb)[:, None] * ATT_BLOCK + jnp.arange(3 * ATT_BLOCK)[None, :] - ATT_BLOCK
    valid = (key_pos >= 0) & (key_pos < S)
    mask = (dist <= WINDOW)[None] & valid[:, None, :]
    slopes = alibi_slopes().reshape(N_KV_HEADS, Q_PER_KV, 1, 1)
    s = s - slopes * dist.astype(jnp.float32)
    s = jnp.where(mask[None, :, None, None], s, NEG_INF)
    sink_b = sink.astype(jnp.float32).reshape(N_KV_HEADS, Q_PER_KV, 1, 1)
    m = jnp.maximum(jnp.max(s, axis=-1, keepdims=True), sink_b)
    p = jnp.exp(s - m)
    p = p / (jnp.sum(p, axis=-1, keepdims=True) + jnp.exp(sink_b - m))
    o = jnp.einsum('bnkgqj,bnjkd->bnqkgd', p, vw.astype(jnp.float32))
    return o.reshape(B, S, ATTN_WIDTH).astype(q.dtype)


def swiglu(h, wg, wu, wd):
    return (jax.nn.silu(h @ wg) * (h @ wu)) @ wd


def moe_ffn(h, w_router, router_bias, w_gate, w_up, w_down, ws_gate, ws_up, ws_down):
    N, D = h.shape
    scores = jax.nn.sigmoid(h.astype(jnp.float32) @ w_router.astype(jnp.float32))
    biased = scores + router_bias.astype(jnp.float32)
    grp = biased.reshape(N, N_EXPERT_GROUPS, N_EXPERTS // N_EXPERT_GROUPS)
    grp_score = jnp.sum(lax.top_k(grp, 2)[0], axis=-1)
    _, top_groups = lax.top_k(grp_score, TOPK_GROUPS)
    gmask = jnp.sum(jax.nn.one_hot(top_groups, N_EXPERT_GROUPS), axis=1) > 0
    emask = jnp.repeat(gmask, N_EXPERTS // N_EXPERT_GROUPS, axis=1)
    _, idx = lax.top_k(jnp.where(emask, biased, NEG_INF), TOP_K)
    wts = jnp.take_along_axis(scores, idx, axis=1)
    wts = wts / jnp.sum(wts, axis=-1, keepdims=True) * ROUTED_SCALE

    NK = N * TOP_K
    flat_e = idx.reshape(-1)
    flat_t = jnp.repeat(jnp.arange(N, dtype=jnp.int32), TOP_K)
    flat_w = wts.reshape(-1)
    order = jnp.argsort(flat_e)
    se, st, sw = flat_e[order], flat_t[order], flat_w[order]
    counts = jnp.bincount(flat_e, length=N_EXPERTS)
    starts = jnp.cumsum(counts) - counts
    padded = (counts + MOE_BLOCK - 1) // MOE_BLOCK * MOE_BLOCK
    pend = jnp.cumsum(padded)
    pstarts = pend - padded
    dest = pstarts[se] + jnp.arange(NK) - starts[se]
    R = NK + N_EXPERTS * MOE_BLOCK
    nblk = R // MOE_BLOCK
    row_tok = jnp.full((R,), N, jnp.int32).at[dest].set(st)
    row_w = jnp.zeros((R,), jnp.float32).at[dest].set(sw)
    blk_e = jnp.clip(jnp.searchsorted(pend, jnp.arange(nblk) * MOE_BLOCK, side='right'), 0, N_EXPERTS - 1)
    h_pad = jnp.concatenate([h, jnp.zeros((1, D), h.dtype)], axis=0)

    def run_block(args):
        tok, wr, e = args
        xb = h_pad[tok]
        yb = swiglu(xb, w_gate[e], w_up[e], w_down[e])
        return (yb * wr[:, None]).astype(h.dtype)

    yb = lax.map(run_block, (row_tok.reshape(nblk, MOE_BLOCK), row_w.reshape(nblk, MOE_BLOCK), blk_e))
    routed = jnp.zeros((N + 1, D), h.dtype).at[row_tok].add(yb.reshape(R, D))[:N]
    return routed + swiglu(h, ws_gate, ws_up, ws_down)


def encoder(x, c, w_ada, b_ada, g_norm1, w_in, w_pool, b_pool, pool_scale, attn_sink, w_out,
            g_norm2, w_router, router_bias, w_gate, w_up, w_down, ws_gate, ws_up, ws_down, g_final):
    B, S, D = x.shape
    for l in range(DEPTH):
        mod = (jax.nn.silu(c.astype(jnp.float32)) @ w_ada[l].astype(jnp.float32)
               + b_ada[l].astype(jnp.float32)).astype(x.dtype)
        sh1, sc1, gt1, sh2, sc2, gt2 = jnp.split(mod, 6, axis=-1)
        h = rmsnorm(x, g_norm1[l]) * (1 + sc1[:, None]) + sh1[:, None]
        z = h @ w_in[l]
        u = z[..., :POOL_WIDTH]
        q = z[..., POOL_WIDTH:POOL_WIDTH + ATTN_WIDTH].reshape(B, S, N_HEADS, HEAD_DIM)
        k = z[..., POOL_WIDTH + ATTN_WIDTH:POOL_WIDTH + ATTN_WIDTH + KV_WIDTH].reshape(B, S, N_KV_HEADS, HEAD_DIM)
        v = z[..., POOL_WIDTH + ATTN_WIDTH + KV_WIDTH:].reshape(B, S, N_KV_HEADS, HEAD_DIM)
        a_pool = pool_mixer(u, w_pool[l], b_pool[l], pool_scale[l])
        a_attn = banded_attention(q, k, v, attn_sink[l])
        mix = jnp.concatenate([a_pool, a_attn], axis=-1) @ w_out[l]
        x = x + gt1[:, None] * mix
        h = rmsnorm(x, g_norm2[l]) * (1 + sc2[:, None]) + sh2[:, None]
        f = moe_ffn(h.reshape(B * S, D), w_router[l], router_bias[l], w_gate[l], w_up[l], w_down[l],
                    ws_gate[l], ws_up[l], ws_down[l]).reshape(B, S, D)
        x = x + gt2[:, None] * f
    return rmsnorm(x, g_final)


def setup_inputs(seed: int = 0) -> dict:
    key = jax.random.key(seed)
    ks = jax.random.split(key, 24)
    f32 = jnp.float32
    nrm = lambda k, shape, s: jax.random.normal(k, shape, f32) * s
    D = D_MODEL
    return {
        'x_prompt': nrm(ks[0], (BATCH, SEQ, D), 1.0),
        'x_sample': nrm(ks[1], (DEC_BATCH, DEC_SEQ, D), 1.0),
        'c_prompt': nrm(ks[2], (BATCH, D), 1.0),
        'c_sample': nrm(ks[3], (DEC_BATCH, D), 1.0),
        'w_ada': nrm(ks[4], (DEPTH, D, 6 * D), 0.5 * D ** -0.5),
        'b_ada': nrm(ks[5], (DEPTH, 6 * D), 0.01),
        'g_norm1': 1.0 + nrm(ks[6], (DEPTH, D), 0.1),
        'w_in': nrm(ks[7], (DEPTH, D, D_IN_PROJ), D ** -0.5),
        'w_pool': nrm(ks[8], (DEPTH, len(POOL_WINDOWS), POOL_GROUP, POOL_GROUP), POOL_GROUP ** -0.5),
        'b_pool': nrm(ks[9], (DEPTH, len(POOL_WINDOWS), POOL_GROUP), 0.01),
        'pool_scale': 1.0 + nrm(ks[10], (DEPTH, POOL_WIDTH), 0.1),
        'attn_sink': nrm(ks[11], (DEPTH, N_HEADS), 0.5),
        'w_out': nrm(ks[12], (DEPTH, D_MIX, D), D_MIX ** -0.5),
        'g_norm2': 1.0 + nrm(ks[13], (DEPTH, D), 0.1),
        'w_router': nrm(ks[14], (DEPTH, D, N_EXPERTS), D ** -0.5),
        'router_bias': nrm(ks[15], (DEPTH, N_EXPERTS), 0.01),
        'w_gate': nrm(ks[16], (DEPTH, N_EXPERTS, D, D_EXPERT), D ** -0.5),
        'w_up': nrm(ks[17], (DEPTH, N_EXPERTS, D, D_EXPERT), D ** -0.5),
        'w_down': nrm(ks[18], (DEPTH, N_EXPERTS, D_EXPERT, D), D_EXPERT ** -0.5),
        'ws_gate': nrm(ks[19], (DEPTH, D, D_SHARED), D ** -0.5),
        'ws_up': nrm(ks[20], (DEPTH, D, D_SHARED), D ** -0.5),
        'ws_down': nrm(ks[21], (DEPTH, D_SHARED, D), D_SHARED ** -0.5),
        'g_final': 1.0 + nrm(ks[22], (D,), 0.1),
    }


def reference(x_prompt, x_sample, c_prompt, c_sample, w_ada, b_ada, g_norm1, w_in, w_pool, b_pool,
              pool_scale, attn_sink, w_out, g_norm2, w_router, router_bias, w_gate, w_up, w_down,
              ws_gate, ws_up, ws_down, g_final):
    y_prompt = encoder(x_prompt, c_prompt, w_ada, b_ada, g_norm1, w_in, w_pool, b_pool, pool_scale,
                       attn_sink, w_out, g_norm2, w_router, router_bias, w_gate, w_up, w_down,
                       ws_gate, ws_up, ws_down, g_final)
    y_sample = encoder(x_sample, c_sample, w_ada, b_ada, g_norm1, w_in, w_pool, b_pool, pool_scale,
                       attn_sink, w_out, g_norm2, w_router, router_bias, w_gate, w_up, w_down,
                       ws_gate, ws_up, ws_down, g_final)
    return (y_prompt, y_sample)
```

```python
import functools

import numpy as np
import jax
import jax.numpy as jnp
from jax import lax
from jax.experimental import pallas as pl
from jax.experimental.pallas import tpu as pltpu
from jax.experimental.pallas import tpu_sc as plsc

F32 = jnp.float32
BF16 = jnp.bfloat16
I32 = jnp.int32

D_MODEL = 1024
POOL_WINDOWS = (2, 4, 8, 16)
POOL_WIDTH = 512
POOL_GROUP = 128
N_HEADS = 8
N_KV_HEADS = 2
HEAD_DIM = 64
Q_PER_KV = N_HEADS // N_KV_HEADS
ATTN_WIDTH = N_HEADS * HEAD_DIM
KV_WIDTH = N_KV_HEADS * HEAD_DIM
D_IN_PROJ = POOL_WIDTH + ATTN_WIDTH + 2 * KV_WIDTH
WINDOW = 128
N_EXPERTS = 256
TOP_K = 8
N_EXPERT_GROUPS = 8
GROUP_SIZE = N_EXPERTS // N_EXPERT_GROUPS
TOPK_GROUPS = 4
D_EXPERT = 256
D_SHARED = 256
ROUTED_SCALE = 2.5
EPS = 1e-6
NEG_INF = -1e30

LANES = 128
HALO = 8
N_PLANES = 4
HALF = D_MODEL // 2

T_IN = 512
T_MIX = 128
T_DEST = 512
BM = 512
T_FIN = 256
SC_ROWS = 128
SC_WORKERS = 32
ADA_ROWS = 16
VMEM_LIMIT = 48 * 1024 * 1024

ALIBI_SLOPES = tuple(float(2.0 ** (-8.0 * (h + 1) / N_HEADS)) for h in range(N_HEADS))


def _pack_planes(y):
    planes = []
    for j in range(N_PLANES):
        hi = y[:, j * LANES:(j + 1) * LANES].astype(BF16).astype(F32)
        lo = y[:, HALF + j * LANES:HALF + (j + 1) * LANES].astype(BF16).astype(F32)
        hb = lax.bitcast_convert_type(hi, jnp.uint32) & jnp.uint32(0xFFFF0000)
        lb = lax.bitcast_convert_type(lo, jnp.uint32) >> jnp.uint32(16)
        planes.append(lax.bitcast_convert_type(hb | lb, I32))
    return planes


def _unpack_plane(w):
    u = lax.bitcast_convert_type(w, jnp.uint32)
    hi = lax.bitcast_convert_type(u & jnp.uint32(0xFFFF0000), F32)
    lo = lax.bitcast_convert_type(u << jnp.uint32(16), F32)
    return hi, lo


def _ada_kernel(c_ref, w_ref, b_ref, o_ref):
    c = c_ref[...]
    s = c * jax.nn.sigmoid(c)
    o_ref[...] = jnp.dot(s, w_ref[...], precision=lax.Precision.HIGHEST,
                         preferred_element_type=F32) + b_ref[...]


def _ada(c_all, w_ada, b_ada):
    n_out = w_ada.shape[1]
    tn = 1024
    return pl.pallas_call(
        _ada_kernel,
        grid=(n_out // tn,),
        in_specs=[pl.BlockSpec((ADA_ROWS, D_MODEL), lambda j: (0, 0)),
                  pl.BlockSpec((D_MODEL, tn), lambda j: (0, j)),
                  pl.BlockSpec((1, tn), lambda j: (0, j))],
        out_specs=pl.BlockSpec((ADA_ROWS, tn), lambda j: (0, j)),
        out_shape=jax.ShapeDtypeStruct((ADA_ROWS, n_out), F32),
        name="ada",
    )(c_all, w_ada, b_ada.reshape(1, n_out))


def _rms_mod(x, g, scale, shift):
    ms = jnp.mean(x * x, axis=-1, keepdims=True)
    return (x * lax.rsqrt(ms + EPS) * g) * (1.0 + scale) + shift


def _inproj_kernel(meta_ref, xp_ref, xs_ref, mod_ref, g_ref, w_ref, u_ref, q_ref, k_ref, v_ref,
                   *, n_prompt_tiles):
    i = pl.program_id(0)

    def body(x_ref):
        mod = mod_ref[0]
        h = _rms_mod(x_ref[...], g_ref[...], mod[:, D_MODEL:2 * D_MODEL], mod[:, 0:D_MODEL])
        z = jnp.dot(h.astype(BF16), w_ref[...], preferred_element_type=F32)
        u_ref[...] = z[:, :POOL_WIDTH]
        q_ref[...] = (z[:, POOL_WIDTH:POOL_WIDTH + ATTN_WIDTH] * (HEAD_DIM ** -0.5)).astype(BF16)
        k_ref[...] = z[:, POOL_WIDTH + ATTN_WIDTH:POOL_WIDTH + ATTN_WIDTH + KV_WIDTH].astype(BF16)
        v_ref[...] = z[:, POOL_WIDTH + ATTN_WIDTH + KV_WIDTH:].astype(BF16)

    pl.when(i < n_prompt_tiles)(lambda: body(xp_ref))
    pl.when(i >= n_prompt_tiles)(lambda: body(xs_ref))


def _tile_meta(seqs, tile):
    rows = []
    sid = 0
    for count, length in seqs:
        for _ in range(count):
            n = length // tile
            for t in range(n):
                rows.append((sid, int(t == 0), int(t == n - 1), t * tile, length))
            sid += 1
    return np.asarray(rows, np.int32).T.copy()


def _inproj(xp, xs, mod3, g1, w_in_bf, seqs):
    n_p, n_s = xp.shape[0], xs.shape[0]
    n = n_p + n_s
    npt = n_p // T_IN
    meta = jnp.asarray(_tile_meta(seqs, T_IN))
    tok = lambda w: pl.BlockSpec((T_IN, w), lambda i, m: (i, 0))
    return pl.pallas_call(
        functools.partial(_inproj_kernel, n_prompt_tiles=npt),
        grid_spec=pltpu.PrefetchScalarGridSpec(
            num_scalar_prefetch=1,
            grid=(n // T_IN,),
            in_specs=[
                pl.BlockSpec((T_IN, D_MODEL), lambda i, m: (jnp.minimum(i, npt - 1), 0)),
                pl.BlockSpec((T_IN, D_MODEL), lambda i, m: (jnp.maximum(i - npt, 0), 0)),
                pl.BlockSpec((1, 1, 6 * D_MODEL), lambda i, m: (m[0, i], 0, 0)),
                pl.BlockSpec((1, D_MODEL), lambda i, m: (0, 0)),
                pl.BlockSpec((D_MODEL, D_IN_PROJ), lambda i, m: (0, 0)),
            ],
            out_specs=[tok(POOL_WIDTH), tok(ATTN_WIDTH), tok(KV_WIDTH), tok(KV_WIDTH)],
        ),
        out_shape=[jax.ShapeDtypeStruct((n, POOL_WIDTH), F32),
                   jax.ShapeDtypeStruct((n, ATTN_WIDTH), BF16),
                   jax.ShapeDtypeStruct((n, KV_WIDTH), BF16),
                   jax.ShapeDtypeStruct((n, KV_WIDTH), BF16)],
        compiler_params=pltpu.CompilerParams(dimension_semantics=("arbitrary",),
                                             vmem_limit_bytes=VMEM_LIMIT),
        name="inproj",
    )(meta, xp, xs, mod3, g1, w_in_bf)


def _pool_mixer(uext_ref, uc, pos, seq_len, wpool_ref, bpool_ref, pscale_ref):
    outs = []
    for gi, w in enumerate(POOL_WINDOWS):
        c0 = gi * POOL_GROUP
        half = w // 2
        acc = uext_ref[pl.ds(HALO - half, T_MIX), pl.ds(c0, POOL_GROUP)]
        for o in range(-half + 1, half):
            acc = acc + uext_ref[pl.ds(HALO + o, T_MIX), pl.ds(c0, POOL_GROUP)]
        lo = jnp.maximum(pos - half, 0)
        hi = jnp.minimum(pos + half, seq_len)
        cnt = (hi - lo).astype(F32)
        d = acc / cnt - uc[:, c0:c0 + POOL_GROUP]
        y = jnp.dot(d.astype(BF16), wpool_ref[gi], preferred_element_type=F32)
        y = (y + bpool_ref[:, c0:c0 + POOL_GROUP]) * pscale_ref[:, c0:c0 + POOL_GROUP]
        outs.append(y)
    return jnp.concatenate(outs, axis=1)


def _banded_attention(q, kw, vw, first, last, sink_ref):
    t = T_MIX
    r = lax.broadcasted_iota(I32, (t, 3 * t), 0)
    c = lax.broadcasted_iota(I32, (t, 3 * t), 1)
    dist = jnp.abs(r - (c - t))
    mask = (dist <= WINDOW) & ((c >= t) | jnp.logical_not(first)) & ((c < 2 * t) | jnp.logical_not(last))
    distf = dist.astype(F32)
    outs = []
    for g in range(N_KV_HEADS):
        kg = kw[:, g * HEAD_DIM:(g + 1) * HEAD_DIM]
        vg = vw[:, g * HEAD_DIM:(g + 1) * HEAD_DIM]
        for hh in range(Q_PER_KV):
            h = g * Q_PER_KV + hh
            qh = q[:, h * HEAD_DIM:(h + 1) * HEAD_DIM]
            s = lax.dot_general(qh, kg, (((1,), (1,)), ((), ())), preferred_element_type=F32)
            s = jnp.where(mask, s - ALIBI_SLOPES[h] * distf, NEG_INF)
            sink = sink_ref[:, h:h + 1]
            m = jnp.maximum(jnp.max(s, axis=-1, keepdims=True), sink)
            p = jnp.exp(s - m)
            den = jnp.sum(p, axis=-1, keepdims=True) + jnp.exp(sink - m)
            o = jnp.dot(p.astype(BF16), vg, preferred_element_type=F32) / den
            outs.append(o)
    return jnp.concatenate(outs, axis=1)


def _route(biased, scores):
    t = biased.shape[1]
    rowf = lax.broadcasted_iota(I32, (N_EXPERTS, t), 0).astype(F32)
    ninf = float("-inf")
    gs = []
    for g in range(N_EXPERT_GROUPS):
        blk = biased[g * GROUP_SIZE:(g + 1) * GROUP_SIZE, :]
        rf = rowf[g * GROUP_SIZE:(g + 1) * GROUP_SIZE, :]
        m1 = jnp.max(blk, axis=0, keepdims=True)
        i1 = jnp.min(jnp.where(blk == m1, rf, float(N_EXPERTS)), axis=0, keepdims=True)
        m2 = jnp.max(jnp.where(rf == i1, ninf, blk), axis=0, keepdims=True)
        gs.append(m1 + m2)
    keep = []
    for g in range(N_EXPERT_GROUPS):
        beat = jnp.zeros((1, t), F32)
        for g2 in range(N_EXPERT_GROUPS):
            if g2 == g:
                continue
            better = (gs[g2] >= gs[g]) if g2 < g else (gs[g2] > gs[g])
            beat = beat + better.astype(F32)
        keep.append(jnp.broadcast_to(beat < float(TOPK_GROUPS), (GROUP_SIZE, t)))
    emask = jnp.concatenate(keep, axis=0)
    masked = jnp.where(emask, biased, NEG_INF)
    idx_rows, w_rows = [], []
    sel_any = jnp.zeros((N_EXPERTS, t), F32)
    for _ in range(TOP_K):
        m = jnp.max(masked, axis=0, keepdims=True)
        ik = jnp.min(jnp.where(masked == m, rowf, float(N_EXPERTS)), axis=0, keepdims=True)
        sel = rowf == ik
        w_rows.append(jnp.sum(jnp.where(sel, scores, 0.0), axis=0, keepdims=True))
        idx_rows.append(ik)
        masked = jnp.where(sel, ninf, masked)
        sel_any = sel_any + sel.astype(F32)
    return idx_rows, w_rows, sel_any, rowf


def _mixer_kernel(meta_ref, xp_ref, xs_ref, mod_ref, up_ref, uc_ref, un_ref, q_ref,
                  kp_ref, kc_ref, kn_ref, vp_ref, vc_ref, vn_ref,
                  wpool_ref, bpool_ref, pscale_ref, sink_ref, wout_ref, g2_ref,
                  wrt_ref, rbias_ref, wsgu_ref, wsd_ref, tri_ref,
                  xa_ref, h2p_ref, idx_ref, rank_ref, wts_ref, cnt_ref,
                  uext_ref, base_ref, *, n_prompt_tiles):
    i = pl.program_id(0)
    first = meta_ref[1, i] == 1
    last = meta_ref[2, i] == 1
    pos0 = meta_ref[3, i]
    seq_len = meta_ref[4, i]
    mod = mod_ref[0]
    gt1 = mod[:, 2 * D_MODEL:3 * D_MODEL]
    sh2 = mod[:, 3 * D_MODEL:4 * D_MODEL]
    sc2 = mod[:, 4 * D_MODEL:5 * D_MODEL]
    gt2 = mod[:, 5 * D_MODEL:6 * D_MODEL]

    @pl.when(i == 0)
    def _():
        base_ref[...] = jnp.zeros_like(base_ref)

    uc = uc_ref[...]
    uext_ref[pl.ds(0, HALO), :] = jnp.where(first, 0.0, up_ref[...])
    uext_ref[pl.ds(HALO, T_MIX), :] = uc
    uext_ref[pl.ds(HALO + T_MIX, HALO), :] = jnp.where(last, 0.0, un_ref[...])
    pos = pos0 + lax.broadcasted_iota(I32, (T_MIX, 1), 0)
    a_pool = _pool_mixer(uext_ref, uc, pos, seq_len, wpool_ref, bpool_ref, pscale_ref)

    kw = jnp.concatenate([kp_ref[...], kc_ref[...], kn_ref[...]], axis=0)
    vw = jnp.concatenate([vp_ref[...], vc_ref[...], vn_ref[...]], axis=0)
    a_attn = _banded_attention(q_ref[...], kw, vw, first, last, sink_ref)

    a = jnp.concatenate([a_pool, a_attn], axis=1).astype(BF16)
    mix = jnp.dot(a, wout_ref[...], preferred_element_type=F32)

    def finish(x_ref):
        x1 = x_ref[...] + gt1 * mix
        h2 = _rms_mod(x1, g2_ref[...], sc2, sh2)
        h2b = h2.astype(BF16)
        planes = _pack_planes(h2)
        for j in range(N_PLANES):
            h2p_ref[j] = planes[j]
        gu = jnp.dot(h2b, wsgu_ref[...], preferred_element_type=F32)
        gate, up = gu[:, :D_SHARED], gu[:, D_SHARED:]
        act = (gate * jax.nn.sigmoid(gate)) * up
        shared = jnp.dot(act.astype(BF16), wsd_ref[...], preferred_element_type=F32)
        xa_ref[...] = x1 + gt2 * shared
        logits = lax.dot_general(wrt_ref[...], h2, (((1,), (1,)), ((), ())),
                                 precision=lax.Precision.HIGHEST, preferred_element_type=F32)
        scores = jax.nn.sigmoid(logits)
        biased = scores + rbias_ref[...]
        idx_rows, w_rows, sel_any, rowf = _route(biased, scores)
        wsum = w_rows[0]
        for wr in w_rows[1:]:
            wsum = wsum + wr
        wts_t = jnp.concatenate([wr / wsum * ROUTED_SCALE for wr in w_rows], axis=0)
        wpad = jnp.concatenate([wts_t, jnp.zeros((LANES - TOP_K, T_MIX), F32)], axis=0)
        wts_ref[...] = wpad.T
        idx_ref[...] = jnp.concatenate(idx_rows, axis=0).astype(I32)
        pref = jnp.dot(sel_any.astype(BF16), tri_ref[...], preferred_element_type=F32)
        before = base_ref[...] + pref[:, :T_MIX]
        ranks = [jnp.sum(jnp.where(rowf == ik, before, 0.0), axis=0, keepdims=True) for ik in idx_rows]
        rank_ref[...] = jnp.concatenate(ranks, axis=0).astype(I32)
        new_base = base_ref[...] + pref[:, T_MIX:]
        base_ref[...] = new_base
        cnt_ref[...] = new_base

    pl.when(i < n_prompt_tiles)(lambda: finish(xp_ref))
    pl.when(i >= n_prompt_tiles)(lambda: finish(xs_ref))


def _mixer(xp, xs, mod3, u, q, k, v, wpool_bf, bpool, pscale, sink, wout_bf, g2, wrt, rbias,
           wsgu_bf, wsd_bf, seqs):
    n_p, n_s = xp.shape[0], xs.shape[0]
    n = n_p + n_s
    t = T_MIX
    nt = n // t
    npt = n_p // t
    meta = jnp.asarray(_tile_meta(seqs, t))
    tri = np.concatenate([np.triu(np.ones((t, t), np.float32), 1), np.ones((t, t), np.float32)], axis=1)
    tri = jnp.asarray(tri, BF16)
    cur = lambda w: pl.BlockSpec((t, w), lambda i, m: (i, 0))
    prev = lambda w: pl.BlockSpec((t, w), lambda i, m: (jnp.maximum(i - 1, 0), 0))
    nxt = lambda w: pl.BlockSpec((t, w), lambda i, m: (jnp.minimum(i + 1, nt - 1), 0))
    full = lambda a: pl.BlockSpec(a.shape, lambda i, m: (0,) * a.ndim)
    hb = t // HALO
    in_specs = [
        pl.BlockSpec((t, D_MODEL), lambda i, m: (jnp.minimum(i, npt - 1), 0)),
        pl.BlockSpec((t, D_MODEL), lambda i, m: (jnp.maximum(i - npt, 0), 0)),
        pl.BlockSpec((1, 1, 6 * D_MODEL), lambda i, m: (m[0, i], 0, 0)),
        pl.BlockSpec((HALO, POOL_WIDTH), lambda i, m: (jnp.maximum(i * hb - 1, 0), 0)),
        cur(POOL_WIDTH),
        pl.BlockSpec((HALO, POOL_WIDTH), lambda i, m: (jnp.minimum((i + 1) * hb, n // HALO - 1), 0)),
        cur(ATTN_WIDTH),
        prev(KV_WIDTH), cur(KV_WIDTH), nxt(KV_WIDTH),
        prev(KV_WIDTH), cur(KV_WIDTH), nxt(KV_WIDTH),
        full(wpool_bf), full(bpool), full(pscale), full(sink), full(wout_bf), full(g2),
        full(wrt), full(rbias), full(wsgu_bf), full(wsd_bf), full(tri),
    ]
    out_specs = [
        cur(D_MODEL),
        pl.BlockSpec((N_PLANES, t, LANES), lambda i, m: (0, i, 0)),
        pl.BlockSpec((TOP_K, t), lambda i, m: (0, i)),
        pl.BlockSpec((TOP_K, t), lambda i, m: (0, i)),
        cur(LANES),
        pl.BlockSpec((N_EXPERTS, t), lambda i, m: (0, 0)),
    ]
    out_shape = [
        jax.ShapeDtypeStruct((n, D_MODEL), F32),
        jax.ShapeDtypeStruct((N_PLANES, n, LANES), I32),
        jax.ShapeDtypeStruct((TOP_K, n), I32),
        jax.ShapeDtypeStruct((TOP_K, n), I32),
        jax.ShapeDtypeStruct((n, LANES), F32),
        jax.ShapeDtypeStruct((N_EXPERTS, t), F32),
    ]
    return pl.pallas_call(
        functools.partial(_mixer_kernel, n_prompt_tiles=npt),
        grid_spec=pltpu.PrefetchScalarGridSpec(
            num_scalar_prefetch=1, grid=(nt,), in_specs=in_specs, out_specs=out_specs,
            scratch_shapes=[pltpu.VMEM((t + 2 * HALO, POOL_WIDTH), F32),
                            pltpu.VMEM((N_EXPERTS, t), F32)]),
        out_shape=out_shape,
        compiler_params=pltpu.CompilerParams(dimension_semantics=("arbitrary",),
                                             vmem_limit_bytes=VMEM_LIMIT),
        name="mixer",
    )(meta, xp, xs, mod3, u, u, u, q, k, k, k, v, v, v, wpool_bf, bpool, pscale, sink, wout_bf, g2,
      wrt, rbias, wsgu_bf, wsd_bf, tri)


def _dest_kernel(idx_ref, rank_ref, pstart_ref, o_ref, *, n_rows):
    rowi = lax.broadcasted_iota(I32, (N_EXPERTS, T_DEST), 0)
    pstart = pstart_ref[...]
    rows = []
    for kk in range(TOP_K):
        sel = rowi == idx_ref[kk:kk + 1, :]
        start = jnp.sum(jnp.where(sel, pstart, 0.0), axis=0, keepdims=True)
        rows.append(start.astype(I32) + rank_ref[kk:kk + 1, :])
    dest = jnp.concatenate(rows, axis=0)
    for j in range(N_PLANES):
        o_ref[:, j, :] = dest + j * n_rows


def _dest(idx_t, rank_t, pstart_col, n_rows):
    n = idx_t.shape[1]
    return pl.pallas_call(
        functools.partial(_dest_kernel, n_rows=n_rows),
        grid=(n // T_DEST,),
        in_specs=[pl.BlockSpec((TOP_K, T_DEST), lambda i: (0, i)),
                  pl.BlockSpec((TOP_K, T_DEST), lambda i: (0, i)),
                  pl.BlockSpec((N_EXPERTS, 1), lambda i: (0, 0))],
        out_specs=pl.BlockSpec((TOP_K, N_PLANES, T_DEST), lambda i: (0, 0, i)),
        out_shape=jax.ShapeDtypeStruct((TOP_K, N_PLANES, n), I32),
        name="dest",
    )(idx_t, rank_t, pstart_col)


def _sc_mesh():
    return plsc.VectorSubcoreMesh(core_axis_name="c", subcore_axis_name="s")


def _sc_worker():
    return lax.axis_index("s") * 2 + lax.axis_index("c")


def _sc_scatter_rows(table, didx, n_out_rows):
    m = table.shape[0]
    per_worker = m // SC_ROWS // SC_WORKERS

    @functools.partial(
        pl.kernel, mesh=_sc_mesh(),
        out_type=jax.ShapeDtypeStruct((n_out_rows, LANES), I32),
        scratch_types=[pltpu.VMEM((SC_ROWS, LANES), I32), pltpu.VMEM((TOP_K, SC_ROWS), I32)])
    def run(table_hbm, didx_hbm, out_hbm, rows_v, idx_v):
        wid = _sc_worker()

        @pl.loop(0, per_worker)
        def _(step):
            off = pl.multiple_of((wid * per_worker + step) * SC_ROWS, SC_ROWS)
            pltpu.sync_copy(table_hbm.at[pl.ds(off, SC_ROWS)], rows_v)
            pltpu.sync_copy(didx_hbm.at[:, pl.ds(off, SC_ROWS)], idx_v)
            for kk in range(TOP_K):
                pltpu.sync_copy(rows_v, out_hbm.at[idx_v.at[kk]])

    return run(table, didx)


def _sc_gather_rows(table, didx):
    m = didx.shape[1]
    per_worker = m // SC_ROWS // SC_WORKERS

    @functools.partial(
        pl.kernel, mesh=_sc_mesh(),
        out_type=jax.ShapeDtypeStruct((TOP_K, m, LANES), I32),
        scratch_types=[pltpu.VMEM((SC_ROWS, LANES), I32), pltpu.VMEM((TOP_K, SC_ROWS), I32)])
    def run(table_hbm, didx_hbm, out_hbm, rows_v, idx_v):
        wid = _sc_worker()

        @pl.loop(0, per_worker)
        def _(step):
            off = pl.multiple_of((wid * per_worker + step) * SC_ROWS, SC_ROWS)
            pltpu.sync_copy(didx_hbm.at[:, pl.ds(off, SC_ROWS)], idx_v)
            for kk in range(TOP_K):
                pltpu.sync_copy(table_hbm.at[idx_v.at[kk]], rows_v)
                pltpu.sync_copy(rows_v, out_hbm.at[kk, pl.ds(off, SC_ROWS)])

    return run(table, didx)


def _experts_kernel(blk_e_ref, blk_valid_ref, nused_ref, x_ref, wg_ref, wu_ref, wd_ref, y_ref,
                    wgu_s, wd_s):
    b = pl.program_id(0)

    @pl.when(b < nused_ref[0])
    def _():
        e = blk_e_ref[b]
        e_prev = blk_e_ref[jnp.maximum(b - 1, 0)]

        @pl.when((b == 0) | (e != e_prev))
        def _():
            for j in range(N_PLANES):
                hi = pl.ds(j * LANES, LANES)
                lo = pl.ds(HALF + j * LANES, LANES)
                wgu_s[j, pl.ds(0, LANES), pl.ds(0, D_EXPERT)] = wg_ref[0, hi, :].astype(BF16)
                wgu_s[j, pl.ds(LANES, LANES), pl.ds(0, D_EXPERT)] = wg_ref[0, lo, :].astype(BF16)
                wgu_s[j, pl.ds(0, LANES), pl.ds(D_EXPERT, D_EXPERT)] = wu_ref[0, hi, :].astype(BF16)
                wgu_s[j, pl.ds(LANES, LANES), pl.ds(D_EXPERT, D_EXPERT)] = wu_ref[0, lo, :].astype(BF16)
            wd_s[...] = wd_ref[0].astype(BF16)

        live = lax.broadcasted_iota(I32, (BM, 2 * LANES), 0) < blk_valid_ref[b]
        acc = jnp.zeros((BM, 2 * D_EXPERT), F32)
        for j in range(N_PLANES):
            hi, lo = _unpack_plane(x_ref[j])
            xj = jnp.where(live, jnp.concatenate([hi, lo], axis=1), 0.0).astype(BF16)
            acc = acc + jnp.dot(xj, wgu_s[j], preferred_element_type=F32)
        gate, up = acc[:, :D_EXPERT], acc[:, D_EXPERT:]
        act = (gate * jax.nn.sigmoid(gate)) * up
        y = jnp.dot(act.astype(BF16), wd_s[...], preferred_element_type=F32)
        planes = _pack_planes(y)
        for j in range(N_PLANES):
            y_ref[j] = planes[j]


def _experts(xs_planes, w_gate, w_up, w_down, blk_e, blk_valid, nused):
    n_rows = xs_planes.shape[1]
    nblk = n_rows // BM
    row_map = lambda b, be, bv, nu: (0, jnp.minimum(b, nu[0] - 1), 0)
    w_map = lambda b, be, bv, nu: (be[jnp.minimum(b, nu[0] - 1)], 0, 0)
    return pl.pallas_call(
        _experts_kernel,
        grid_spec=pltpu.PrefetchScalarGridSpec(
            num_scalar_prefetch=3, grid=(nblk,),
            in_specs=[pl.BlockSpec((N_PLANES, BM, LANES), row_map),
                      pl.BlockSpec((1, D_MODEL, D_EXPERT), w_map),
                      pl.BlockSpec((1, D_MODEL, D_EXPERT), w_map),
                      pl.BlockSpec((1, D_EXPERT, D_MODEL), w_map)],
            out_specs=pl.BlockSpec((N_PLANES, BM, LANES), row_map),
            scratch_shapes=[pltpu.VMEM((N_PLANES, 2 * LANES, 2 * D_EXPERT), BF16),
                            pltpu.VMEM((D_EXPERT, D_MODEL), BF16)]),
        out_shape=jax.ShapeDtypeStruct((N_PLANES, n_rows, LANES), I32),
        compiler_params=pltpu.CompilerParams(dimension_semantics=("arbitrary",),
                                             vmem_limit_bytes=VMEM_LIMIT),
        name="experts",
    )(blk_e, blk_valid, nused, xs_planes, w_gate, w_up, w_down)


def _final_kernel(xa_ref, yk_ref, wts_ref, mod_ref, g_ref, o_ref):
    wts = wts_ref[...]
    his = [jnp.zeros((T_FIN, LANES), F32) for _ in range(N_PLANES)]
    los = [jnp.zeros((T_FIN, LANES), F32) for _ in range(N_PLANES)]
    for kk in range(TOP_K):
        wk = wts[:, kk:kk + 1]
        for j in range(N_PLANES):
            hi, lo = _unpack_plane(yk_ref[kk, j])
            his[j] = his[j] + wk * hi
            los[j] = los[j] + wk * lo
    routed = jnp.concatenate(his + los, axis=1)
    gt2 = mod_ref[0][:, 5 * D_MODEL:6 * D_MODEL]
    x = xa_ref[...] + gt2 * routed
    ms = jnp.mean(x * x, axis=-1, keepdims=True)
    o_ref[...] = x * lax.rsqrt(ms + EPS) * g_ref[...]


def _final(xa, yk, wts, mod3, g_final, tile_off, n_tok, seq_len, sid0):
    tiles_per_seq = seq_len // T_FIN
    return pl.pallas_call(
        _final_kernel,
        grid=(n_tok // T_FIN,),
        in_specs=[pl.BlockSpec((T_FIN, D_MODEL), lambda i: (i + tile_off, 0)),
                  pl.BlockSpec((TOP_K, N_PLANES, T_FIN, LANES), lambda i: (0, 0, i + tile_off, 0)),
                  pl.BlockSpec((T_FIN, LANES), lambda i: (i + tile_off, 0)),
                  pl.BlockSpec((1, 1, 6 * D_MODEL), lambda i: (sid0 + i // tiles_per_seq, 0, 0)),
                  pl.BlockSpec((1, D_MODEL), lambda i: (0, 0))],
        out_specs=pl.BlockSpec((T_FIN, D_MODEL), lambda i: (i, 0)),
        out_shape=jax.ShapeDtypeStruct((n_tok, D_MODEL), F32),
        compiler_params=pltpu.CompilerParams(dimension_semantics=("arbitrary",),
                                             vmem_limit_bytes=VMEM_LIMIT),
        name="final",
    )(xa, yk, wts, mod3, g_final)


def kernel(x_prompt, x_sample, c_prompt, c_sample, w_ada, b_ada, g_norm1, w_in, w_pool, b_pool, pool_scale, attn_sink, w_out, g_norm2, w_router, router_bias, w_gate, w_up, w_down, ws_gate, ws_up, ws_down, g_final):
    assert w_ada.shape[0] == 1, "one layer"
    bp, sp, d = x_prompt.shape
    bs, ss, _ = x_sample.shape
    assert d == D_MODEL and bp + bs <= ADA_ROWS
    seqs = ((bp, sp), (bs, ss))
    n_p, n_s = bp * sp, bs * ss
    n = n_p + n_s
    xp = x_prompt.reshape(n_p, d)
    xs = x_sample.reshape(n_s, d)

    c_all = jnp.concatenate([c_prompt, c_sample, jnp.zeros((ADA_ROWS - bp - bs, d), F32)], axis=0)
    mod3 = _ada(c_all, w_ada[0], b_ada[0]).reshape(ADA_ROWS, 1, 6 * d)

    u, q, k, v = _inproj(xp, xs, mod3, g_norm1[0].reshape(1, d), w_in[0].astype(BF16), seqs)

    wsgu = jnp.concatenate([ws_gate[0], ws_up[0]], axis=1).astype(BF16)
    xa, h2p, idx_t, rank_t, wts, cnt = _mixer(
        xp, xs, mod3, u, q, k, v,
        w_pool[0].astype(BF16), b_pool[0].reshape(1, POOL_WIDTH), pool_scale[0].reshape(1, POOL_WIDTH),
        attn_sink[0].reshape(1, N_HEADS), w_out[0].astype(BF16), g_norm2[0].reshape(1, d),
        w_router[0].T, router_bias[0].reshape(N_EXPERTS, 1), wsgu, ws_down[0].astype(BF16), seqs)

    n_rows = n * TOP_K + N_EXPERTS * BM
    nblk = n_rows // BM
    counts = cnt[:, 0].astype(I32)
    padded = (counts + BM - 1) // BM * BM
    pend = jnp.cumsum(padded)
    pstart = pend - padded
    blk_start = jnp.arange(nblk, dtype=I32) * BM
    blk_e = jnp.clip(jnp.searchsorted(pend, blk_start, side="right"), 0, N_EXPERTS - 1).astype(I32)
    blk_valid = jnp.clip(counts[blk_e] - (blk_start - pstart[blk_e]), 0, BM).astype(I32)
    nused = (pend[-1:] // BM).astype(I32)

    didx = _dest(idx_t, rank_t, pstart.astype(F32).reshape(N_EXPERTS, 1), n_rows)
    didx = didx.reshape(TOP_K, N_PLANES * n)

    xs_rows = _sc_scatter_rows(h2p.reshape(N_PLANES * n, LANES), didx, N_PLANES * n_rows)
    ys_rows = _experts(xs_rows.reshape(N_PLANES, n_rows, LANES), w_gate[0], w_up[0], w_down[0],
                       blk_e, blk_valid, nused)
    yk = _sc_gather_rows(ys_rows.reshape(N_PLANES * n_rows, LANES), didx)
    yk = yk.reshape(TOP_K, N_PLANES, n, LANES)

    gf = g_final.reshape(1, d)
    y_p = _final(xa, yk, wts, mod3, gf, 0, n_p, sp, 0)
    y_s = _final(xa, yk, wts, mod3, gf, n_p // T_FIN, n_s, ss, bp)
    return y_p.reshape(bp, sp, d), y_s.reshape(bs, ss, d)
```

```python
import functools

import numpy as np
import jax
import jax.numpy as jnp
from jax import lax
from jax.experimental import pallas as pl
from jax.experimental.pallas import tpu as pltpu
from jax.experimental.pallas import tpu_sc as plsc

F32 = jnp.float32
BF16 = jnp.bfloat16
I32 = jnp.int32

D_MODEL = 1024
POOL_WINDOWS = (2, 4, 8, 16)
POOL_WIDTH = 512
POOL_GROUP = 128
N_HEADS = 8
N_KV_HEADS = 2
HEAD_DIM = 64
Q_PER_KV = N_HEADS // N_KV_HEADS
ATTN_WIDTH = N_HEADS * HEAD_DIM
KV_WIDTH = N_KV_HEADS * HEAD_DIM
D_IN_PROJ = POOL_WIDTH + ATTN_WIDTH + 2 * KV_WIDTH
WINDOW = 128
N_EXPERTS = 256
TOP_K = 8
N_EXPERT_GROUPS = 8
GROUP_SIZE = N_EXPERTS // N_EXPERT_GROUPS
TOPK_GROUPS = 4
D_EXPERT = 256
D_SHARED = 256
ROUTED_SCALE = 2.5
EPS = 1e-6
NEG_INF = -1e30

LANES = 128
HALO = 8
N_PLANES = 4
HALF = D_MODEL // 2

T_IN = 512
T_MIX = 256
ATT_BLOCK = 128
T_DEST = 512
BM = 512
T_FIN = 256
SC_ROWS = 128
SC_WORKERS = 32
SC_NBUF = 4
SC_LAG = 2
ADA_ROWS = 16
VMEM_LIMIT = 48 * 1024 * 1024

ALIBI_SLOPES = tuple(float(2.0 ** (-8.0 * (h + 1) / N_HEADS)) for h in range(N_HEADS))


def _pack_planes(y):
    planes = []
    for j in range(N_PLANES):
        hi = y[:, j * LANES:(j + 1) * LANES].astype(BF16).astype(F32)
        lo = y[:, HALF + j * LANES:HALF + (j + 1) * LANES].astype(BF16).astype(F32)
        hb = lax.bitcast_convert_type(hi, jnp.uint32) & jnp.uint32(0xFFFF0000)
        lb = lax.bitcast_convert_type(lo, jnp.uint32) >> jnp.uint32(16)
        planes.append(lax.bitcast_convert_type(hb | lb, I32))
    return planes


def _unpack_plane(w):
    u = lax.bitcast_convert_type(w, jnp.uint32)
    hi = lax.bitcast_convert_type(u & jnp.uint32(0xFFFF0000), F32)
    lo = lax.bitcast_convert_type(u << jnp.uint32(16), F32)
    return hi, lo


def _ada_kernel(c_ref, w_ref, b_ref, o_ref):
    c = c_ref[...]
    s = c * jax.nn.sigmoid(c)
    o_ref[...] = jnp.dot(s, w_ref[...], precision=lax.Precision.HIGHEST,
                         preferred_element_type=F32) + b_ref[...]


def _ada(c_all, w_ada, b_ada):
    n_out = w_ada.shape[1]
    tn = 1024
    return pl.pallas_call(
        _ada_kernel,
        grid=(n_out // tn,),
        in_specs=[pl.BlockSpec((ADA_ROWS, D_MODEL), lambda j: (0, 0)),
                  pl.BlockSpec((D_MODEL, tn), lambda j: (0, j)),
                  pl.BlockSpec((1, tn), lambda j: (0, j))],
        out_specs=pl.BlockSpec((ADA_ROWS, tn), lambda j: (0, j)),
        out_shape=jax.ShapeDtypeStruct((ADA_ROWS, n_out), F32),
        name="ada",
    )(c_all, w_ada, b_ada.reshape(1, n_out))


def _rms_mod(x, g, scale, shift):
    ms = jnp.mean(x * x, axis=-1, keepdims=True)
    return (x * lax.rsqrt(ms + EPS) * g) * (1.0 + scale) + shift


def _inproj_kernel(meta_ref, xp_ref, xs_ref, mod_ref, g_ref, w_ref, u_ref, q_ref, k_ref, v_ref,
                   *, n_prompt_tiles):
    x = jnp.where(pl.program_id(0) < n_prompt_tiles, xp_ref[...], xs_ref[...])
    mod = mod_ref[0]
    h = _rms_mod(x, g_ref[...], mod[:, D_MODEL:2 * D_MODEL], mod[:, 0:D_MODEL])
    z = jnp.dot(h.astype(BF16), w_ref[...], preferred_element_type=F32)
    u_ref[...] = z[:, :POOL_WIDTH]
    q_ref[...] = (z[:, POOL_WIDTH:POOL_WIDTH + ATTN_WIDTH] * (HEAD_DIM ** -0.5)).astype(BF16)
    k_ref[...] = z[:, POOL_WIDTH + ATTN_WIDTH:POOL_WIDTH + ATTN_WIDTH + KV_WIDTH].astype(BF16)
    v_ref[...] = z[:, POOL_WIDTH + ATTN_WIDTH + KV_WIDTH:].astype(BF16)


def _tile_meta(seqs, tile):
    rows = []
    sid = 0
    for count, length in seqs:
        for _ in range(count):
            n = length // tile
            for t in range(n):
                rows.append((sid, int(t == 0), int(t == n - 1), t * tile, length))
            sid += 1
    return np.asarray(rows, np.int32).T.copy()


def _inproj(xp, xs, mod3, g1, w_in_bf, seqs):
    n_p, n_s = xp.shape[0], xs.shape[0]
    n = n_p + n_s
    npt = n_p // T_IN
    meta = jnp.asarray(_tile_meta(seqs, T_IN))
    tok = lambda w: pl.BlockSpec((T_IN, w), lambda i, m: (i, 0))
    return pl.pallas_call(
        functools.partial(_inproj_kernel, n_prompt_tiles=npt),
        grid_spec=pltpu.PrefetchScalarGridSpec(
            num_scalar_prefetch=1,
            grid=(n // T_IN,),
            in_specs=[
                pl.BlockSpec((T_IN, D_MODEL), lambda i, m: (jnp.minimum(i, npt - 1), 0)),
                pl.BlockSpec((T_IN, D_MODEL), lambda i, m: (jnp.maximum(i - npt, 0), 0)),
                pl.BlockSpec((1, 1, 6 * D_MODEL), lambda i, m: (m[0, i], 0, 0)),
                pl.BlockSpec((1, D_MODEL), lambda i, m: (0, 0)),
                pl.BlockSpec((D_MODEL, D_IN_PROJ), lambda i, m: (0, 0)),
            ],
            out_specs=[tok(POOL_WIDTH), tok(ATTN_WIDTH), tok(KV_WIDTH), tok(KV_WIDTH)],
        ),
        out_shape=[jax.ShapeDtypeStruct((n, POOL_WIDTH), F32),
                   jax.ShapeDtypeStruct((n, ATTN_WIDTH), BF16),
                   jax.ShapeDtypeStruct((n, KV_WIDTH), BF16),
                   jax.ShapeDtypeStruct((n, KV_WIDTH), BF16)],
        compiler_params=pltpu.CompilerParams(dimension_semantics=("arbitrary",),
                                             vmem_limit_bytes=VMEM_LIMIT),
        name="inproj",
    )(meta, xp, xs, mod3, g1, w_in_bf)


def _pool_mixer(uext_ref, uc, pos, seq_len, wpool_ref, bpool_ref, pscale_ref):
    outs = []
    for gi, w in enumerate(POOL_WINDOWS):
        c0 = gi * POOL_GROUP
        half = w // 2
        acc = uext_ref[pl.ds(HALO - half, T_MIX), pl.ds(c0, POOL_GROUP)]
        for o in range(-half + 1, half):
            acc = acc + uext_ref[pl.ds(HALO + o, T_MIX), pl.ds(c0, POOL_GROUP)]
        lo = jnp.maximum(pos - half, 0)
        hi = jnp.minimum(pos + half, seq_len)
        cnt = (hi - lo).astype(F32)
        d = acc / cnt - uc[:, c0:c0 + POOL_GROUP]
        y = jnp.dot(d.astype(BF16), wpool_ref[gi], preferred_element_type=F32)
        y = (y + bpool_ref[:, c0:c0 + POOL_GROUP]) * pscale_ref[:, c0:c0 + POOL_GROUP]
        outs.append(y)
    return jnp.concatenate(outs, axis=1)


def _banded_attention(q, kw, vw, first, last, sink_ref):
    t = ATT_BLOCK
    low = lax.broadcasted_iota(I32, (t, LANES), 1) < HEAD_DIM
    zero = jnp.zeros((t, LANES), BF16)
    qs = []
    for h in range(N_HEADS):
        blk = q[:, (h % Q_PER_KV) * LANES:(h % Q_PER_KV + 1) * LANES]
        qs.append(jnp.where(low, blk, zero) if h < Q_PER_KV else jnp.where(low, zero, blk))
    s = lax.dot_general(jnp.concatenate(qs, axis=0), kw, (((1,), (1,)), ((), ())),
                        preferred_element_type=F32)
    r = lax.broadcasted_iota(I32, (t, 3 * t), 0)
    c = lax.broadcasted_iota(I32, (t, 3 * t), 1)
    dist = jnp.abs(r - (c - t))
    mask = (dist <= WINDOW) & ((c >= t) | jnp.logical_not(first)) & ((c < 2 * t) | jnp.logical_not(last))
    distf = dist.astype(F32)
    ps, dens = [], []
    for h in range(N_HEADS):
        sh = jnp.where(mask, s[h * t:(h + 1) * t] - ALIBI_SLOPES[h] * distf, NEG_INF)
        sink = sink_ref[:, h:h + 1]
        m = jnp.maximum(jnp.max(sh, axis=-1, keepdims=True), sink)
        p = jnp.exp(sh - m)
        dens.append(jnp.sum(p, axis=-1, keepdims=True) + jnp.exp(sink - m))
        ps.append(p.astype(BF16))
    o = jnp.dot(jnp.concatenate(ps, axis=0), vw, preferred_element_type=F32)
    oh = [o[h * t:(h + 1) * t] / dens[h] for h in range(N_HEADS)]
    return jnp.concatenate([jnp.where(low, oh[cb], oh[Q_PER_KV + cb]) for cb in range(Q_PER_KV)], axis=1)


def _route(biased, scores):
    t = biased.shape[1]
    rowf = lax.broadcasted_iota(I32, (N_EXPERTS, t), 0).astype(F32)
    ninf = float("-inf")
    gs = []
    for g in range(N_EXPERT_GROUPS):
        blk = biased[g * GROUP_SIZE:(g + 1) * GROUP_SIZE, :]
        rf = rowf[g * GROUP_SIZE:(g + 1) * GROUP_SIZE, :]
        m1 = jnp.max(blk, axis=0, keepdims=True)
        i1 = jnp.min(jnp.where(blk == m1, rf, float(N_EXPERTS)), axis=0, keepdims=True)
        m2 = jnp.max(jnp.where(rf == i1, ninf, blk), axis=0, keepdims=True)
        gs.append(m1 + m2)
    keep = []
    for g in range(N_EXPERT_GROUPS):
        beat = jnp.zeros((1, t), F32)
        for g2 in range(N_EXPERT_GROUPS):
            if g2 == g:
                continue
            better = (gs[g2] >= gs[g]) if g2 < g else (gs[g2] > gs[g])
            beat = beat + better.astype(F32)
        keep.append(jnp.broadcast_to(beat < float(TOPK_GROUPS), (GROUP_SIZE, t)))
    emask = jnp.concatenate(keep, axis=0)
    masked = jnp.where(emask, biased, NEG_INF)
    idx_rows, w_rows = [], []
    sel_any = jnp.zeros((N_EXPERTS, t), F32)
    for _ in range(TOP_K):
        m = jnp.max(masked, axis=0, keepdims=True)
        ik = jnp.min(jnp.where(masked == m, rowf, float(N_EXPERTS)), axis=0, keepdims=True)
        sel = rowf == ik
        w_rows.append(jnp.sum(jnp.where(sel, scores, 0.0), axis=0, keepdims=True))
        idx_rows.append(ik)
        masked = jnp.where(sel, ninf, masked)
        sel_any = sel_any + sel.astype(F32)
    return idx_rows, w_rows, sel_any, rowf


def _mixer_kernel(meta_ref, xp_ref, xs_ref, mod_ref, up_ref, uc_ref, un_ref, q_ref,
                  kp_ref, kc_ref, kn_ref, vp_ref, vc_ref, vn_ref,
                  wpool_ref, bpool_ref, pscale_ref, sink_ref, wout_ref, g2_ref,
                  wr3_ref, rbias_ref, wsgu_ref, wsd_ref, tri_ref,
                  xa_ref, h2p_ref, idx_ref, rank_ref, wts_ref, cnt_ref,
                  uext_ref, kext_ref, vext_ref, base_ref, *, n_prompt_tiles):
    i = pl.program_id(0)
    first = meta_ref[1, i] == 1
    last = meta_ref[2, i] == 1
    pos0 = meta_ref[3, i]
    seq_len = meta_ref[4, i]
    mod = mod_ref[0]
    gt1 = mod[:, 2 * D_MODEL:3 * D_MODEL]
    sh2 = mod[:, 3 * D_MODEL:4 * D_MODEL]
    sc2 = mod[:, 4 * D_MODEL:5 * D_MODEL]
    gt2 = mod[:, 5 * D_MODEL:6 * D_MODEL]

    @pl.when(i == 0)
    def _():
        base_ref[...] = jnp.zeros_like(base_ref)

    uc = uc_ref[...]
    uext_ref[pl.ds(0, HALO), :] = jnp.where(first, 0.0, up_ref[...])
    uext_ref[pl.ds(HALO, T_MIX), :] = uc
    uext_ref[pl.ds(HALO + T_MIX, HALO), :] = jnp.where(last, 0.0, un_ref[...])
    pos = pos0 + lax.broadcasted_iota(I32, (T_MIX, 1), 0)
    a_pool = _pool_mixer(uext_ref, uc, pos, seq_len, wpool_ref, bpool_ref, pscale_ref)

    t = ATT_BLOCK
    kext_ref[pl.ds(0, t), :] = kp_ref[...]
    kext_ref[pl.ds(t, T_MIX), :] = kc_ref[...]
    kext_ref[pl.ds(t + T_MIX, t), :] = kn_ref[...]
    vext_ref[pl.ds(0, t), :] = vp_ref[...]
    vext_ref[pl.ds(t, T_MIX), :] = vc_ref[...]
    vext_ref[pl.ds(t + T_MIX, t), :] = vn_ref[...]
    n_sub = T_MIX // t
    attn = []
    for sub in range(n_sub):
        attn.append(_banded_attention(
            q_ref[pl.ds(sub * t, t), :], kext_ref[pl.ds(sub * t, 3 * t), :], vext_ref[pl.ds(sub * t, 3 * t), :],
            first if sub == 0 else False, last if sub == n_sub - 1 else False, sink_ref))
    a_attn = jnp.concatenate(attn, axis=0)

    a = jnp.concatenate([a_pool, a_attn], axis=1).astype(BF16)
    mix = jnp.dot(a, wout_ref[...], preferred_element_type=F32)

    x = jnp.where(i < n_prompt_tiles, xp_ref[...], xs_ref[...])
    x1 = x + gt1 * mix
    h2 = _rms_mod(x1, g2_ref[...], sc2, sh2)
    h2b = h2.astype(BF16)
    planes = _pack_planes(h2)
    for j in range(N_PLANES):
        h2p_ref[j] = planes[j]
    gu = jnp.dot(h2b, wsgu_ref[...], preferred_element_type=F32)
    gate, up = gu[:, :D_SHARED], gu[:, D_SHARED:]
    act = (gate * jax.nn.sigmoid(gate)) * up
    shared = jnp.dot(act.astype(BF16), wsd_ref[...], preferred_element_type=F32)
    xa_ref[...] = x1 + gt2 * shared
    h2lo = (h2 - h2b.astype(F32)).astype(BF16)
    logits = lax.dot_general(wr3_ref[...], jnp.concatenate([h2b, h2b, h2lo], axis=1),
                             (((1,), (1,)), ((), ())), preferred_element_type=F32)
    scores = jax.nn.sigmoid(logits)
    biased = scores + rbias_ref[...]
    routed = [_route(biased[:, c * LANES:(c + 1) * LANES], scores[:, c * LANES:(c + 1) * LANES])
              for c in range(T_MIX // LANES)]
    for c, (idx_rows, w_rows, _, _) in enumerate(routed):
        wsum = w_rows[0]
        for wr in w_rows[1:]:
            wsum = wsum + wr
        wts_t = jnp.concatenate([wr / wsum * ROUTED_SCALE for wr in w_rows], axis=0)
        wpad = jnp.concatenate([wts_t, jnp.zeros((LANES - TOP_K, LANES), F32)], axis=0)
        wts_ref[pl.ds(c * LANES, LANES), :] = wpad.T
        idx_ref[:, pl.ds(c * LANES, LANES)] = jnp.concatenate(idx_rows, axis=0).astype(I32)
    sel_any = jnp.concatenate([r[2] for r in routed], axis=1)
    pref = jnp.dot(sel_any.astype(BF16), tri_ref[...], preferred_element_type=F32)
    before = base_ref[...] + pref[:, :T_MIX]
    for c, (idx_rows, _, _, rowf) in enumerate(routed):
        bc = before[:, c * LANES:(c + 1) * LANES]
        ranks = [jnp.sum(jnp.where(rowf == ik, bc, 0.0), axis=0, keepdims=True) for ik in idx_rows]
        rank_ref[:, pl.ds(c * LANES, LANES)] = jnp.concatenate(ranks, axis=0).astype(I32)
    new_base = base_ref[...] + pref[:, T_MIX:]
    base_ref[...] = new_base
    cnt_ref[...] = new_base


def _mixer(xp, xs, mod3, u, q, k, v, wpool_bf, bpool, pscale, sink, wout_bf, g2, wr3, rbias,
           wsgu_bf, wsd_bf, seqs):
    n_p, n_s = xp.shape[0], xs.shape[0]
    n = n_p + n_s
    t = T_MIX
    nt = n // t
    npt = n_p // t
    ab = ATT_BLOCK
    meta = jnp.asarray(_tile_meta(seqs, t))
    tri = np.concatenate([np.triu(np.ones((t, t), np.float32), 1), np.ones((t, t), np.float32)], axis=1)
    tri = jnp.asarray(tri, BF16)
    cur = lambda w: pl.BlockSpec((t, w), lambda i, m: (i, 0))
    prev = lambda w: pl.BlockSpec((ab, w), lambda i, m: (jnp.maximum(i * (t // ab) - 1, 0), 0))
    nxt = lambda w: pl.BlockSpec((ab, w), lambda i, m: (jnp.minimum((i + 1) * (t // ab), n // ab - 1), 0))
    full = lambda a: pl.BlockSpec(a.shape, lambda i, m: (0,) * a.ndim)
    hb = t // HALO
    in_specs = [
        pl.BlockSpec((t, D_MODEL), lambda i, m: (jnp.minimum(i, npt - 1), 0)),
        pl.BlockSpec((t, D_MODEL), lambda i, m: (jnp.maximum(i - npt, 0), 0)),
        pl.BlockSpec((1, 1, 6 * D_MODEL), lambda i, m: (m[0, i], 0, 0)),
        pl.BlockSpec((HALO, POOL_WIDTH), lambda i, m: (jnp.maximum(i * hb - 1, 0), 0)),
        cur(POOL_WIDTH),
        pl.BlockSpec((HALO, POOL_WIDTH), lambda i, m: (jnp.minimum((i + 1) * hb, n // HALO - 1), 0)),
        cur(ATTN_WIDTH),
        prev(KV_WIDTH), cur(KV_WIDTH), nxt(KV_WIDTH),
        prev(KV_WIDTH), cur(KV_WIDTH), nxt(KV_WIDTH),
        full(wpool_bf), full(bpool), full(pscale), full(sink), full(wout_bf), full(g2),
        full(wr3), full(rbias), full(wsgu_bf), full(wsd_bf), full(tri),
    ]
    out_specs = [
        cur(D_MODEL),
        pl.BlockSpec((N_PLANES, t, LANES), lambda i, m: (0, i, 0)),
        pl.BlockSpec((TOP_K, t), lambda i, m: (0, i)),
        pl.BlockSpec((TOP_K, t), lambda i, m: (0, i)),
        cur(LANES),
        pl.BlockSpec((N_EXPERTS, t), lambda i, m: (0, 0)),
    ]
    out_shape = [
        jax.ShapeDtypeStruct((n, D_MODEL), F32),
        jax.ShapeDtypeStruct((N_PLANES, n, LANES), I32),
        jax.ShapeDtypeStruct((TOP_K, n), I32),
        jax.ShapeDtypeStruct((TOP_K, n), I32),
        jax.ShapeDtypeStruct((n, LANES), F32),
        jax.ShapeDtypeStruct((N_EXPERTS, t), F32),
    ]
    return pl.pallas_call(
        functools.partial(_mixer_kernel, n_prompt_tiles=npt),
        grid_spec=pltpu.PrefetchScalarGridSpec(
            num_scalar_prefetch=1, grid=(nt,), in_specs=in_specs, out_specs=out_specs,
            scratch_shapes=[pltpu.VMEM((t + 2 * HALO, POOL_WIDTH), F32),
                            pltpu.VMEM((t + 2 * ab, KV_WIDTH), BF16),
                            pltpu.VMEM((t + 2 * ab, KV_WIDTH), BF16),
                            pltpu.VMEM((N_EXPERTS, t), F32)]),
        out_shape=out_shape,
        compiler_params=pltpu.CompilerParams(dimension_semantics=("arbitrary",),
                                             vmem_limit_bytes=VMEM_LIMIT),
        name="mixer",
    )(meta, xp, xs, mod3, u, u, u, q, k, k, k, v, v, v, wpool_bf, bpool, pscale, sink, wout_bf, g2,
      wr3, rbias, wsgu_bf, wsd_bf, tri)


def _dest_kernel(idx_ref, rank_ref, pstart_ref, o_ref, *, n_rows):
    rowi = lax.broadcasted_iota(I32, (N_EXPERTS, T_DEST), 0)
    pstart = pstart_ref[...]
    rows = []
    for kk in range(TOP_K):
        sel = rowi == idx_ref[kk:kk + 1, :]
        start = jnp.sum(jnp.where(sel, pstart, 0.0), axis=0, keepdims=True)
        rows.append(start.astype(I32) + rank_ref[kk:kk + 1, :])
    dest = jnp.concatenate(rows, axis=0)
    for j in range(N_PLANES):
        o_ref[:, j, :] = dest + j * n_rows


def _dest(idx_t, rank_t, pstart_col, n_rows):
    n = idx_t.shape[1]
    return pl.pallas_call(
        functools.partial(_dest_kernel, n_rows=n_rows),
        grid=(n // T_DEST,),
        in_specs=[pl.BlockSpec((TOP_K, T_DEST), lambda i: (0, i)),
                  pl.BlockSpec((TOP_K, T_DEST), lambda i: (0, i)),
                  pl.BlockSpec((N_EXPERTS, 1), lambda i: (0, 0))],
        out_specs=pl.BlockSpec((TOP_K, N_PLANES, T_DEST), lambda i: (0, 0, i)),
        out_shape=jax.ShapeDtypeStruct((TOP_K, N_PLANES, n), I32),
        name="dest",
    )(idx_t, rank_t, pstart_col)


def _sc_mesh():
    return plsc.VectorSubcoreMesh(core_axis_name="c", subcore_axis_name="s")


def _sc_worker():
    return lax.axis_index("s") * 2 + lax.axis_index("c")


def _sc_scatter_rows(table, didx, n_out_rows):
    m = table.shape[0]
    per_worker = m // SC_ROWS // SC_WORKERS

    @functools.partial(
        pl.kernel, mesh=_sc_mesh(),
        out_type=jax.ShapeDtypeStruct((n_out_rows, LANES), I32),
        scratch_types=[pltpu.VMEM((SC_ROWS, LANES), I32), pltpu.VMEM((TOP_K, SC_ROWS), I32)])
    def run(table_hbm, didx_hbm, out_hbm, rows_v, idx_v):
        wid = _sc_worker()

        @pl.loop(0, per_worker)
        def _(step):
            off = pl.multiple_of((wid * per_worker + step) * SC_ROWS, SC_ROWS)
            pltpu.sync_copy(table_hbm.at[pl.ds(off, SC_ROWS)], rows_v)
            pltpu.sync_copy(didx_hbm.at[:, pl.ds(off, SC_ROWS)], idx_v)
            for kk in range(TOP_K):
                pltpu.sync_copy(rows_v, out_hbm.at[idx_v.at[kk]])

    return run(table, didx)


def _sc_gather_rows(table, didx):
    m = didx.shape[1]
    per_worker = m // SC_ROWS // SC_WORKERS
    assert per_worker % 2 == 0 and (2 * TOP_K) % SC_NBUF == 0
    items = 2 * TOP_K

    @functools.partial(
        pl.kernel, mesh=_sc_mesh(),
        out_type=jax.ShapeDtypeStruct((TOP_K, m, LANES), I32),
        scratch_types=[pltpu.VMEM((SC_NBUF, SC_ROWS, LANES), I32), pltpu.VMEM((2, TOP_K, SC_ROWS), I32),
                       pltpu.SemaphoreType.DMA((SC_NBUF,)), pltpu.SemaphoreType.DMA((SC_NBUF,)),
                       pltpu.SemaphoreType.DMA((2,))])
    def run(table_hbm, didx_hbm, out_hbm, rows, idx, gsem, wsem, isem):
        wid = _sc_worker()

        def off_of(step):
            return pl.multiple_of((wid * per_worker + step) * SC_ROWS, SC_ROWS)

        def idx_copy(step, p):
            return pltpu.make_async_copy(didx_hbm.at[:, pl.ds(off_of(step), SC_ROWS)], idx.at[p], isem.at[p])

        def gather(j):
            r = j % SC_NBUF
            return pltpu.make_async_copy(table_hbm.at[idx.at[j // TOP_K].at[j % TOP_K]], rows.at[r], gsem.at[r])

        def write(s0, j):
            r = j % SC_NBUF
            return pltpu.make_async_copy(rows.at[r], out_hbm.at[j % TOP_K, pl.ds(off_of(s0 + j // TOP_K), SC_ROWS)],
                                         wsem.at[r])

        def retire_write(s0, j):
            if j >= 0:
                write(s0, j).wait()
            else:
                @pl.when(s0 > 0)
                def _():
                    write(s0 - 2, j + items).wait()

        def finish_read(s0, j):
            if j >= 0:
                gather(j).wait()
                write(s0, j).start()
            else:
                @pl.when(s0 > 0)
                def _():
                    gather(j + items).wait()
                    write(s0 - 2, j + items).start()

        idx_copy(0, 0).start()

        @pl.loop(0, per_worker, step=2)
        def _(s0):
            for j in range(items):
                p, kk = j // TOP_K, j % TOP_K
                if kk == 0:
                    idx_copy(s0 + p, p).wait()
                retire_write(s0, j - SC_NBUF)
                gather(j).start()
                finish_read(s0, j - SC_LAG)
                if kk == SC_LAG:
                    @pl.when(s0 + p + 1 < per_worker)
                    def _():
                        idx_copy(s0 + p + 1, 1 - p).start()

        last = per_worker - 2
        for j in range(items - SC_LAG, items):
            gather(j).wait()
            write(last, j).start()
        for j in range(items - SC_NBUF, items):
            write(last, j).wait()

    return run(table, didx)


def _experts_kernel(blk_e_ref, blk_valid_ref, x_ref, wg_ref, wu_ref, wd_ref, y_ref, wgu_s, wd_s):
    b = pl.program_id(0)
    e = blk_e_ref[b]
    e_prev = blk_e_ref[jnp.maximum(b - 1, 0)]

    @pl.when((b == 0) | (e != e_prev))
    def _():
        for j in range(N_PLANES):
            hi = pl.ds(j * LANES, LANES)
            lo = pl.ds(HALF + j * LANES, LANES)
            wgu_s[j, pl.ds(0, LANES), pl.ds(0, D_EXPERT)] = wg_ref[0, hi, :].astype(BF16)
            wgu_s[j, pl.ds(LANES, LANES), pl.ds(0, D_EXPERT)] = wg_ref[0, lo, :].astype(BF16)
            wgu_s[j, pl.ds(0, LANES), pl.ds(D_EXPERT, D_EXPERT)] = wu_ref[0, hi, :].astype(BF16)
            wgu_s[j, pl.ds(LANES, LANES), pl.ds(D_EXPERT, D_EXPERT)] = wu_ref[0, lo, :].astype(BF16)
        wd_s[...] = wd_ref[0].astype(BF16)

    live = lax.broadcasted_iota(I32, (BM, 2 * LANES), 0) < blk_valid_ref[b]
    acc = jnp.zeros((BM, 2 * D_EXPERT), F32)
    for j in range(N_PLANES):
        hi, lo = _unpack_plane(x_ref[j])
        xj = jnp.where(live, jnp.concatenate([hi, lo], axis=1), 0.0).astype(BF16)
        acc = acc + jnp.dot(xj, wgu_s[j], preferred_element_type=F32)
    gate, up = acc[:, :D_EXPERT], acc[:, D_EXPERT:]
    act = (gate * jax.nn.sigmoid(gate)) * up
    y = jnp.dot(act.astype(BF16), wd_s[...], preferred_element_type=F32)
    planes = _pack_planes(y)
    for j in range(N_PLANES):
        y_ref[j] = planes[j]


def _experts(xs_planes, w_gate, w_up, w_down, blk_e, blk_valid, nused):
    n_rows = xs_planes.shape[1]
    row_map = lambda b, be, bv: (0, b, 0)
    w_map = lambda b, be, bv: (be[b], 0, 0)
    return pl.pallas_call(
        _experts_kernel,
        grid_spec=pltpu.PrefetchScalarGridSpec(
            num_scalar_prefetch=2, grid=(nused,),
            in_specs=[pl.BlockSpec((N_PLANES, BM, LANES), row_map),
                      pl.BlockSpec((1, D_MODEL, D_EXPERT), w_map),
                      pl.BlockSpec((1, D_MODEL, D_EXPERT), w_map),
                      pl.BlockSpec((1, D_EXPERT, D_MODEL), w_map)],
            out_specs=pl.BlockSpec((N_PLANES, BM, LANES), row_map),
            scratch_shapes=[pltpu.VMEM((N_PLANES, 2 * LANES, 2 * D_EXPERT), BF16),
                            pltpu.VMEM((D_EXPERT, D_MODEL), BF16)]),
        out_shape=jax.ShapeDtypeStruct((N_PLANES, n_rows, LANES), I32),
        compiler_params=pltpu.CompilerParams(dimension_semantics=("arbitrary",),
                                             vmem_limit_bytes=VMEM_LIMIT),
        name="experts",
    )(blk_e, blk_valid, xs_planes, w_gate, w_up, w_down)


def _final_kernel(xa_ref, yk_ref, wts_ref, mod_ref, g_ref, o_ref):
    wts = wts_ref[...]
    his = [jnp.zeros((T_FIN, LANES), F32) for _ in range(N_PLANES)]
    los = [jnp.zeros((T_FIN, LANES), F32) for _ in range(N_PLANES)]
    for kk in range(TOP_K):
        wk = wts[:, kk:kk + 1]
        for j in range(N_PLANES):
            hi, lo = _unpack_plane(yk_ref[kk, j])
            his[j] = his[j] + wk * hi
            los[j] = los[j] + wk * lo
    routed = jnp.concatenate(his + los, axis=1)
    gt2 = mod_ref[0][:, 5 * D_MODEL:6 * D_MODEL]
    x = xa_ref[...] + gt2 * routed
    ms = jnp.mean(x * x, axis=-1, keepdims=True)
    o_ref[...] = x * lax.rsqrt(ms + EPS) * g_ref[...]


def _final(xa, yk, wts, mod3, g_final, tile_off, n_tok, seq_len, sid0):
    tiles_per_seq = seq_len // T_FIN
    return pl.pallas_call(
        _final_kernel,
        grid=(n_tok // T_FIN,),
        in_specs=[pl.BlockSpec((T_FIN, D_MODEL), lambda i: (i + tile_off, 0)),
                  pl.BlockSpec((TOP_K, N_PLANES, T_FIN, LANES), lambda i: (0, 0, i + tile_off, 0)),
                  pl.BlockSpec((T_FIN, LANES), lambda i: (i + tile_off, 0)),
                  pl.BlockSpec((1, 1, 6 * D_MODEL), lambda i: (sid0 + i // tiles_per_seq, 0, 0)),
                  pl.BlockSpec((1, D_MODEL), lambda i: (0, 0))],
        out_specs=pl.BlockSpec((T_FIN, D_MODEL), lambda i: (i, 0)),
        out_shape=jax.ShapeDtypeStruct((n_tok, D_MODEL), F32),
        compiler_params=pltpu.CompilerParams(dimension_semantics=("arbitrary",),
                                             vmem_limit_bytes=VMEM_LIMIT),
        name="final",
    )(xa, yk, wts, mod3, g_final)


def kernel(x_prompt, x_sample, c_prompt, c_sample, w_ada, b_ada, g_norm1, w_in, w_pool, b_pool, pool_scale, attn_sink, w_out, g_norm2, w_router, router_bias, w_gate, w_up, w_down, ws_gate, ws_up, ws_down, g_final):
    assert w_ada.shape[0] == 1, "one layer"
    bp, sp, d = x_prompt.shape
    bs, ss, _ = x_sample.shape
    assert d == D_MODEL and bp + bs <= ADA_ROWS
    seqs = ((bp, sp), (bs, ss))
    n_p, n_s = bp * sp, bs * ss
    n = n_p + n_s
    xp = x_prompt.reshape(n_p, d)
    xs = x_sample.reshape(n_s, d)

    c_all = jnp.concatenate([c_prompt, c_sample, jnp.zeros((ADA_ROWS - bp - bs, d), F32)], axis=0)
    mod3 = _ada(c_all, w_ada[0], b_ada[0]).reshape(ADA_ROWS, 1, 6 * d)

    pair = np.concatenate([np.arange(HEAD_DIM) + HEAD_DIM * h
                           for c in range(Q_PER_KV) for h in (c, Q_PER_KV + c)])
    in_cols = np.concatenate([np.arange(POOL_WIDTH), POOL_WIDTH + pair,
                              np.arange(POOL_WIDTH + ATTN_WIDTH, D_IN_PROJ)])
    out_rows = np.concatenate([np.arange(POOL_WIDTH), POOL_WIDTH + pair])
    u, q, k, v = _inproj(xp, xs, mod3, g_norm1[0].reshape(1, d), w_in[0][:, in_cols].astype(BF16), seqs)

    wsgu = jnp.concatenate([ws_gate[0], ws_up[0]], axis=1).astype(BF16)
    wr_hi = w_router[0].astype(BF16)
    wr_lo = (w_router[0] - wr_hi.astype(F32)).astype(BF16)
    wr3 = jnp.concatenate([wr_hi, wr_lo, wr_hi], axis=0).T
    xa, h2p, idx_t, rank_t, wts, cnt = _mixer(
        xp, xs, mod3, u, q, k, v,
        w_pool[0].astype(BF16), b_pool[0].reshape(1, POOL_WIDTH), pool_scale[0].reshape(1, POOL_WIDTH),
        attn_sink[0].reshape(1, N_HEADS), w_out[0][out_rows].astype(BF16), g_norm2[0].reshape(1, d),
        wr3, router_bias[0].reshape(N_EXPERTS, 1), wsgu, ws_down[0].astype(BF16), seqs)

    n_rows = n * TOP_K + N_EXPERTS * BM
    nblk = n_rows // BM
    counts = cnt[:, 0].astype(I32)
    padded = (counts + BM - 1) // BM * BM
    pend = jnp.cumsum(padded)
    pstart = pend - padded
    blk_start = jnp.arange(nblk, dtype=I32) * BM
    blk_e = jnp.clip(jnp.searchsorted(pend, blk_start, side="right"), 0, N_EXPERTS - 1).astype(I32)
    blk_valid = jnp.clip(counts[blk_e] - (blk_start - pstart[blk_e]), 0, BM).astype(I32)
    nused = (pend[-1] // BM).astype(I32)

    didx = _dest(idx_t, rank_t, pstart.astype(F32).reshape(N_EXPERTS, 1), n_rows)
    didx = didx.reshape(TOP_K, N_PLANES * n)

    xs_rows = _sc_scatter_rows(h2p.reshape(N_PLANES * n, LANES), didx, N_PLANES * n_rows)
    ys_rows = _experts(xs_rows.reshape(N_PLANES, n_rows, LANES), w_gate[0], w_up[0], w_down[0],
                       blk_e, blk_valid, nused)
    yk = _sc_gather_rows(ys_rows.reshape(N_PLANES * n_rows, LANES), didx)
    yk = yk.reshape(TOP_K, N_PLANES, n, LANES)

    gf = g_final.reshape(1, d)
    y_p = _final(xa, yk, wts, mod3, gf, 0, n_p, sp, 0)
    y_s = _final(xa, yk, wts, mod3, gf, n_p // T_FIN, n_s, ss, bp)
    return y_p.reshape(bp, sp, d), y_s.reshape(bs, ss, d)
```

```python
import functools

import numpy as np
import jax
import jax.numpy as jnp
from jax import lax
from jax.experimental import pallas as pl
from jax.experimental.pallas import tpu as pltpu
from jax.experimental.pallas import tpu_sc as plsc

F32 = jnp.float32
BF16 = jnp.bfloat16
I32 = jnp.int32

D_MODEL = 1024
POOL_WINDOWS = (2, 4, 8, 16)
POOL_WIDTH = 512
POOL_GROUP = 128
N_HEADS = 8
N_KV_HEADS = 2
HEAD_DIM = 64
Q_PER_KV = N_HEADS // N_KV_HEADS
ATTN_WIDTH = N_HEADS * HEAD_DIM
KV_WIDTH = N_KV_HEADS * HEAD_DIM
D_IN_PROJ = POOL_WIDTH + ATTN_WIDTH + 2 * KV_WIDTH
WINDOW = 128
N_EXPERTS = 256
TOP_K = 8
N_EXPERT_GROUPS = 8
GROUP_SIZE = N_EXPERTS // N_EXPERT_GROUPS
TOPK_GROUPS = 4
D_EXPERT = 256
D_SHARED = 256
ROUTED_SCALE = 2.5
EPS = 1e-6
NEG_INF = -1e30

LANES = 128
HALO = 8
N_PLANES = 4
HALF = D_MODEL // 2

T_IN = 512
T_MIX = 256
ATT_BLOCK = 128
T_DEST = 512
EXPERT_BLOCK_TARGET = 768
EXPERT_BLOCK_MAX = 1024
EXPERT_LOAD_SLACK = 6.0
T_FIN = 256
SC_ROWS = 128
SC_WORKERS = 32
SC_NBUF = 4
SC_LAG = 2
ADA_ROWS = 16
VMEM_LIMIT = 48 * 1024 * 1024

LOG2E = 1.4426950408889634
ALIBI_SLOPES = tuple(float(2.0 ** (-8.0 * (h + 1) / N_HEADS)) for h in range(N_HEADS))


def _pack_planes(y):
    planes = []
    for j in range(N_PLANES):
        hi = y[:, j * LANES:(j + 1) * LANES].astype(BF16).astype(F32)
        lo = y[:, HALF + j * LANES:HALF + (j + 1) * LANES].astype(BF16).astype(F32)
        hb = lax.bitcast_convert_type(hi, jnp.uint32) & jnp.uint32(0xFFFF0000)
        lb = lax.bitcast_convert_type(lo, jnp.uint32) >> jnp.uint32(16)
        planes.append(lax.bitcast_convert_type(hb | lb, I32))
    return planes


def _unpack_plane(w):
    u = lax.bitcast_convert_type(w, jnp.uint32)
    hi = lax.bitcast_convert_type(u & jnp.uint32(0xFFFF0000), F32)
    lo = lax.bitcast_convert_type(u << jnp.uint32(16), F32)
    return hi, lo


def _ada_kernel(c_ref, w_ref, b_ref, o_ref):
    c = c_ref[...]
    s = c * jax.nn.sigmoid(c)
    o_ref[...] = jnp.dot(s, w_ref[...], precision=lax.Precision.HIGHEST,
                         preferred_element_type=F32) + b_ref[...]


def _ada(c_all, w_ada, b_ada):
    n_out = w_ada.shape[1]
    tn = 1024
    return pl.pallas_call(
        _ada_kernel,
        grid=(n_out // tn,),
        in_specs=[pl.BlockSpec((ADA_ROWS, D_MODEL), lambda j: (0, 0)),
                  pl.BlockSpec((D_MODEL, tn), lambda j: (0, j)),
                  pl.BlockSpec((1, tn), lambda j: (0, j))],
        out_specs=pl.BlockSpec((ADA_ROWS, tn), lambda j: (0, j)),
        out_shape=jax.ShapeDtypeStruct((ADA_ROWS, n_out), F32),
        name="ada",
    )(c_all, w_ada, b_ada.reshape(1, n_out))


def _rms_mod(x, g, scale, shift):
    ms = jnp.mean(x * x, axis=-1, keepdims=True)
    return (x * lax.rsqrt(ms + EPS) * g) * (1.0 + scale) + shift


def _inproj_kernel(meta_ref, xp_ref, xs_ref, mod_ref, g_ref, w_ref, u_ref, q_ref, k_ref, v_ref,
                   *, n_prompt_tiles):
    x = jnp.where(pl.program_id(0) < n_prompt_tiles, xp_ref[...], xs_ref[...])
    mod = mod_ref[0]
    h = _rms_mod(x, g_ref[...], mod[:, D_MODEL:2 * D_MODEL], mod[:, 0:D_MODEL])
    z = jnp.dot(h.astype(BF16), w_ref[...], preferred_element_type=F32)
    u_ref[...] = z[:, :POOL_WIDTH]
    q_ref[...] = (z[:, POOL_WIDTH:POOL_WIDTH + ATTN_WIDTH] * (LOG2E * HEAD_DIM ** -0.5)).astype(BF16)
    k_ref[...] = z[:, POOL_WIDTH + ATTN_WIDTH:POOL_WIDTH + ATTN_WIDTH + KV_WIDTH].astype(BF16)
    v_ref[...] = z[:, POOL_WIDTH + ATTN_WIDTH + KV_WIDTH:].astype(BF16)


def _tile_meta(seqs, tile):
    rows = []
    sid = 0
    for count, length in seqs:
        for _ in range(count):
            n = length // tile
            for t in range(n):
                rows.append((sid, int(t == 0), int(t == n - 1), t * tile, length))
            sid += 1
    return np.asarray(rows, np.int32).T.copy()


def _inproj(xp, xs, mod3, g1, w_in_bf, seqs):
    n_p, n_s = xp.shape[0], xs.shape[0]
    n = n_p + n_s
    npt = n_p // T_IN
    meta = jnp.asarray(_tile_meta(seqs, T_IN))
    tok = lambda w: pl.BlockSpec((T_IN, w), lambda i, m: (i, 0))
    return pl.pallas_call(
        functools.partial(_inproj_kernel, n_prompt_tiles=npt),
        grid_spec=pltpu.PrefetchScalarGridSpec(
            num_scalar_prefetch=1,
            grid=(n // T_IN,),
            in_specs=[
                pl.BlockSpec((T_IN, D_MODEL), lambda i, m: (jnp.minimum(i, npt - 1), 0)),
                pl.BlockSpec((T_IN, D_MODEL), lambda i, m: (jnp.maximum(i - npt, 0), 0)),
                pl.BlockSpec((1, 1, 6 * D_MODEL), lambda i, m: (m[0, i], 0, 0)),
                pl.BlockSpec((1, D_MODEL), lambda i, m: (0, 0)),
                pl.BlockSpec((D_MODEL, D_IN_PROJ), lambda i, m: (0, 0)),
            ],
            out_specs=[tok(POOL_WIDTH), tok(ATTN_WIDTH), tok(KV_WIDTH), tok(KV_WIDTH)],
        ),
        out_shape=[jax.ShapeDtypeStruct((n, POOL_WIDTH), F32),
                   jax.ShapeDtypeStruct((n, ATTN_WIDTH), BF16),
                   jax.ShapeDtypeStruct((n, KV_WIDTH), BF16),
                   jax.ShapeDtypeStruct((n, KV_WIDTH), BF16)],
        compiler_params=pltpu.CompilerParams(dimension_semantics=("arbitrary",),
                                             vmem_limit_bytes=VMEM_LIMIT),
        name="inproj",
    )(meta, xp, xs, mod3, g1, w_in_bf)


def _pool_mixer(uext_ref, uc, pos, seq_len, wpool_ref, bpool_ref, pscale_ref):
    outs = []
    for gi, w in enumerate(POOL_WINDOWS):
        c0 = gi * POOL_GROUP
        half = w // 2
        acc = uext_ref[pl.ds(HALO - half, T_MIX), pl.ds(c0, POOL_GROUP)]
        for o in range(-half + 1, half):
            acc = acc + uext_ref[pl.ds(HALO + o, T_MIX), pl.ds(c0, POOL_GROUP)]
        lo = jnp.maximum(pos - half, 0)
        hi = jnp.minimum(pos + half, seq_len)
        cnt = (hi - lo).astype(F32)
        d = acc / cnt - uc[:, c0:c0 + POOL_GROUP]
        y = jnp.dot(d.astype(BF16), wpool_ref[gi], preferred_element_type=F32)
        y = (y + bpool_ref[:, c0:c0 + POOL_GROUP]) * pscale_ref[:, c0:c0 + POOL_GROUP]
        outs.append(y)
    return jnp.concatenate(outs, axis=1)


def _attention_bias():
    t = ATT_BLOCK
    r = np.arange(t)[:, None]
    c = np.arange(3 * t)[None, :]
    dist = np.abs(r - (c - t))
    band = dist <= WINDOW
    out = np.empty((3, N_HEADS, t, 3 * t), np.float32)
    for var, valid in enumerate((band, band & (c >= t), band & (c < 2 * t))):
        for h in range(N_HEADS):
            out[var, h] = np.where(valid, -ALIBI_SLOPES[h] * LOG2E * dist, NEG_INF)
    return out.reshape(3, N_HEADS * t, 3 * t)


def _banded_attention(q, kw, vw, variant, bias_ref, sink_ref):
    t = ATT_BLOCK
    low = lax.broadcasted_iota(I32, (t, LANES), 1) < HEAD_DIM
    zero = jnp.zeros((t, LANES), BF16)
    qs = []
    for h in range(N_HEADS):
        blk = q[:, (h % Q_PER_KV) * LANES:(h % Q_PER_KV + 1) * LANES]
        qs.append(jnp.where(low, blk, zero) if h < Q_PER_KV else jnp.where(low, zero, blk))
    s = lax.dot_general(jnp.concatenate(qs, axis=0), kw, (((1,), (1,)), ((), ())),
                        preferred_element_type=F32)
    ps, dens = [], []
    for h in range(N_HEADS):
        sh = s[h * t:(h + 1) * t] + bias_ref[variant, pl.ds(h * t, t), :]
        sink = sink_ref[:, h:h + 1] * LOG2E
        m = jnp.maximum(jnp.max(sh, axis=-1, keepdims=True), sink)
        p = jnp.exp2(sh - m)
        dens.append(jnp.sum(p, axis=-1, keepdims=True) + jnp.exp2(sink - m))
        ps.append(p.astype(BF16))
    o = jnp.dot(jnp.concatenate(ps, axis=0), vw, preferred_element_type=F32)
    oh = [o[h * t:(h + 1) * t] / dens[h] for h in range(N_HEADS)]
    return jnp.concatenate([jnp.where(low, oh[cb], oh[Q_PER_KV + cb]) for cb in range(Q_PER_KV)], axis=1)


def _route(biased, scores):
    t = biased.shape[1]
    rowf = lax.broadcasted_iota(I32, (N_EXPERTS, t), 0).astype(F32)
    ninf = float("-inf")
    gs = []
    for g in range(N_EXPERT_GROUPS):
        blk = biased[g * GROUP_SIZE:(g + 1) * GROUP_SIZE, :]
        rf = rowf[g * GROUP_SIZE:(g + 1) * GROUP_SIZE, :]
        m1 = jnp.max(blk, axis=0, keepdims=True)
        i1 = jnp.min(jnp.where(blk == m1, rf, float(N_EXPERTS)), axis=0, keepdims=True)
        m2 = jnp.max(jnp.where(rf == i1, ninf, blk), axis=0, keepdims=True)
        gs.append(m1 + m2)
    keep = []
    for g in range(N_EXPERT_GROUPS):
        beat = jnp.zeros((1, t), F32)
        for g2 in range(N_EXPERT_GROUPS):
            if g2 == g:
                continue
            better = (gs[g2] >= gs[g]) if g2 < g else (gs[g2] > gs[g])
            beat = beat + better.astype(F32)
        keep.append(jnp.broadcast_to(beat < float(TOPK_GROUPS), (GROUP_SIZE, t)))
    emask = jnp.concatenate(keep, axis=0)
    masked = jnp.where(emask, biased, NEG_INF)
    idx_rows, w_rows = [], []
    sel_any = jnp.zeros((N_EXPERTS, t), F32)
    for _ in range(TOP_K):
        m = jnp.max(masked, axis=0, keepdims=True)
        ik = jnp.min(jnp.where(masked == m, rowf, float(N_EXPERTS)), axis=0, keepdims=True)
        sel = rowf == ik
        w_rows.append(jnp.sum(jnp.where(sel, scores, 0.0), axis=0, keepdims=True))
        idx_rows.append(ik)
        masked = jnp.where(sel, ninf, masked)
        sel_any = sel_any + sel.astype(F32)
    return idx_rows, w_rows, sel_any, rowf


def _mixer_kernel(meta_ref, xp_ref, xs_ref, mod_ref, up_ref, uc_ref, un_ref, q_ref,
                  kp_ref, kc_ref, kn_ref, vp_ref, vc_ref, vn_ref,
                  wpool_ref, bpool_ref, pscale_ref, sink_ref, wout_ref, g2_ref,
                  wr3_ref, rbias_ref, wsgu_ref, wsd_ref, tri_ref, abias_ref,
                  xa_ref, h2p_ref, idx_ref, rank_ref, wts_ref, cnt_ref,
                  uext_ref, kext_ref, vext_ref, base_ref, *, n_prompt_tiles):
    i = pl.program_id(0)
    first = meta_ref[1, i] == 1
    last = meta_ref[2, i] == 1
    pos0 = meta_ref[3, i]
    seq_len = meta_ref[4, i]
    mod = mod_ref[0]
    gt1 = mod[:, 2 * D_MODEL:3 * D_MODEL]
    sh2 = mod[:, 3 * D_MODEL:4 * D_MODEL]
    sc2 = mod[:, 4 * D_MODEL:5 * D_MODEL]
    gt2 = mod[:, 5 * D_MODEL:6 * D_MODEL]

    @pl.when(i == 0)
    def _():
        base_ref[...] = jnp.zeros_like(base_ref)

    uc = uc_ref[...]
    uext_ref[pl.ds(0, HALO), :] = jnp.where(first, 0.0, up_ref[...])
    uext_ref[pl.ds(HALO, T_MIX), :] = uc
    uext_ref[pl.ds(HALO + T_MIX, HALO), :] = jnp.where(last, 0.0, un_ref[...])
    pos = pos0 + lax.broadcasted_iota(I32, (T_MIX, 1), 0)
    a_pool = _pool_mixer(uext_ref, uc, pos, seq_len, wpool_ref, bpool_ref, pscale_ref)

    t = ATT_BLOCK
    kext_ref[pl.ds(0, t), :] = kp_ref[...]
    kext_ref[pl.ds(t, T_MIX), :] = kc_ref[...]
    kext_ref[pl.ds(t + T_MIX, t), :] = kn_ref[...]
    vext_ref[pl.ds(0, t), :] = vp_ref[...]
    vext_ref[pl.ds(t, T_MIX), :] = vc_ref[...]
    vext_ref[pl.ds(t + T_MIX, t), :] = vn_ref[...]
    n_sub = T_MIX // t
    assert n_sub >= 2, "a query block is never both first and last in its sequence"
    attn = []
    for sub in range(n_sub):
        variant = 0
        if sub == 0:
            variant = jnp.where(first, 1, variant)
        if sub == n_sub - 1:
            variant = jnp.where(last, 2, variant)
        attn.append(_banded_attention(
            q_ref[pl.ds(sub * t, t), :], kext_ref[pl.ds(sub * t, 3 * t), :], vext_ref[pl.ds(sub * t, 3 * t), :],
            variant, abias_ref, sink_ref))
    a_attn = jnp.concatenate(attn, axis=0)

    a = jnp.concatenate([a_pool, a_attn], axis=1).astype(BF16)
    mix = jnp.dot(a, wout_ref[...], preferred_element_type=F32)

    x = jnp.where(i < n_prompt_tiles, xp_ref[...], xs_ref[...])
    x1 = x + gt1 * mix
    h2 = _rms_mod(x1, g2_ref[...], sc2, sh2)
    h2b = h2.astype(BF16)
    planes = _pack_planes(h2)
    for j in range(N_PLANES):
        h2p_ref[j] = planes[j]
    gu = jnp.dot(h2b, wsgu_ref[...], preferred_element_type=F32)
    gate, up = gu[:, :D_SHARED], gu[:, D_SHARED:]
    act = (gate * jax.nn.sigmoid(gate)) * up
    shared = jnp.dot(act.astype(BF16), wsd_ref[...], preferred_element_type=F32)
    xa_ref[...] = x1 + gt2 * shared
    h2lo = (h2 - h2b.astype(F32)).astype(BF16)
    logits = lax.dot_general(wr3_ref[...], jnp.concatenate([h2b, h2b, h2lo], axis=1),
                             (((1,), (1,)), ((), ())), preferred_element_type=F32)
    scores = jax.nn.sigmoid(logits)
    biased = scores + rbias_ref[...]
    routed = [_route(biased[:, c * LANES:(c + 1) * LANES], scores[:, c * LANES:(c + 1) * LANES])
              for c in range(T_MIX // LANES)]
    for c, (idx_rows, w_rows, _, _) in enumerate(routed):
        wsum = w_rows[0]
        for wr in w_rows[1:]:
            wsum = wsum + wr
        wts_t = jnp.concatenate([wr / wsum * ROUTED_SCALE for wr in w_rows], axis=0)
        wpad = jnp.concatenate([wts_t, jnp.zeros((LANES - TOP_K, LANES), F32)], axis=0)
        wts_ref[pl.ds(c * LANES, LANES), :] = wpad.T
        idx_ref[:, pl.ds(c * LANES, LANES)] = jnp.concatenate(idx_rows, axis=0).astype(I32)
    sel_any = jnp.concatenate([r[2] for r in routed], axis=1)
    pref = jnp.dot(sel_any.astype(BF16), tri_ref[...], preferred_element_type=F32)
    before = base_ref[...] + pref[:, :T_MIX]
    for c, (idx_rows, _, _, rowf) in enumerate(routed):
        bc = before[:, c * LANES:(c + 1) * LANES]
        ranks = [jnp.sum(jnp.where(rowf == ik, bc, 0.0), axis=0, keepdims=True) for ik in idx_rows]
        rank_ref[:, pl.ds(c * LANES, LANES)] = jnp.concatenate(ranks, axis=0).astype(I32)
    new_base = base_ref[...] + pref[:, T_MIX:]
    base_ref[...] = new_base
    cnt_ref[...] = new_base


def _mixer(xp, xs, mod3, u, q, k, v, wpool_bf, bpool, pscale, sink, wout_bf, g2, wr3, rbias,
           wsgu_bf, wsd_bf, seqs):
    n_p, n_s = xp.shape[0], xs.shape[0]
    n = n_p + n_s
    t = T_MIX
    nt = n // t
    npt = n_p // t
    ab = ATT_BLOCK
    meta = jnp.asarray(_tile_meta(seqs, t))
    tri = np.concatenate([np.triu(np.ones((t, t), np.float32), 1), np.ones((t, t), np.float32)], axis=1)
    tri = jnp.asarray(tri, BF16)
    abias = jnp.asarray(_attention_bias())
    cur = lambda w: pl.BlockSpec((t, w), lambda i, m: (i, 0))
    prev = lambda w: pl.BlockSpec((ab, w), lambda i, m: (jnp.maximum(i * (t // ab) - 1, 0), 0))
    nxt = lambda w: pl.BlockSpec((ab, w), lambda i, m: (jnp.minimum((i + 1) * (t // ab), n // ab - 1), 0))
    full = lambda a: pl.BlockSpec(a.shape, lambda i, m: (0,) * a.ndim)
    hb = t // HALO
    in_specs = [
        pl.BlockSpec((t, D_MODEL), lambda i, m: (jnp.minimum(i, npt - 1), 0)),
        pl.BlockSpec((t, D_MODEL), lambda i, m: (jnp.maximum(i - npt, 0), 0)),
        pl.BlockSpec((1, 1, 6 * D_MODEL), lambda i, m: (m[0, i], 0, 0)),
        pl.BlockSpec((HALO, POOL_WIDTH), lambda i, m: (jnp.maximum(i * hb - 1, 0), 0)),
        cur(POOL_WIDTH),
        pl.BlockSpec((HALO, POOL_WIDTH), lambda i, m: (jnp.minimum((i + 1) * hb, n // HALO - 1), 0)),
        cur(ATTN_WIDTH),
        prev(KV_WIDTH), cur(KV_WIDTH), nxt(KV_WIDTH),
        prev(KV_WIDTH), cur(KV_WIDTH), nxt(KV_WIDTH),
        full(wpool_bf), full(bpool), full(pscale), full(sink), full(wout_bf), full(g2),
        full(wr3), full(rbias), full(wsgu_bf), full(wsd_bf), full(tri), full(abias),
    ]
    out_specs = [
        cur(D_MODEL),
        pl.BlockSpec((N_PLANES, t, LANES), lambda i, m: (0, i, 0)),
        pl.BlockSpec((TOP_K, t), lambda i, m: (0, i)),
        pl.BlockSpec((TOP_K, t), lambda i, m: (0, i)),
        cur(LANES),
        pl.BlockSpec((N_EXPERTS, t), lambda i, m: (0, 0)),
    ]
    out_shape = [
        jax.ShapeDtypeStruct((n, D_MODEL), F32),
        jax.ShapeDtypeStruct((N_PLANES, n, LANES), I32),
        jax.ShapeDtypeStruct((TOP_K, n), I32),
        jax.ShapeDtypeStruct((TOP_K, n), I32),
        jax.ShapeDtypeStruct((n, LANES), F32),
        jax.ShapeDtypeStruct((N_EXPERTS, t), F32),
    ]
    return pl.pallas_call(
        functools.partial(_mixer_kernel, n_prompt_tiles=npt),
        grid_spec=pltpu.PrefetchScalarGridSpec(
            num_scalar_prefetch=1, grid=(nt,), in_specs=in_specs, out_specs=out_specs,
            scratch_shapes=[pltpu.VMEM((t + 2 * HALO, POOL_WIDTH), F32),
                            pltpu.VMEM((t + 2 * ab, KV_WIDTH), BF16),
                            pltpu.VMEM((t + 2 * ab, KV_WIDTH), BF16),
                            pltpu.VMEM((N_EXPERTS, t), F32)]),
        out_shape=out_shape,
        compiler_params=pltpu.CompilerParams(dimension_semantics=("arbitrary",),
                                             vmem_limit_bytes=VMEM_LIMIT),
        name="mixer",
    )(meta, xp, xs, mod3, u, u, u, q, k, k, k, v, v, v, wpool_bf, bpool, pscale, sink, wout_bf, g2,
      wr3, rbias, wsgu_bf, wsd_bf, tri, abias)


def _dest_kernel(idx_ref, rank_ref, pstart_ref, o_ref, *, n_rows):
    rowi = lax.broadcasted_iota(I32, (N_EXPERTS, T_DEST), 0)
    pstart = pstart_ref[...]
    rows = []
    for kk in range(TOP_K):
        sel = rowi == idx_ref[kk:kk + 1, :]
        start = jnp.sum(jnp.where(sel, pstart, 0.0), axis=0, keepdims=True)
        rows.append(start.astype(I32) + rank_ref[kk:kk + 1, :])
    dest = jnp.concatenate(rows, axis=0)
    for j in range(N_PLANES):
        o_ref[:, j, :] = dest + j * n_rows


def _dest(idx_t, rank_t, pstart_col, n_rows):
    n = idx_t.shape[1]
    return pl.pallas_call(
        functools.partial(_dest_kernel, n_rows=n_rows),
        grid=(n // T_DEST,),
        in_specs=[pl.BlockSpec((TOP_K, T_DEST), lambda i: (0, i)),
                  pl.BlockSpec((TOP_K, T_DEST), lambda i: (0, i)),
                  pl.BlockSpec((N_EXPERTS, 1), lambda i: (0, 0))],
        out_specs=pl.BlockSpec((TOP_K, N_PLANES, T_DEST), lambda i: (0, 0, i)),
        out_shape=jax.ShapeDtypeStruct((TOP_K, N_PLANES, n), I32),
        name="dest",
    )(idx_t, rank_t, pstart_col)


def _sc_mesh():
    return plsc.VectorSubcoreMesh(core_axis_name="c", subcore_axis_name="s")


def _sc_worker():
    return lax.axis_index("s") * 2 + lax.axis_index("c")


def _sc_scatter_rows(table, didx, n_out_rows):
    m = table.shape[0]
    per_worker = m // SC_ROWS // SC_WORKERS

    @functools.partial(
        pl.kernel, mesh=_sc_mesh(),
        out_type=jax.ShapeDtypeStruct((n_out_rows, LANES), I32),
        scratch_types=[pltpu.VMEM((SC_ROWS, LANES), I32), pltpu.VMEM((TOP_K, SC_ROWS), I32)])
    def run(table_hbm, didx_hbm, out_hbm, rows_v, idx_v):
        wid = _sc_worker()

        @pl.loop(0, per_worker)
        def _(step):
            off = pl.multiple_of((wid * per_worker + step) * SC_ROWS, SC_ROWS)
            pltpu.sync_copy(table_hbm.at[pl.ds(off, SC_ROWS)], rows_v)
            pltpu.sync_copy(didx_hbm.at[:, pl.ds(off, SC_ROWS)], idx_v)
            for kk in range(TOP_K):
                pltpu.sync_copy(rows_v, out_hbm.at[idx_v.at[kk]])

    return run(table, didx)


def _sc_gather_rows(table, didx):
    m = didx.shape[1]
    per_worker = m // SC_ROWS // SC_WORKERS
    assert per_worker % 2 == 0 and (2 * TOP_K) % SC_NBUF == 0
    items = 2 * TOP_K

    @functools.partial(
        pl.kernel, mesh=_sc_mesh(),
        out_type=jax.ShapeDtypeStruct((TOP_K, m, LANES), I32),
        scratch_types=[pltpu.VMEM((SC_NBUF, SC_ROWS, LANES), I32), pltpu.VMEM((2, TOP_K, SC_ROWS), I32),
                       pltpu.SemaphoreType.DMA((SC_NBUF,)), pltpu.SemaphoreType.DMA((SC_NBUF,)),
                       pltpu.SemaphoreType.DMA((2,))])
    def run(table_hbm, didx_hbm, out_hbm, rows, idx, gsem, wsem, isem):
        wid = _sc_worker()

        def off_of(step):
            return pl.multiple_of((wid * per_worker + step) * SC_ROWS, SC_ROWS)

        def idx_copy(step, p):
            return pltpu.make_async_copy(didx_hbm.at[:, pl.ds(off_of(step), SC_ROWS)], idx.at[p], isem.at[p])

        def gather(j):
            r = j % SC_NBUF
            return pltpu.make_async_copy(table_hbm.at[idx.at[j // TOP_K].at[j % TOP_K]], rows.at[r], gsem.at[r])

        def write(s0, j):
            r = j % SC_NBUF
            return pltpu.make_async_copy(rows.at[r], out_hbm.at[j % TOP_K, pl.ds(off_of(s0 + j // TOP_K), SC_ROWS)],
                                         wsem.at[r])

        def retire_write(s0, j):
            if j >= 0:
                write(s0, j).wait()
            else:
                @pl.when(s0 > 0)
                def _():
                    write(s0 - 2, j + items).wait()

        def finish_read(s0, j):
            if j >= 0:
                gather(j).wait()
                write(s0, j).start()
            else:
                @pl.when(s0 > 0)
                def _():
                    gather(j + items).wait()
                    write(s0 - 2, j + items).start()

        idx_copy(0, 0).start()

        @pl.loop(0, per_worker, step=2)
        def _(s0):
            for j in range(items):
                p, kk = j // TOP_K, j % TOP_K
                if kk == 0:
                    idx_copy(s0 + p, p).wait()
                retire_write(s0, j - SC_NBUF)
                gather(j).start()
                finish_read(s0, j - SC_LAG)
                if kk == SC_LAG:
                    @pl.when(s0 + p + 1 < per_worker)
                    def _():
                        idx_copy(s0 + p + 1, 1 - p).start()

        last = per_worker - 2
        for j in range(items - SC_LAG, items):
            gather(j).wait()
            write(last, j).start()
        for j in range(items - SC_NBUF, items):
            write(last, j).wait()

    return run(table, didx)


def _experts_kernel(blk_e_ref, blk_valid_ref, x_ref, wg_ref, wu_ref, wd_ref, y_ref, wgu_s, wd_s):
    b = pl.program_id(0)
    e = blk_e_ref[b]
    e_prev = blk_e_ref[jnp.maximum(b - 1, 0)]

    @pl.when((b == 0) | (e != e_prev))
    def _():
        for j in range(N_PLANES):
            hi = pl.ds(j * LANES, LANES)
            lo = pl.ds(HALF + j * LANES, LANES)
            r_hi = pl.ds(2 * j * LANES, LANES)
            r_lo = pl.ds((2 * j + 1) * LANES, LANES)
            wgu_s[r_hi, pl.ds(0, D_EXPERT)] = wg_ref[0, hi, :].astype(BF16)
            wgu_s[r_lo, pl.ds(0, D_EXPERT)] = wg_ref[0, lo, :].astype(BF16)
            wgu_s[r_hi, pl.ds(D_EXPERT, D_EXPERT)] = wu_ref[0, hi, :].astype(BF16)
            wgu_s[r_lo, pl.ds(D_EXPERT, D_EXPERT)] = wu_ref[0, lo, :].astype(BF16)
        wd_s[...] = wd_ref[0].astype(BF16)

    bm = x_ref.shape[1]
    live = lax.broadcasted_iota(I32, (bm, LANES), 0) < blk_valid_ref[b]
    xs = []
    for j in range(N_PLANES):
        hi, lo = _unpack_plane(jnp.where(live, x_ref[j], 0))
        xs += [hi.astype(BF16), lo.astype(BF16)]
    acc = jnp.dot(jnp.concatenate(xs, axis=1), wgu_s[...], preferred_element_type=F32)
    gate, up = acc[:, :D_EXPERT], acc[:, D_EXPERT:]
    act = (gate * jax.nn.sigmoid(gate)) * up
    y = jnp.dot(act.astype(BF16), wd_s[...], preferred_element_type=F32)
    planes = _pack_planes(y)
    for j in range(N_PLANES):
        y_ref[j] = planes[j]


def _expert_block_rows(n_tokens):
    load = n_tokens * TOP_K / N_EXPERTS
    blocks = max(1, round(load / EXPERT_BLOCK_TARGET))
    rows = (load + EXPERT_LOAD_SLACK * load ** 0.5) / blocks
    return int(min(max(-(-rows // LANES) * LANES, LANES), EXPERT_BLOCK_MAX))


def _experts(xs_planes, w_gate, w_up, w_down, blk_e, blk_valid, nused, bm):
    n_rows = xs_planes.shape[1]
    row_map = lambda b, be, bv: (0, b, 0)
    w_map = lambda b, be, bv: (be[b], 0, 0)
    return pl.pallas_call(
        _experts_kernel,
        grid_spec=pltpu.PrefetchScalarGridSpec(
            num_scalar_prefetch=2, grid=(nused,),
            in_specs=[pl.BlockSpec((N_PLANES, bm, LANES), row_map),
                      pl.BlockSpec((1, D_MODEL, D_EXPERT), w_map),
                      pl.BlockSpec((1, D_MODEL, D_EXPERT), w_map),
                      pl.BlockSpec((1, D_EXPERT, D_MODEL), w_map)],
            out_specs=pl.BlockSpec((N_PLANES, bm, LANES), row_map),
            scratch_shapes=[pltpu.VMEM((D_MODEL, 2 * D_EXPERT), BF16),
                            pltpu.VMEM((D_EXPERT, D_MODEL), BF16)]),
        out_shape=jax.ShapeDtypeStruct((N_PLANES, n_rows, LANES), I32),
        compiler_params=pltpu.CompilerParams(dimension_semantics=("arbitrary",),
                                             vmem_limit_bytes=VMEM_LIMIT),
        name="experts",
    )(blk_e, blk_valid, xs_planes, w_gate, w_up, w_down)


def _final_kernel(xa_ref, yk_ref, wts_ref, mod_ref, g_ref, o_ref):
    wts = wts_ref[...]
    his = [jnp.zeros((T_FIN, LANES), F32) for _ in range(N_PLANES)]
    los = [jnp.zeros((T_FIN, LANES), F32) for _ in range(N_PLANES)]
    for kk in range(TOP_K):
        wk = wts[:, kk:kk + 1]
        for j in range(N_PLANES):
            hi, lo = _unpack_plane(yk_ref[kk, j])
            his[j] = his[j] + wk * hi
            los[j] = los[j] + wk * lo
    routed = jnp.concatenate(his + los, axis=1)
    gt2 = mod_ref[0][:, 5 * D_MODEL:6 * D_MODEL]
    x = xa_ref[...] + gt2 * routed
    ms = jnp.mean(x * x, axis=-1, keepdims=True)
    o_ref[...] = x * lax.rsqrt(ms + EPS) * g_ref[...]


def _final(xa, yk, wts, mod3, g_final, tile_off, n_tok, seq_len, sid0):
    tiles_per_seq = seq_len // T_FIN
    return pl.pallas_call(
        _final_kernel,
        grid=(n_tok // T_FIN,),
        in_specs=[pl.BlockSpec((T_FIN, D_MODEL), lambda i: (i + tile_off, 0)),
                  pl.BlockSpec((TOP_K, N_PLANES, T_FIN, LANES), lambda i: (0, 0, i + tile_off, 0)),
                  pl.BlockSpec((T_FIN, LANES), lambda i: (i + tile_off, 0)),
                  pl.BlockSpec((1, 1, 6 * D_MODEL), lambda i: (sid0 + i // tiles_per_seq, 0, 0)),
                  pl.BlockSpec((1, D_MODEL), lambda i: (0, 0))],
        out_specs=pl.BlockSpec((T_FIN, D_MODEL), lambda i: (i, 0)),
        out_shape=jax.ShapeDtypeStruct((n_tok, D_MODEL), F32),
        compiler_params=pltpu.CompilerParams(dimension_semantics=("arbitrary",),
                                             vmem_limit_bytes=VMEM_LIMIT),
        name="final",
    )(xa, yk, wts, mod3, g_final)


def kernel(x_prompt, x_sample, c_prompt, c_sample, w_ada, b_ada, g_norm1, w_in, w_pool, b_pool, pool_scale, attn_sink, w_out, g_norm2, w_router, router_bias, w_gate, w_up, w_down, ws_gate, ws_up, ws_down, g_final):
    assert w_ada.shape[0] == 1, "one layer"
    bp, sp, d = x_prompt.shape
    bs, ss, _ = x_sample.shape
    assert d == D_MODEL and bp + bs <= ADA_ROWS
    seqs = ((bp, sp), (bs, ss))
    n_p, n_s = bp * sp, bs * ss
    n = n_p + n_s
    xp = x_prompt.reshape(n_p, d)
    xs = x_sample.reshape(n_s, d)

    c_all = jnp.concatenate([c_prompt, c_sample, jnp.zeros((ADA_ROWS - bp - bs, d), F32)], axis=0)
    mod3 = _ada(c_all, w_ada[0], b_ada[0]).reshape(ADA_ROWS, 1, 6 * d)

    q0, q1 = POOL_WIDTH, POOL_WIDTH + ATTN_WIDTH
    wq = w_in[0][:, q0:q1].reshape(d, N_KV_HEADS, Q_PER_KV, HEAD_DIM).transpose(0, 2, 1, 3).reshape(d, ATTN_WIDTH)
    w_in_p = jnp.concatenate([w_in[0][:, :q0], wq, w_in[0][:, q1:]], axis=1).astype(BF16)
    wo = w_out[0][q0:].reshape(N_KV_HEADS, Q_PER_KV, HEAD_DIM, d).transpose(1, 0, 2, 3).reshape(ATTN_WIDTH, d)
    w_out_p = jnp.concatenate([w_out[0][:q0], wo], axis=0).astype(BF16)
    u, q, k, v = _inproj(xp, xs, mod3, g_norm1[0].reshape(1, d), w_in_p, seqs)

    wsgu = jnp.concatenate([ws_gate[0], ws_up[0]], axis=1).astype(BF16)
    wr_hi = w_router[0].astype(BF16)
    wr_lo = (w_router[0] - wr_hi.astype(F32)).astype(BF16)
    wr3 = jnp.concatenate([wr_hi, wr_lo, wr_hi], axis=0).T
    xa, h2p, idx_t, rank_t, wts, cnt = _mixer(
        xp, xs, mod3, u, q, k, v,
        w_pool[0].astype(BF16), b_pool[0].reshape(1, POOL_WIDTH), pool_scale[0].reshape(1, POOL_WIDTH),
        attn_sink[0].reshape(1, N_HEADS), w_out_p, g_norm2[0].reshape(1, d),
        wr3, router_bias[0].reshape(N_EXPERTS, 1), wsgu, ws_down[0].astype(BF16), seqs)

    bm = _expert_block_rows(n)
    nblk = -(-n * TOP_K // bm) + N_EXPERTS
    n_rows = nblk * bm
    counts = cnt[:, 0].astype(I32)
    padded = (counts + bm - 1) // bm * bm
    pend = jnp.cumsum(padded)
    pstart = pend - padded
    blk_start = jnp.arange(nblk, dtype=I32) * bm
    blk_e = jnp.minimum(jnp.sum((blk_start[:, None] >= pend[None, :]).astype(I32), axis=1), N_EXPERTS - 1)
    own = blk_e[:, None] == jnp.arange(N_EXPERTS, dtype=I32)[None, :]
    seg_end = jnp.sum(jnp.where(own, (pstart + counts)[None, :], 0), axis=1)
    blk_valid = jnp.clip(seg_end - blk_start, 0, bm).astype(I32)
    nused = (pend[-1] // bm).astype(I32)

    didx = _dest(idx_t, rank_t, pstart.astype(F32).reshape(N_EXPERTS, 1), n_rows)
    didx = didx.reshape(TOP_K, N_PLANES * n)

    xs_rows = _sc_scatter_rows(h2p.reshape(N_PLANES * n, LANES), didx, N_PLANES * n_rows)
    ys_rows = _experts(xs_rows.reshape(N_PLANES, n_rows, LANES), w_gate[0], w_up[0], w_down[0],
                       blk_e, blk_valid, nused, bm)
    yk = _sc_gather_rows(ys_rows.reshape(N_PLANES * n_rows, LANES), didx)
    yk = yk.reshape(TOP_K, N_PLANES, n, LANES)

    gf = g_final.reshape(1, d)
    y_p = _final(xa, yk, wts, mod3, gf, 0, n_p, sp, 0)
    y_s = _final(xa, yk, wts, mod3, gf, n_p // T_FIN, n_s, ss, bp)
    return y_p.reshape(bp, sp, d), y_s.reshape(bs, ss, d)
```

```python
import functools

import numpy as np
import jax
import jax.numpy as jnp
from jax import lax
from jax.experimental import pallas as pl
from jax.experimental.pallas import tpu as pltpu
from jax.experimental.pallas import tpu_sc as plsc

F32 = jnp.float32
BF16 = jnp.bfloat16
I32 = jnp.int32

D_MODEL = 1024
POOL_WINDOWS = (2, 4, 8, 16)
POOL_WIDTH = 512
POOL_GROUP = 128
N_HEADS = 8
N_KV_HEADS = 2
HEAD_DIM = 64
Q_PER_KV = N_HEADS // N_KV_HEADS
ATTN_WIDTH = N_HEADS * HEAD_DIM
KV_WIDTH = N_KV_HEADS * HEAD_DIM
D_IN_PROJ = POOL_WIDTH + ATTN_WIDTH + 2 * KV_WIDTH
WINDOW = 128
N_EXPERTS = 256
TOP_K = 8
N_EXPERT_GROUPS = 8
GROUP_SIZE = N_EXPERTS // N_EXPERT_GROUPS
TOPK_GROUPS = 4
D_EXPERT = 256
D_SHARED = 256
ROUTED_SCALE = 2.5
EPS = 1e-6
NEG_INF = -1e30

LANES = 128
HALO = 8
N_PLANES = 4
HALF = D_MODEL // 2

T_IN = 512
T_MIX = 256
ATT_BLOCK = 128
T_DEST = 512
EXPERT_CHUNK = 512
EXPERT_NBUF = 4
T_FIN = 256
SC_ROWS = 128
SC_WORKERS = 32
SC_NBUF = 4
SC_LAG = 2
ADA_ROWS = 16
VMEM_LIMIT = 48 * 1024 * 1024

LOG2E = 1.4426950408889634
ALIBI_SLOPES = tuple(float(2.0 ** (-8.0 * (h + 1) / N_HEADS)) for h in range(N_HEADS))


def _pack_planes(y):
    planes = []
    for j in range(N_PLANES):
        hi = y[:, j * LANES:(j + 1) * LANES].astype(BF16).astype(F32)
        lo = y[:, HALF + j * LANES:HALF + (j + 1) * LANES].astype(BF16).astype(F32)
        hb = lax.bitcast_convert_type(hi, jnp.uint32) & jnp.uint32(0xFFFF0000)
        lb = lax.bitcast_convert_type(lo, jnp.uint32) >> jnp.uint32(16)
        planes.append(lax.bitcast_convert_type(hb | lb, I32))
    return planes


def _unpack_plane(w):
    u = lax.bitcast_convert_type(w, jnp.uint32)
    hi = lax.bitcast_convert_type(u & jnp.uint32(0xFFFF0000), F32)
    lo = lax.bitcast_convert_type(u << jnp.uint32(16), F32)
    return hi, lo


def _ada_kernel(c_ref, w_ref, b_ref, o_ref):
    c = c_ref[...]
    s = c * jax.nn.sigmoid(c)
    o_ref[...] = jnp.dot(s, w_ref[...], precision=lax.Precision.HIGHEST,
                         preferred_element_type=F32) + b_ref[...]


def _ada(c_all, w_ada, b_ada):
    n_out = w_ada.shape[1]
    tn = 1024
    return pl.pallas_call(
        _ada_kernel,
        grid=(n_out // tn,),
        in_specs=[pl.BlockSpec((ADA_ROWS, D_MODEL), lambda j: (0, 0)),
                  pl.BlockSpec((D_MODEL, tn), lambda j: (0, j)),
                  pl.BlockSpec((1, tn), lambda j: (0, j))],
        out_specs=pl.BlockSpec((ADA_ROWS, tn), lambda j: (0, j)),
        out_shape=jax.ShapeDtypeStruct((ADA_ROWS, n_out), F32),
        name="ada",
    )(c_all, w_ada, b_ada.reshape(1, n_out))


def _rms_mod(x, g, scale, shift):
    ms = jnp.mean(x * x, axis=-1, keepdims=True)
    return (x * lax.rsqrt(ms + EPS) * g) * (1.0 + scale) + shift


def _inproj_kernel(meta_ref, xp_ref, xs_ref, mod_ref, g_ref, w_ref, u_ref, q_ref, k_ref, v_ref,
                   *, n_prompt_tiles):
    x = jnp.where(pl.program_id(0) < n_prompt_tiles, xp_ref[...], xs_ref[...])
    mod = mod_ref[0]
    h = _rms_mod(x, g_ref[...], mod[:, D_MODEL:2 * D_MODEL], mod[:, 0:D_MODEL])
    z = jnp.dot(h.astype(BF16), w_ref[...], preferred_element_type=F32)
    u_ref[...] = z[:, :POOL_WIDTH]
    q_ref[...] = (z[:, POOL_WIDTH:POOL_WIDTH + ATTN_WIDTH] * (LOG2E * HEAD_DIM ** -0.5)).astype(BF16)
    k_ref[...] = z[:, POOL_WIDTH + ATTN_WIDTH:POOL_WIDTH + ATTN_WIDTH + KV_WIDTH].astype(BF16)
    v_ref[...] = z[:, POOL_WIDTH + ATTN_WIDTH + KV_WIDTH:].astype(BF16)


def _tile_meta(seqs, tile):
    rows = []
    sid = 0
    for count, length in seqs:
        for _ in range(count):
            n = length // tile
            for t in range(n):
                rows.append((sid, int(t == 0), int(t == n - 1), t * tile, length))
            sid += 1
    return np.asarray(rows, np.int32).T.copy()


def _inproj(xp, xs, mod3, g1, w_in_bf, seqs):
    n_p, n_s = xp.shape[0], xs.shape[0]
    n = n_p + n_s
    npt = n_p // T_IN
    meta = jnp.asarray(_tile_meta(seqs, T_IN))
    tok = lambda w: pl.BlockSpec((T_IN, w), lambda i, m: (i, 0))
    return pl.pallas_call(
        functools.partial(_inproj_kernel, n_prompt_tiles=npt),
        grid_spec=pltpu.PrefetchScalarGridSpec(
            num_scalar_prefetch=1,
            grid=(n // T_IN,),
            in_specs=[
                pl.BlockSpec((T_IN, D_MODEL), lambda i, m: (jnp.minimum(i, npt - 1), 0)),
                pl.BlockSpec((T_IN, D_MODEL), lambda i, m: (jnp.maximum(i - npt, 0), 0)),
                pl.BlockSpec((1, 1, 6 * D_MODEL), lambda i, m: (m[0, i], 0, 0)),
                pl.BlockSpec((1, D_MODEL), lambda i, m: (0, 0)),
                pl.BlockSpec((D_MODEL, D_IN_PROJ), lambda i, m: (0, 0)),
            ],
            out_specs=[tok(POOL_WIDTH), tok(ATTN_WIDTH), tok(KV_WIDTH), tok(KV_WIDTH)],
        ),
        out_shape=[jax.ShapeDtypeStruct((n, POOL_WIDTH), F32),
                   jax.ShapeDtypeStruct((n, ATTN_WIDTH), BF16),
                   jax.ShapeDtypeStruct((n, KV_WIDTH), BF16),
                   jax.ShapeDtypeStruct((n, KV_WIDTH), BF16)],
        compiler_params=pltpu.CompilerParams(dimension_semantics=("arbitrary",),
                                             vmem_limit_bytes=VMEM_LIMIT),
        name="inproj",
    )(meta, xp, xs, mod3, g1, w_in_bf)


def _pool_mixer(uext_ref, uc, pos, seq_len, wpool_ref, bpool_ref, pscale_ref):
    outs = []
    for gi, w in enumerate(POOL_WINDOWS):
        c0 = gi * POOL_GROUP
        half = w // 2
        acc = uext_ref[pl.ds(HALO - half, T_MIX), pl.ds(c0, POOL_GROUP)]
        for o in range(-half + 1, half):
            acc = acc + uext_ref[pl.ds(HALO + o, T_MIX), pl.ds(c0, POOL_GROUP)]
        lo = jnp.maximum(pos - half, 0)
        hi = jnp.minimum(pos + half, seq_len)
        cnt = (hi - lo).astype(F32)
        d = acc / cnt - uc[:, c0:c0 + POOL_GROUP]
        y = jnp.dot(d.astype(BF16), wpool_ref[gi], preferred_element_type=F32)
        y = (y + bpool_ref[:, c0:c0 + POOL_GROUP]) * pscale_ref[:, c0:c0 + POOL_GROUP]
        outs.append(y)
    return jnp.concatenate(outs, axis=1)


def _attention_bias():
    t = ATT_BLOCK
    r = np.arange(t)[:, None]
    c = np.arange(3 * t)[None, :]
    dist = np.abs(r - (c - t))
    band = dist <= WINDOW
    out = np.empty((3, N_HEADS, t, 3 * t), np.float32)
    for var, valid in enumerate((band, band & (c >= t), band & (c < 2 * t))):
        for h in range(N_HEADS):
            out[var, h] = np.where(valid, -ALIBI_SLOPES[h] * LOG2E * dist, NEG_INF)
    return out.reshape(3, N_HEADS * t, 3 * t)


def _banded_attention(q, kw, vw, variant, bias_ref, sink_ref):
    t = ATT_BLOCK
    low = lax.broadcasted_iota(I32, (t, LANES), 1) < HEAD_DIM
    zero = jnp.zeros((t, LANES), BF16)
    qs = []
    for h in range(N_HEADS):
        blk = q[:, (h % Q_PER_KV) * LANES:(h % Q_PER_KV + 1) * LANES]
        qs.append(jnp.where(low, blk, zero) if h < Q_PER_KV else jnp.where(low, zero, blk))
    s = lax.dot_general(jnp.concatenate(qs, axis=0), kw, (((1,), (1,)), ((), ())),
                        preferred_element_type=F32)
    ps, dens = [], []
    for h in range(N_HEADS):
        sh = s[h * t:(h + 1) * t] + bias_ref[variant, pl.ds(h * t, t), :]
        sink = sink_ref[:, h:h + 1] * LOG2E
        m = jnp.maximum(jnp.max(sh, axis=-1, keepdims=True), sink)
        p = jnp.exp2(sh - m)
        dens.append(jnp.sum(p, axis=-1, keepdims=True) + jnp.exp2(sink - m))
        ps.append(p.astype(BF16))
    o = jnp.dot(jnp.concatenate(ps, axis=0), vw, preferred_element_type=F32)
    oh = [o[h * t:(h + 1) * t] / dens[h] for h in range(N_HEADS)]
    return jnp.concatenate([jnp.where(low, oh[cb], oh[Q_PER_KV + cb]) for cb in range(Q_PER_KV)], axis=1)


def _route(biased, scores):
    t = biased.shape[1]
    rowf = lax.broadcasted_iota(I32, (N_EXPERTS, t), 0).astype(F32)
    ninf = float("-inf")
    gs = []
    for g in range(N_EXPERT_GROUPS):
        blk = biased[g * GROUP_SIZE:(g + 1) * GROUP_SIZE, :]
        rf = rowf[g * GROUP_SIZE:(g + 1) * GROUP_SIZE, :]
        m1 = jnp.max(blk, axis=0, keepdims=True)
        i1 = jnp.min(jnp.where(blk == m1, rf, float(N_EXPERTS)), axis=0, keepdims=True)
        m2 = jnp.max(jnp.where(rf == i1, ninf, blk), axis=0, keepdims=True)
        gs.append(m1 + m2)
    keep = []
    for g in range(N_EXPERT_GROUPS):
        beat = jnp.zeros((1, t), F32)
        for g2 in range(N_EXPERT_GROUPS):
            if g2 == g:
                continue
            better = (gs[g2] >= gs[g]) if g2 < g else (gs[g2] > gs[g])
            beat = beat + better.astype(F32)
        keep.append(jnp.broadcast_to(beat < float(TOPK_GROUPS), (GROUP_SIZE, t)))
    emask = jnp.concatenate(keep, axis=0)
    masked = jnp.where(emask, biased, NEG_INF)
    idx_rows, w_rows = [], []
    sel_any = jnp.zeros((N_EXPERTS, t), F32)
    for _ in range(TOP_K):
        m = jnp.max(masked, axis=0, keepdims=True)
        ik = jnp.min(jnp.where(masked == m, rowf, float(N_EXPERTS)), axis=0, keepdims=True)
        sel = rowf == ik
        w_rows.append(jnp.sum(jnp.where(sel, scores, 0.0), axis=0, keepdims=True))
        idx_rows.append(ik)
        masked = jnp.where(sel, ninf, masked)
        sel_any = sel_any + sel.astype(F32)
    return idx_rows, w_rows, sel_any, rowf


def _mixer_kernel(meta_ref, xp_ref, xs_ref, mod_ref, up_ref, uc_ref, un_ref, q_ref,
                  kp_ref, kc_ref, kn_ref, vp_ref, vc_ref, vn_ref,
                  wpool_ref, bpool_ref, pscale_ref, sink_ref, wout_ref, g2_ref,
                  wr3_ref, rbias_ref, wsgu_ref, wsd_ref, tri_ref, abias_ref,
                  xa_ref, h2p_ref, idx_ref, rank_ref, wts_ref, cnt_ref,
                  uext_ref, kext_ref, vext_ref, base_ref, *, n_prompt_tiles):
    i = pl.program_id(0)
    first = meta_ref[1, i] == 1
    last = meta_ref[2, i] == 1
    pos0 = meta_ref[3, i]
    seq_len = meta_ref[4, i]
    mod = mod_ref[0]
    gt1 = mod[:, 2 * D_MODEL:3 * D_MODEL]
    sh2 = mod[:, 3 * D_MODEL:4 * D_MODEL]
    sc2 = mod[:, 4 * D_MODEL:5 * D_MODEL]
    gt2 = mod[:, 5 * D_MODEL:6 * D_MODEL]

    @pl.when(i == 0)
    def _():
        base_ref[...] = jnp.zeros_like(base_ref)

    uc = uc_ref[...]
    uext_ref[pl.ds(0, HALO), :] = jnp.where(first, 0.0, up_ref[...])
    uext_ref[pl.ds(HALO, T_MIX), :] = uc
    uext_ref[pl.ds(HALO + T_MIX, HALO), :] = jnp.where(last, 0.0, un_ref[...])
    pos = pos0 + lax.broadcasted_iota(I32, (T_MIX, 1), 0)
    a_pool = _pool_mixer(uext_ref, uc, pos, seq_len, wpool_ref, bpool_ref, pscale_ref)

    t = ATT_BLOCK
    kext_ref[pl.ds(0, t), :] = kp_ref[...]
    kext_ref[pl.ds(t, T_MIX), :] = kc_ref[...]
    kext_ref[pl.ds(t + T_MIX, t), :] = kn_ref[...]
    vext_ref[pl.ds(0, t), :] = vp_ref[...]
    vext_ref[pl.ds(t, T_MIX), :] = vc_ref[...]
    vext_ref[pl.ds(t + T_MIX, t), :] = vn_ref[...]
    n_sub = T_MIX // t
    assert n_sub >= 2, "a query block is never both first and last in its sequence"
    attn = []
    for sub in range(n_sub):
        variant = 0
        if sub == 0:
            variant = jnp.where(first, 1, variant)
        if sub == n_sub - 1:
            variant = jnp.where(last, 2, variant)
        attn.append(_banded_attention(
            q_ref[pl.ds(sub * t, t), :], kext_ref[pl.ds(sub * t, 3 * t), :], vext_ref[pl.ds(sub * t, 3 * t), :],
            variant, abias_ref, sink_ref))
    a_attn = jnp.concatenate(attn, axis=0)

    a = jnp.concatenate([a_pool, a_attn], axis=1).astype(BF16)
    mix = jnp.dot(a, wout_ref[...], preferred_element_type=F32)

    x = jnp.where(i < n_prompt_tiles, xp_ref[...], xs_ref[...])
    x1 = x + gt1 * mix
    h2 = _rms_mod(x1, g2_ref[...], sc2, sh2)
    h2b = h2.astype(BF16)
    planes = _pack_planes(h2)
    for j in range(N_PLANES):
        h2p_ref[j] = planes[j]
    gu = jnp.dot(h2b, wsgu_ref[...], preferred_element_type=F32)
    gate, up = gu[:, :D_SHARED], gu[:, D_SHARED:]
    act = (gate * jax.nn.sigmoid(gate)) * up
    shared = jnp.dot(act.astype(BF16), wsd_ref[...], preferred_element_type=F32)
    xa_ref[...] = x1 + gt2 * shared
    h2lo = (h2 - h2b.astype(F32)).astype(BF16)
    logits = lax.dot_general(wr3_ref[...], jnp.concatenate([h2b, h2b, h2lo], axis=1),
                             (((1,), (1,)), ((), ())), preferred_element_type=F32)
    scores = jax.nn.sigmoid(logits)
    biased = scores + rbias_ref[...]
    routed = [_route(biased[:, c * LANES:(c + 1) * LANES], scores[:, c * LANES:(c + 1) * LANES])
              for c in range(T_MIX // LANES)]
    for c, (idx_rows, w_rows, _, _) in enumerate(routed):
        wsum = w_rows[0]
        for wr in w_rows[1:]:
            wsum = wsum + wr
        wts_t = jnp.concatenate([wr / wsum * ROUTED_SCALE for wr in w_rows], axis=0)
        wpad = jnp.concatenate([wts_t, jnp.zeros((LANES - TOP_K, LANES), F32)], axis=0)
        wts_ref[pl.ds(c * LANES, LANES), :] = wpad.T
        idx_ref[:, pl.ds(c * LANES, LANES)] = jnp.concatenate(idx_rows, axis=0).astype(I32)
    sel_any = jnp.concatenate([r[2] for r in routed], axis=1)
    pref = jnp.dot(sel_any.astype(BF16), tri_ref[...], preferred_element_type=F32)
    before = base_ref[...] + pref[:, :T_MIX]
    for c, (idx_rows, _, _, rowf) in enumerate(routed):
        bc = before[:, c * LANES:(c + 1) * LANES]
        ranks = [jnp.sum(jnp.where(rowf == ik, bc, 0.0), axis=0, keepdims=True) for ik in idx_rows]
        rank_ref[:, pl.ds(c * LANES, LANES)] = jnp.concatenate(ranks, axis=0).astype(I32)
    new_base = base_ref[...] + pref[:, T_MIX:]
    base_ref[...] = new_base
    cnt_ref[...] = new_base


def _mixer(xp, xs, mod3, u, q, k, v, wpool_bf, bpool, pscale, sink, wout_bf, g2, wr3, rbias,
           wsgu_bf, wsd_bf, seqs):
    n_p, n_s = xp.shape[0], xs.shape[0]
    n = n_p + n_s
    t = T_MIX
    nt = n // t
    npt = n_p // t
    ab = ATT_BLOCK
    meta = jnp.asarray(_tile_meta(seqs, t))
    tri = np.concatenate([np.triu(np.ones((t, t), np.float32), 1), np.ones((t, t), np.float32)], axis=1)
    tri = jnp.asarray(tri, BF16)
    abias = jnp.asarray(_attention_bias())
    cur = lambda w: pl.BlockSpec((t, w), lambda i, m: (i, 0))
    prev = lambda w: pl.BlockSpec((ab, w), lambda i, m: (jnp.maximum(i * (t // ab) - 1, 0), 0))
    nxt = lambda w: pl.BlockSpec((ab, w), lambda i, m: (jnp.minimum((i + 1) * (t // ab), n // ab - 1), 0))
    full = lambda a: pl.BlockSpec(a.shape, lambda i, m: (0,) * a.ndim)
    hb = t // HALO
    in_specs = [
        pl.BlockSpec((t, D_MODEL), lambda i, m: (jnp.minimum(i, npt - 1), 0)),
        pl.BlockSpec((t, D_MODEL), lambda i, m: (jnp.maximum(i - npt, 0), 0)),
        pl.BlockSpec((1, 1, 6 * D_MODEL), lambda i, m: (m[0, i], 0, 0)),
        pl.BlockSpec((HALO, POOL_WIDTH), lambda i, m: (jnp.maximum(i * hb - 1, 0), 0)),
        cur(POOL_WIDTH),
        pl.BlockSpec((HALO, POOL_WIDTH), lambda i, m: (jnp.minimum((i + 1) * hb, n // HALO - 1), 0)),
        cur(ATTN_WIDTH),
        prev(KV_WIDTH), cur(KV_WIDTH), nxt(KV_WIDTH),
        prev(KV_WIDTH), cur(KV_WIDTH), nxt(KV_WIDTH),
        full(wpool_bf), full(bpool), full(pscale), full(sink), full(wout_bf), full(g2),
        full(wr3), full(rbias), full(wsgu_bf), full(wsd_bf), full(tri), full(abias),
    ]
    out_specs = [
        cur(D_MODEL),
        pl.BlockSpec((N_PLANES, t, LANES), lambda i, m: (0, i, 0)),
        pl.BlockSpec((TOP_K, t), lambda i, m: (0, i)),
        pl.BlockSpec((TOP_K, t), lambda i, m: (0, i)),
        cur(LANES),
        pl.BlockSpec((N_EXPERTS, t), lambda i, m: (0, 0)),
    ]
    out_shape = [
        jax.ShapeDtypeStruct((n, D_MODEL), F32),
        jax.ShapeDtypeStruct((N_PLANES, n, LANES), I32),
        jax.ShapeDtypeStruct((TOP_K, n), I32),
        jax.ShapeDtypeStruct((TOP_K, n), I32),
        jax.ShapeDtypeStruct((n, LANES), F32),
        jax.ShapeDtypeStruct((N_EXPERTS, t), F32),
    ]
    return pl.pallas_call(
        functools.partial(_mixer_kernel, n_prompt_tiles=npt),
        grid_spec=pltpu.PrefetchScalarGridSpec(
            num_scalar_prefetch=1, grid=(nt,), in_specs=in_specs, out_specs=out_specs,
            scratch_shapes=[pltpu.VMEM((t + 2 * HALO, POOL_WIDTH), F32),
                            pltpu.VMEM((t + 2 * ab, KV_WIDTH), BF16),
                            pltpu.VMEM((t + 2 * ab, KV_WIDTH), BF16),
                            pltpu.VMEM((N_EXPERTS, t), F32)]),
        out_shape=out_shape,
        compiler_params=pltpu.CompilerParams(dimension_semantics=("arbitrary",),
                                             vmem_limit_bytes=VMEM_LIMIT),
        name="mixer",
    )(meta, xp, xs, mod3, u, u, u, q, k, k, k, v, v, v, wpool_bf, bpool, pscale, sink, wout_bf, g2,
      wr3, rbias, wsgu_bf, wsd_bf, tri, abias)


def _dest_kernel(idx_ref, rank_ref, pstart_ref, o_ref, *, n_rows):
    rowi = lax.broadcasted_iota(I32, (N_EXPERTS, T_DEST), 0)
    pstart = pstart_ref[...]
    rows = []
    for kk in range(TOP_K):
        sel = rowi == idx_ref[kk:kk + 1, :]
        start = jnp.sum(jnp.where(sel, pstart, 0.0), axis=0, keepdims=True)
        rows.append(start.astype(I32) + rank_ref[kk:kk + 1, :])
    dest = jnp.concatenate(rows, axis=0)
    for j in range(N_PLANES):
        o_ref[:, j, :] = dest + j * n_rows


def _dest(idx_t, rank_t, pstart_col, n_rows):
    n = idx_t.shape[1]
    return pl.pallas_call(
        functools.partial(_dest_kernel, n_rows=n_rows),
        grid=(n // T_DEST,),
        in_specs=[pl.BlockSpec((TOP_K, T_DEST), lambda i: (0, i)),
                  pl.BlockSpec((TOP_K, T_DEST), lambda i: (0, i)),
                  pl.BlockSpec((N_EXPERTS, 1), lambda i: (0, 0))],
        out_specs=pl.BlockSpec((TOP_K, N_PLANES, T_DEST), lambda i: (0, 0, i)),
        out_shape=jax.ShapeDtypeStruct((TOP_K, N_PLANES, n), I32),
        name="dest",
    )(idx_t, rank_t, pstart_col)


def _sc_mesh():
    return plsc.VectorSubcoreMesh(core_axis_name="c", subcore_axis_name="s")


def _sc_worker():
    return lax.axis_index("s") * 2 + lax.axis_index("c")


def _sc_scatter_rows(table, didx, n_out_rows):
    m = table.shape[0]
    per_worker = m // SC_ROWS // SC_WORKERS
    assert per_worker % 2 == 0

    @functools.partial(
        pl.kernel, mesh=_sc_mesh(),
        out_type=jax.ShapeDtypeStruct((n_out_rows, LANES), I32),
        scratch_types=[pltpu.VMEM((2, SC_ROWS, LANES), I32), pltpu.VMEM((2, TOP_K, SC_ROWS), I32),
                       pltpu.SemaphoreType.DMA((2,))])
    def run(table_hbm, didx_hbm, out_hbm, rows, idx, ssem):
        wid = _sc_worker()

        def scatter(p, kk):
            return pltpu.make_async_copy(rows.at[p], out_hbm.at[idx.at[p].at[kk]], ssem.at[p])

        @pl.loop(0, per_worker, step=2)
        def _(s0):
            for p in range(2):
                @pl.when(s0 > 0)
                def _():
                    for kk in range(TOP_K):
                        scatter(p, kk).wait()

                off = pl.multiple_of((wid * per_worker + s0 + p) * SC_ROWS, SC_ROWS)
                pltpu.sync_copy(table_hbm.at[pl.ds(off, SC_ROWS)], rows.at[p])
                pltpu.sync_copy(didx_hbm.at[:, pl.ds(off, SC_ROWS)], idx.at[p])
                for kk in range(TOP_K):
                    scatter(p, kk).start()

        for p in range(2):
            for kk in range(TOP_K):
                scatter(p, kk).wait()

    return run(table, didx)


def _sc_gather_rows(table, didx):
    m = didx.shape[1]
    per_worker = m // SC_ROWS // SC_WORKERS
    assert per_worker % 2 == 0 and (2 * TOP_K) % SC_NBUF == 0
    items = 2 * TOP_K

    @functools.partial(
        pl.kernel, mesh=_sc_mesh(),
        out_type=jax.ShapeDtypeStruct((TOP_K, m, LANES), I32),
        scratch_types=[pltpu.VMEM((SC_NBUF, SC_ROWS, LANES), I32), pltpu.VMEM((2, TOP_K, SC_ROWS), I32),
                       pltpu.SemaphoreType.DMA((SC_NBUF,)), pltpu.SemaphoreType.DMA((SC_NBUF,)),
                       pltpu.SemaphoreType.DMA((2,))])
    def run(table_hbm, didx_hbm, out_hbm, rows, idx, gsem, wsem, isem):
        wid = _sc_worker()

        def off_of(step):
            return pl.multiple_of((wid * per_worker + step) * SC_ROWS, SC_ROWS)

        def idx_copy(step, p):
            return pltpu.make_async_copy(didx_hbm.at[:, pl.ds(off_of(step), SC_ROWS)], idx.at[p], isem.at[p])

        def gather(j):
            r = j % SC_NBUF
            return pltpu.make_async_copy(table_hbm.at[idx.at[j // TOP_K].at[j % TOP_K]], rows.at[r], gsem.at[r])

        def write(s0, j):
            r = j % SC_NBUF
            return pltpu.make_async_copy(rows.at[r], out_hbm.at[j % TOP_K, pl.ds(off_of(s0 + j // TOP_K), SC_ROWS)],
                                         wsem.at[r])

        def retire_write(s0, j):
            if j >= 0:
                write(s0, j).wait()
            else:
                @pl.when(s0 > 0)
                def _():
                    write(s0 - 2, j + items).wait()

        def finish_read(s0, j):
            if j >= 0:
                gather(j).wait()
                write(s0, j).start()
            else:
                @pl.when(s0 > 0)
                def _():
                    gather(j + items).wait()
                    write(s0 - 2, j + items).start()

        idx_copy(0, 0).start()

        @pl.loop(0, per_worker, step=2)
        def _(s0):
            for j in range(items):
                p, kk = j // TOP_K, j % TOP_K
                if kk == 0:
                    idx_copy(s0 + p, p).wait()
                retire_write(s0, j - SC_NBUF)
                gather(j).start()
                finish_read(s0, j - SC_LAG)
                if kk == SC_LAG:
                    @pl.when(s0 + p + 1 < per_worker)
                    def _():
                        idx_copy(s0 + p + 1, 1 - p).start()

        last = per_worker - 2
        for j in range(items - SC_LAG, items):
            gather(j).wait()
            write(last, j).start()
        for j in range(items - SC_NBUF, items):
            write(last, j).wait()

    return run(table, didx)


def _experts_kernel(cstart_ref, nchunk_ref, count_ref, total_ref, x_hbm, wg_ref, wu_ref, wd_ref, y_hbm,
                    wgu_s, wd_s, xbuf, ybuf, xsem, ysem):
    e = pl.program_id(0)
    nb = xbuf.shape[0]
    ch = xbuf.shape[2]
    total = total_ref[0]

    def x_copy(c):
        slot = lax.rem(c, nb)
        rows = pl.ds(pl.multiple_of(c * ch, ch), ch)
        return pltpu.make_async_copy(x_hbm.at[:, rows, :], xbuf.at[slot], xsem.at[slot])

    def y_copy(c):
        slot = lax.rem(c, nb)
        rows = pl.ds(pl.multiple_of(c * ch, ch), ch)
        return pltpu.make_async_copy(ybuf.at[slot], y_hbm.at[:, rows, :], ysem.at[slot])

    @pl.when(e == 0)
    def _():
        for c in range(nb - 1):
            @pl.when(c < total)
            def _():
                x_copy(c).start()

    for j in range(N_PLANES):
        hi = pl.ds(j * LANES, LANES)
        lo = pl.ds(HALF + j * LANES, LANES)
        r_hi = pl.ds(2 * j * LANES, LANES)
        r_lo = pl.ds((2 * j + 1) * LANES, LANES)
        wgu_s[r_hi, pl.ds(0, D_EXPERT)] = wg_ref[0, hi, :].astype(BF16)
        wgu_s[r_lo, pl.ds(0, D_EXPERT)] = wg_ref[0, lo, :].astype(BF16)
        wgu_s[r_hi, pl.ds(D_EXPERT, D_EXPERT)] = wu_ref[0, hi, :].astype(BF16)
        wgu_s[r_lo, pl.ds(D_EXPERT, D_EXPERT)] = wu_ref[0, lo, :].astype(BF16)
    wd_s[...] = wd_ref[0].astype(BF16)

    c0 = cstart_ref[e]
    count = count_ref[e]

    def chunk(i, carry):
        c = c0 + i
        slot = lax.rem(c, nb)

        @pl.when(c + nb - 1 < total)
        def _():
            x_copy(c + nb - 1).start()

        x_copy(c).wait()

        @pl.when(c >= nb)
        def _():
            y_copy(c - nb).wait()

        live = lax.broadcasted_iota(I32, (ch, LANES), 0) < count - i * ch
        xs = []
        for j in range(N_PLANES):
            hi, lo = _unpack_plane(jnp.where(live, xbuf[slot, j], 0))
            xs += [hi.astype(BF16), lo.astype(BF16)]
        acc = jnp.dot(jnp.concatenate(xs, axis=1), wgu_s[...], preferred_element_type=F32)
        gate, up = acc[:, :D_EXPERT], acc[:, D_EXPERT:]
        act = (gate * jax.nn.sigmoid(gate)) * up
        y = jnp.dot(act.astype(BF16), wd_s[...], preferred_element_type=F32)
        planes = _pack_planes(y)
        for j in range(N_PLANES):
            ybuf[slot, j] = planes[j]
        y_copy(c).start()
        return carry

    lax.fori_loop(0, nchunk_ref[e], chunk, 0)

    @pl.when(e == pl.num_programs(0) - 1)
    def _():
        for back in range(nb, 0, -1):
            @pl.when(total >= back)
            def _():
                y_copy(total - back).wait()


def _experts(xs_planes, w_gate, w_up, w_down, cstart, nchunk, counts, total):
    n_rows = xs_planes.shape[1]
    ch = EXPERT_CHUNK
    w_map = lambda e, *_: (e, 0, 0)
    any_space = pl.BlockSpec(memory_space=pl.ANY)
    return pl.pallas_call(
        _experts_kernel,
        grid_spec=pltpu.PrefetchScalarGridSpec(
            num_scalar_prefetch=4, grid=(N_EXPERTS,),
            in_specs=[any_space,
                      pl.BlockSpec((1, D_MODEL, D_EXPERT), w_map),
                      pl.BlockSpec((1, D_MODEL, D_EXPERT), w_map),
                      pl.BlockSpec((1, D_EXPERT, D_MODEL), w_map)],
            out_specs=any_space,
            scratch_shapes=[pltpu.VMEM((D_MODEL, 2 * D_EXPERT), BF16),
                            pltpu.VMEM((D_EXPERT, D_MODEL), BF16),
                            pltpu.VMEM((EXPERT_NBUF, N_PLANES, ch, LANES), I32),
                            pltpu.VMEM((EXPERT_NBUF, N_PLANES, ch, LANES), I32),
                            pltpu.SemaphoreType.DMA((EXPERT_NBUF,)),
                            pltpu.SemaphoreType.DMA((EXPERT_NBUF,))]),
        out_shape=jax.ShapeDtypeStruct((N_PLANES, n_rows, LANES), I32),
        compiler_params=pltpu.CompilerParams(dimension_semantics=("arbitrary",),
                                             vmem_limit_bytes=VMEM_LIMIT),
        name="experts",
    )(cstart, nchunk, counts, total, xs_planes, w_gate, w_up, w_down)


def _final_kernel(xa_ref, yk_ref, wts_ref, mod_ref, g_ref, o_ref):
    wts = wts_ref[...]
    his = [jnp.zeros((T_FIN, LANES), F32) for _ in range(N_PLANES)]
    los = [jnp.zeros((T_FIN, LANES), F32) for _ in range(N_PLANES)]
    for kk in range(TOP_K):
        wk = wts[:, kk:kk + 1]
        for j in range(N_PLANES):
            hi, lo = _unpack_plane(yk_ref[kk, j])
            his[j] = his[j] + wk * hi
            los[j] = los[j] + wk * lo
    routed = jnp.concatenate(his + los, axis=1)
    gt2 = mod_ref[0][:, 5 * D_MODEL:6 * D_MODEL]
    x = xa_ref[...] + gt2 * routed
    ms = jnp.mean(x * x, axis=-1, keepdims=True)
    o_ref[...] = x * lax.rsqrt(ms + EPS) * g_ref[...]


def _final(xa, yk, wts, mod3, g_final, tile_off, n_tok, seq_len, sid0):
    tiles_per_seq = seq_len // T_FIN
    return pl.pallas_call(
        _final_kernel,
        grid=(n_tok // T_FIN,),
        in_specs=[pl.BlockSpec((T_FIN, D_MODEL), lambda i: (i + tile_off, 0)),
                  pl.BlockSpec((TOP_K, N_PLANES, T_FIN, LANES), lambda i: (0, 0, i + tile_off, 0)),
                  pl.BlockSpec((T_FIN, LANES), lambda i: (i + tile_off, 0)),
                  pl.BlockSpec((1, 1, 6 * D_MODEL), lambda i: (sid0 + i // tiles_per_seq, 0, 0)),
                  pl.BlockSpec((1, D_MODEL), lambda i: (0, 0))],
        out_specs=pl.BlockSpec((T_FIN, D_MODEL), lambda i: (i, 0)),
        out_shape=jax.ShapeDtypeStruct((n_tok, D_MODEL), F32),
        compiler_params=pltpu.CompilerParams(dimension_semantics=("arbitrary",),
                                             vmem_limit_bytes=VMEM_LIMIT),
        name="final",
    )(xa, yk, wts, mod3, g_final)


def kernel(x_prompt, x_sample, c_prompt, c_sample, w_ada, b_ada, g_norm1, w_in, w_pool, b_pool, pool_scale, attn_sink, w_out, g_norm2, w_router, router_bias, w_gate, w_up, w_down, ws_gate, ws_up, ws_down, g_final):
    assert w_ada.shape[0] == 1, "one layer"
    bp, sp, d = x_prompt.shape
    bs, ss, _ = x_sample.shape
    assert d == D_MODEL and bp + bs <= ADA_ROWS
    seqs = ((bp, sp), (bs, ss))
    n_p, n_s = bp * sp, bs * ss
    n = n_p + n_s
    xp = x_prompt.reshape(n_p, d)
    xs = x_sample.reshape(n_s, d)

    c_all = jnp.concatenate([c_prompt, c_sample, jnp.zeros((ADA_ROWS - bp - bs, d), F32)], axis=0)
    mod3 = _ada(c_all, w_ada[0], b_ada[0]).reshape(ADA_ROWS, 1, 6 * d)

    q0, q1 = POOL_WIDTH, POOL_WIDTH + ATTN_WIDTH
    wq = w_in[0][:, q0:q1].reshape(d, N_KV_HEADS, Q_PER_KV, HEAD_DIM).transpose(0, 2, 1, 3).reshape(d, ATTN_WIDTH)
    w_in_p = jnp.concatenate([w_in[0][:, :q0], wq, w_in[0][:, q1:]], axis=1).astype(BF16)
    wo = w_out[0][q0:].reshape(N_KV_HEADS, Q_PER_KV, HEAD_DIM, d).transpose(1, 0, 2, 3).reshape(ATTN_WIDTH, d)
    w_out_p = jnp.concatenate([w_out[0][:q0], wo], axis=0).astype(BF16)
    u, q, k, v = _inproj(xp, xs, mod3, g_norm1[0].reshape(1, d), w_in_p, seqs)

    wsgu = jnp.concatenate([ws_gate[0], ws_up[0]], axis=1).astype(BF16)
    wr_hi = w_router[0].astype(BF16)
    wr_lo = (w_router[0] - wr_hi.astype(F32)).astype(BF16)
    wr3 = jnp.concatenate([wr_hi, wr_lo, wr_hi], axis=0).T
    xa, h2p, idx_t, rank_t, wts, cnt = _mixer(
        xp, xs, mod3, u, q, k, v,
        w_pool[0].astype(BF16), b_pool[0].reshape(1, POOL_WIDTH), pool_scale[0].reshape(1, POOL_WIDTH),
        attn_sink[0].reshape(1, N_HEADS), w_out_p, g_norm2[0].reshape(1, d),
        wr3, router_bias[0].reshape(N_EXPERTS, 1), wsgu, ws_down[0].astype(BF16), seqs)

    ch = EXPERT_CHUNK
    n_rows = (-(-n * TOP_K // ch) + N_EXPERTS) * ch
    counts = cnt[:, 0].astype(I32)
    padded = (counts + ch - 1) // ch * ch
    pend = jnp.cumsum(padded)
    pstart = pend - padded

    didx = _dest(idx_t, rank_t, pstart.astype(F32).reshape(N_EXPERTS, 1), n_rows)
    didx = didx.reshape(TOP_K, N_PLANES * n)

    xs_rows = _sc_scatter_rows(h2p.reshape(N_PLANES * n, LANES), didx, N_PLANES * n_rows)
    ys_rows = _experts(xs_rows.reshape(N_PLANES, n_rows, LANES), w_gate[0], w_up[0], w_down[0],
                       pstart // ch, padded // ch, counts, pend[-1:] // ch)
    yk = _sc_gather_rows(ys_rows.reshape(N_PLANES * n_rows, LANES), didx)
    yk = yk.reshape(TOP_K, N_PLANES, n, LANES)

    gf = g_final.reshape(1, d)
    y_p = _final(xa, yk, wts, mod3, gf, 0, n_p, sp, 0)
    y_s = _final(xa, yk, wts, mod3, gf, n_p // T_FIN, n_s, ss, bp)
    return y_p.reshape(bp, sp, d), y_s.reshape(bs, ss, d)
```

```python
import functools

import numpy as np
import jax
import jax.numpy as jnp
from jax import lax
from jax.experimental import pallas as pl
from jax.experimental.pallas import tpu as pltpu
from jax.experimental.pallas import tpu_sc as plsc

F32 = jnp.float32
BF16 = jnp.bfloat16
I32 = jnp.int32

D_MODEL = 1024
POOL_WINDOWS = (2, 4, 8, 16)
POOL_WIDTH = 512
POOL_GROUP = 128
N_HEADS = 8
N_KV_HEADS = 2
HEAD_DIM = 64
Q_PER_KV = N_HEADS // N_KV_HEADS
ATTN_WIDTH = N_HEADS * HEAD_DIM
KV_WIDTH = N_KV_HEADS * HEAD_DIM
D_IN_PROJ = POOL_WIDTH + ATTN_WIDTH + 2 * KV_WIDTH
WINDOW = 128
N_EXPERTS = 256
TOP_K = 8
N_EXPERT_GROUPS = 8
GROUP_SIZE = N_EXPERTS // N_EXPERT_GROUPS
TOPK_GROUPS = 4
D_EXPERT = 256
D_SHARED = 256
ROUTED_SCALE = 2.5
EPS = 1e-6
NEG_INF = -1e30

LANES = 128
HALO = 8
N_PLANES = 4
HALF = D_MODEL // 2

T_IN = 512
T_MIX = 256
ATT_BLOCK = 128
T_DEST = 512
EXPERT_CHUNK = 512
EXPERT_NBUF = 4
T_FIN = 256
SC_ROWS = 128
SC_WORKERS = 32
SC_NBUF = 4
SC_LAG = 2
ADA_ROWS = 16
VMEM_LIMIT = 48 * 1024 * 1024

LOG2E = 1.4426950408889634
ALIBI_SLOPES = tuple(float(2.0 ** (-8.0 * (h + 1) / N_HEADS)) for h in range(N_HEADS))


def _pack_planes(y):
    planes = []
    for j in range(N_PLANES):
        hi = y[:, j * LANES:(j + 1) * LANES].astype(BF16).astype(F32)
        lo = y[:, HALF + j * LANES:HALF + (j + 1) * LANES].astype(BF16).astype(F32)
        hb = lax.bitcast_convert_type(hi, jnp.uint32) & jnp.uint32(0xFFFF0000)
        lb = lax.bitcast_convert_type(lo, jnp.uint32) >> jnp.uint32(16)
        planes.append(lax.bitcast_convert_type(hb | lb, I32))
    return planes


def _unpack_plane(w):
    u = lax.bitcast_convert_type(w, jnp.uint32)
    hi = lax.bitcast_convert_type(u & jnp.uint32(0xFFFF0000), F32)
    lo = lax.bitcast_convert_type(u << jnp.uint32(16), F32)
    return hi, lo


def _ada_kernel(c_ref, w_ref, b_ref, o_ref):
    c = c_ref[...]
    s = c * jax.nn.sigmoid(c)
    o_ref[...] = jnp.dot(s, w_ref[...], precision=lax.Precision.HIGHEST,
                         preferred_element_type=F32) + b_ref[...]


def _ada(c_all, w_ada, b_ada):
    n_out = w_ada.shape[1]
    tn = 1024
    return pl.pallas_call(
        _ada_kernel,
        grid=(n_out // tn,),
        in_specs=[pl.BlockSpec((ADA_ROWS, D_MODEL), lambda j: (0, 0)),
                  pl.BlockSpec((D_MODEL, tn), lambda j: (0, j)),
                  pl.BlockSpec((1, tn), lambda j: (0, j))],
        out_specs=pl.BlockSpec((ADA_ROWS, tn), lambda j: (0, j)),
        out_shape=jax.ShapeDtypeStruct((ADA_ROWS, n_out), F32),
        name="ada",
    )(c_all, w_ada, b_ada.reshape(1, n_out))


def _rms_mod(x, g, scale, shift):
    ms = jnp.mean(x * x, axis=-1, keepdims=True)
    return (x * lax.rsqrt(ms + EPS) * g) * (1.0 + scale) + shift


def _inproj_kernel(meta_ref, xp_ref, xs_ref, mod_ref, g_ref, w_ref, u_ref, q_ref, k_ref, v_ref):
    x = jnp.where(meta_ref[7, pl.program_id(0)] == 0, xp_ref[...], xs_ref[...])
    mod = mod_ref[0]
    h = _rms_mod(x, g_ref[...], mod[:, D_MODEL:2 * D_MODEL], mod[:, 0:D_MODEL])
    z = jnp.dot(h.astype(BF16), w_ref[...], preferred_element_type=F32)
    u_ref[...] = z[:, :POOL_WIDTH]
    q_ref[...] = (z[:, POOL_WIDTH:POOL_WIDTH + ATTN_WIDTH] * (LOG2E * HEAD_DIM ** -0.5)).astype(BF16)
    k_ref[...] = z[:, POOL_WIDTH + ATTN_WIDTH:POOL_WIDTH + ATTN_WIDTH + KV_WIDTH].astype(BF16)
    v_ref[...] = z[:, POOL_WIDTH + ATTN_WIDTH + KV_WIDTH:].astype(BF16)


def _tile_meta(segments, tile):
    rows = []
    for src, tok0, n_seqs, seq_len, sid0 in segments:
        per = seq_len // tile
        for q in range(n_seqs):
            for t in range(per):
                blk = (tok0 + q * seq_len) // tile + t
                rows.append((sid0 + q, int(t == 0), int(t == per - 1), t * tile, seq_len,
                             blk if src == 0 else -1, blk if src == 1 else -1, src))
    meta = np.asarray(rows, np.int32)
    for col in (5, 6):
        known = meta[:, col] >= 0
        if not known.any():
            meta[:, col] = 0
            continue
        last = np.maximum.accumulate(np.where(known, np.arange(len(meta)), -1))
        last = np.where(last < 0, np.argmax(known), last)
        meta[:, col] = meta[last, col]
    return meta.T.copy()


def _group_tokens(segments):
    return sum(n_seqs * seq_len for _, _, n_seqs, seq_len, _ in segments)


def _inproj(xp, xs, mod3, g1, w_in_bf, segments):
    n = _group_tokens(segments)
    meta = jnp.asarray(_tile_meta(segments, T_IN))
    tok = lambda w: pl.BlockSpec((T_IN, w), lambda i, m: (i, 0))
    return pl.pallas_call(
        _inproj_kernel,
        grid_spec=pltpu.PrefetchScalarGridSpec(
            num_scalar_prefetch=1,
            grid=(n // T_IN,),
            in_specs=[
                pl.BlockSpec((T_IN, D_MODEL), lambda i, m: (m[5, i], 0)),
                pl.BlockSpec((T_IN, D_MODEL), lambda i, m: (m[6, i], 0)),
                pl.BlockSpec((1, 1, 6 * D_MODEL), lambda i, m: (m[0, i], 0, 0)),
                pl.BlockSpec((1, D_MODEL), lambda i, m: (0, 0)),
                pl.BlockSpec((D_MODEL, D_IN_PROJ), lambda i, m: (0, 0)),
            ],
            out_specs=[tok(POOL_WIDTH), tok(ATTN_WIDTH), tok(KV_WIDTH), tok(KV_WIDTH)],
        ),
        out_shape=[jax.ShapeDtypeStruct((n, POOL_WIDTH), F32),
                   jax.ShapeDtypeStruct((n, ATTN_WIDTH), BF16),
                   jax.ShapeDtypeStruct((n, KV_WIDTH), BF16),
                   jax.ShapeDtypeStruct((n, KV_WIDTH), BF16)],
        compiler_params=pltpu.CompilerParams(dimension_semantics=("arbitrary",),
                                             vmem_limit_bytes=VMEM_LIMIT),
        name="inproj",
    )(meta, xp, xs, mod3, g1, w_in_bf)


def _pool_mixer(uext_ref, uc, pos, seq_len, wpool_ref, bpool_ref, pscale_ref):
    outs = []
    for gi, w in enumerate(POOL_WINDOWS):
        c0 = gi * POOL_GROUP
        half = w // 2
        acc = uext_ref[pl.ds(HALO - half, T_MIX), pl.ds(c0, POOL_GROUP)]
        for o in range(-half + 1, half):
            acc = acc + uext_ref[pl.ds(HALO + o, T_MIX), pl.ds(c0, POOL_GROUP)]
        lo = jnp.maximum(pos - half, 0)
        hi = jnp.minimum(pos + half, seq_len)
        cnt = (hi - lo).astype(F32)
        d = acc / cnt - uc[:, c0:c0 + POOL_GROUP]
        y = jnp.dot(d.astype(BF16), wpool_ref[gi], preferred_element_type=F32)
        y = (y + bpool_ref[:, c0:c0 + POOL_GROUP]) * pscale_ref[:, c0:c0 + POOL_GROUP]
        outs.append(y)
    return jnp.concatenate(outs, axis=1)


def _attention_bias():
    t = ATT_BLOCK
    r = np.arange(t)[:, None]
    c = np.arange(3 * t)[None, :]
    dist = np.abs(r - (c - t))
    band = dist <= WINDOW
    out = np.empty((3, N_HEADS, t, 3 * t), np.float32)
    for var, valid in enumerate((band, band & (c >= t), band & (c < 2 * t))):
        for h in range(N_HEADS):
            out[var, h] = np.where(valid, -ALIBI_SLOPES[h] * LOG2E * dist, NEG_INF)
    return out.reshape(3, N_HEADS * t, 3 * t)


def _banded_attention(q, kw, vw, variant, bias_ref, sink_ref):
    t = ATT_BLOCK
    low = lax.broadcasted_iota(I32, (t, LANES), 1) < HEAD_DIM
    zero = jnp.zeros((t, LANES), BF16)
    qs = []
    for h in range(N_HEADS):
        blk = q[:, (h % Q_PER_KV) * LANES:(h % Q_PER_KV + 1) * LANES]
        qs.append(jnp.where(low, blk, zero) if h < Q_PER_KV else jnp.where(low, zero, blk))
    s = lax.dot_general(jnp.concatenate(qs, axis=0), kw, (((1,), (1,)), ((), ())),
                        preferred_element_type=F32)
    ps, dens = [], []
    for h in range(N_HEADS):
        sh = s[h * t:(h + 1) * t] + bias_ref[variant, pl.ds(h * t, t), :]
        sink = sink_ref[:, h:h + 1] * LOG2E
        m = jnp.maximum(jnp.max(sh, axis=-1, keepdims=True), sink)
        p = jnp.exp2(sh - m)
        dens.append(jnp.sum(p, axis=-1, keepdims=True) + jnp.exp2(sink - m))
        ps.append(p.astype(BF16))
    o = jnp.dot(jnp.concatenate(ps, axis=0), vw, preferred_element_type=F32)
    oh = [o[h * t:(h + 1) * t] / dens[h] for h in range(N_HEADS)]
    return jnp.concatenate([jnp.where(low, oh[cb], oh[Q_PER_KV + cb]) for cb in range(Q_PER_KV)], axis=1)


def _route(biased, scores):
    t = biased.shape[1]
    rowf = lax.broadcasted_iota(I32, (N_EXPERTS, t), 0).astype(F32)
    ninf = float("-inf")
    gs = []
    for g in range(N_EXPERT_GROUPS):
        blk = biased[g * GROUP_SIZE:(g + 1) * GROUP_SIZE, :]
        rf = rowf[g * GROUP_SIZE:(g + 1) * GROUP_SIZE, :]
        m1 = jnp.max(blk, axis=0, keepdims=True)
        i1 = jnp.min(jnp.where(blk == m1, rf, float(N_EXPERTS)), axis=0, keepdims=True)
        m2 = jnp.max(jnp.where(rf == i1, ninf, blk), axis=0, keepdims=True)
        gs.append(m1 + m2)
    keep = []
    for g in range(N_EXPERT_GROUPS):
        beat = jnp.zeros((1, t), F32)
        for g2 in range(N_EXPERT_GROUPS):
            if g2 == g:
                continue
            better = (gs[g2] >= gs[g]) if g2 < g else (gs[g2] > gs[g])
            beat = beat + better.astype(F32)
        keep.append(jnp.broadcast_to(beat < float(TOPK_GROUPS), (GROUP_SIZE, t)))
    emask = jnp.concatenate(keep, axis=0)
    masked = jnp.where(emask, biased, NEG_INF)
    idx_rows, w_rows = [], []
    sel_any = jnp.zeros((N_EXPERTS, t), F32)
    for _ in range(TOP_K):
        m = jnp.max(masked, axis=0, keepdims=True)
        ik = jnp.min(jnp.where(masked == m, rowf, float(N_EXPERTS)), axis=0, keepdims=True)
        sel = rowf == ik
        w_rows.append(jnp.sum(jnp.where(sel, scores, 0.0), axis=0, keepdims=True))
        idx_rows.append(ik)
        masked = jnp.where(sel, ninf, masked)
        sel_any = sel_any + sel.astype(F32)
    return idx_rows, w_rows, sel_any, rowf


def _mixer_kernel(meta_ref, xp_ref, xs_ref, mod_ref, up_ref, uc_ref, un_ref, q_ref,
                  kp_ref, kc_ref, kn_ref, vp_ref, vc_ref, vn_ref,
                  wpool_ref, bpool_ref, pscale_ref, sink_ref, wout_ref, g2_ref,
                  wr3_ref, rbias_ref, wsgu_ref, wsd_ref, tri_ref, abias_ref,
                  xa_ref, h2p_ref, idx_ref, rank_ref, wts_ref, cnt_ref,
                  uext_ref, kext_ref, vext_ref, base_ref):
    i = pl.program_id(0)
    first = meta_ref[1, i] == 1
    last = meta_ref[2, i] == 1
    pos0 = meta_ref[3, i]
    seq_len = meta_ref[4, i]
    mod = mod_ref[0]
    gt1 = mod[:, 2 * D_MODEL:3 * D_MODEL]
    sh2 = mod[:, 3 * D_MODEL:4 * D_MODEL]
    sc2 = mod[:, 4 * D_MODEL:5 * D_MODEL]
    gt2 = mod[:, 5 * D_MODEL:6 * D_MODEL]

    @pl.when(i == 0)
    def _():
        base_ref[...] = jnp.zeros_like(base_ref)

    uc = uc_ref[...]
    uext_ref[pl.ds(0, HALO), :] = jnp.where(first, 0.0, up_ref[...])
    uext_ref[pl.ds(HALO, T_MIX), :] = uc
    uext_ref[pl.ds(HALO + T_MIX, HALO), :] = jnp.where(last, 0.0, un_ref[...])
    pos = pos0 + lax.broadcasted_iota(I32, (T_MIX, 1), 0)
    a_pool = _pool_mixer(uext_ref, uc, pos, seq_len, wpool_ref, bpool_ref, pscale_ref)

    t = ATT_BLOCK
    kext_ref[pl.ds(0, t), :] = kp_ref[...]
    kext_ref[pl.ds(t, T_MIX), :] = kc_ref[...]
    kext_ref[pl.ds(t + T_MIX, t), :] = kn_ref[...]
    vext_ref[pl.ds(0, t), :] = vp_ref[...]
    vext_ref[pl.ds(t, T_MIX), :] = vc_ref[...]
    vext_ref[pl.ds(t + T_MIX, t), :] = vn_ref[...]
    n_sub = T_MIX // t
    assert n_sub >= 2, "a query block is never both first and last in its sequence"
    attn = []
    for sub in range(n_sub):
        variant = 0
        if sub == 0:
            variant = jnp.where(first, 1, variant)
        if sub == n_sub - 1:
            variant = jnp.where(last, 2, variant)
        attn.append(_banded_attention(
            q_ref[pl.ds(sub * t, t), :], kext_ref[pl.ds(sub * t, 3 * t), :], vext_ref[pl.ds(sub * t, 3 * t), :],
            variant, abias_ref, sink_ref))
    a_attn = jnp.concatenate(attn, axis=0)

    a = jnp.concatenate([a_pool, a_attn], axis=1).astype(BF16)
    mix = jnp.dot(a, wout_ref[...], preferred_element_type=F32)

    x = jnp.where(meta_ref[7, i] == 0, xp_ref[...], xs_ref[...])
    x1 = x + gt1 * mix
    h2 = _rms_mod(x1, g2_ref[...], sc2, sh2)
    h2b = h2.astype(BF16)
    planes = _pack_planes(h2)
    for j in range(N_PLANES):
        h2p_ref[j] = planes[j]
    gu = jnp.dot(h2b, wsgu_ref[...], preferred_element_type=F32)
    gate, up = gu[:, :D_SHARED], gu[:, D_SHARED:]
    act = (gate * jax.nn.sigmoid(gate)) * up
    shared = jnp.dot(act.astype(BF16), wsd_ref[...], preferred_element_type=F32)
    xa_ref[...] = x1 + gt2 * shared
    h2lo = (h2 - h2b.astype(F32)).astype(BF16)
    logits = lax.dot_general(wr3_ref[...], jnp.concatenate([h2b, h2b, h2lo], axis=1),
                             (((1,), (1,)), ((), ())), preferred_element_type=F32)
    scores = jax.nn.sigmoid(logits)
    biased = scores + rbias_ref[...]
    routed = [_route(biased[:, c * LANES:(c + 1) * LANES], scores[:, c * LANES:(c + 1) * LANES])
              for c in range(T_MIX // LANES)]
    for c, (idx_rows, w_rows, _, _) in enumerate(routed):
        wsum = w_rows[0]
        for wr in w_rows[1:]:
            wsum = wsum + wr
        wts_t = jnp.concatenate([wr / wsum * ROUTED_SCALE for wr in w_rows], axis=0)
        wpad = jnp.concatenate([wts_t, jnp.zeros((LANES - TOP_K, LANES), F32)], axis=0)
        wts_ref[pl.ds(c * LANES, LANES), :] = wpad.T
        idx_ref[:, pl.ds(c * LANES, LANES)] = jnp.concatenate(idx_rows, axis=0).astype(I32)
    sel_any = jnp.concatenate([r[2] for r in routed], axis=1)
    pref = jnp.dot(sel_any.astype(BF16), tri_ref[...], preferred_element_type=F32)
    before = base_ref[...] + pref[:, :T_MIX]
    for c, (idx_rows, _, _, rowf) in enumerate(routed):
        bc = before[:, c * LANES:(c + 1) * LANES]
        ranks = [jnp.sum(jnp.where(rowf == ik, bc, 0.0), axis=0, keepdims=True) for ik in idx_rows]
        rank_ref[:, pl.ds(c * LANES, LANES)] = jnp.concatenate(ranks, axis=0).astype(I32)
    new_base = base_ref[...] + pref[:, T_MIX:]
    base_ref[...] = new_base
    cnt_ref[...] = new_base


def _mixer(xp, xs, mod3, u, q, k, v, wpool_bf, bpool, pscale, sink, wout_bf, g2, wr3, rbias,
           wsgu_bf, wsd_bf, segments):
    n = _group_tokens(segments)
    t = T_MIX
    nt = n // t
    ab = ATT_BLOCK
    meta = jnp.asarray(_tile_meta(segments, t))
    tri = np.concatenate([np.triu(np.ones((t, t), np.float32), 1), np.ones((t, t), np.float32)], axis=1)
    tri = jnp.asarray(tri, BF16)
    abias = jnp.asarray(_attention_bias())
    cur = lambda w: pl.BlockSpec((t, w), lambda i, m: (i, 0))
    prev = lambda w: pl.BlockSpec((ab, w), lambda i, m: (jnp.maximum(i * (t // ab) - 1, 0), 0))
    nxt = lambda w: pl.BlockSpec((ab, w), lambda i, m: (jnp.minimum((i + 1) * (t // ab), n // ab - 1), 0))
    full = lambda a: pl.BlockSpec(a.shape, lambda i, m: (0,) * a.ndim)
    hb = t // HALO
    in_specs = [
        pl.BlockSpec((t, D_MODEL), lambda i, m: (m[5, i], 0)),
        pl.BlockSpec((t, D_MODEL), lambda i, m: (m[6, i], 0)),
        pl.BlockSpec((1, 1, 6 * D_MODEL), lambda i, m: (m[0, i], 0, 0)),
        pl.BlockSpec((HALO, POOL_WIDTH), lambda i, m: (jnp.maximum(i * hb - 1, 0), 0)),
        cur(POOL_WIDTH),
        pl.BlockSpec((HALO, POOL_WIDTH), lambda i, m: (jnp.minimum((i + 1) * hb, n // HALO - 1), 0)),
        cur(ATTN_WIDTH),
        prev(KV_WIDTH), cur(KV_WIDTH), nxt(KV_WIDTH),
        prev(KV_WIDTH), cur(KV_WIDTH), nxt(KV_WIDTH),
        full(wpool_bf), full(bpool), full(pscale), full(sink), full(wout_bf), full(g2),
        full(wr3), full(rbias), full(wsgu_bf), full(wsd_bf), full(tri), full(abias),
    ]
    out_specs = [
        cur(D_MODEL),
        pl.BlockSpec((N_PLANES, t, LANES), lambda i, m: (0, i, 0)),
        pl.BlockSpec((TOP_K, t), lambda i, m: (0, i)),
        pl.BlockSpec((TOP_K, t), lambda i, m: (0, i)),
        cur(LANES),
        pl.BlockSpec((N_EXPERTS, t), lambda i, m: (0, 0)),
    ]
    out_shape = [
        jax.ShapeDtypeStruct((n, D_MODEL), F32),
        jax.ShapeDtypeStruct((N_PLANES, n, LANES), I32),
        jax.ShapeDtypeStruct((TOP_K, n), I32),
        jax.ShapeDtypeStruct((TOP_K, n), I32),
        jax.ShapeDtypeStruct((n, LANES), F32),
        jax.ShapeDtypeStruct((N_EXPERTS, t), F32),
    ]
    return pl.pallas_call(
        _mixer_kernel,
        grid_spec=pltpu.PrefetchScalarGridSpec(
            num_scalar_prefetch=1, grid=(nt,), in_specs=in_specs, out_specs=out_specs,
            scratch_shapes=[pltpu.VMEM((t + 2 * HALO, POOL_WIDTH), F32),
                            pltpu.VMEM((t + 2 * ab, KV_WIDTH), BF16),
                            pltpu.VMEM((t + 2 * ab, KV_WIDTH), BF16),
                            pltpu.VMEM((N_EXPERTS, t), F32)]),
        out_shape=out_shape,
        compiler_params=pltpu.CompilerParams(dimension_semantics=("arbitrary",),
                                             vmem_limit_bytes=VMEM_LIMIT),
        name="mixer",
    )(meta, xp, xs, mod3, u, u, u, q, k, k, k, v, v, v, wpool_bf, bpool, pscale, sink, wout_bf, g2,
      wr3, rbias, wsgu_bf, wsd_bf, tri, abias)


def _dest_kernel(idx_ref, rank_ref, pstart_ref, o_ref, *, n_rows):
    rowi = lax.broadcasted_iota(I32, (N_EXPERTS, T_DEST), 0)
    pstart = pstart_ref[...]
    rows = []
    for kk in range(TOP_K):
        sel = rowi == idx_ref[kk:kk + 1, :]
        start = jnp.sum(jnp.where(sel, pstart, 0.0), axis=0, keepdims=True)
        rows.append(start.astype(I32) + rank_ref[kk:kk + 1, :])
    dest = jnp.concatenate(rows, axis=0)
    for j in range(N_PLANES):
        o_ref[:, j, :] = dest + j * n_rows


def _dest(idx_t, rank_t, pstart_col, n_rows):
    n = idx_t.shape[1]
    return pl.pallas_call(
        functools.partial(_dest_kernel, n_rows=n_rows),
        grid=(n // T_DEST,),
        in_specs=[pl.BlockSpec((TOP_K, T_DEST), lambda i: (0, i)),
                  pl.BlockSpec((TOP_K, T_DEST), lambda i: (0, i)),
                  pl.BlockSpec((N_EXPERTS, 1), lambda i: (0, 0))],
        out_specs=pl.BlockSpec((TOP_K, N_PLANES, T_DEST), lambda i: (0, 0, i)),
        out_shape=jax.ShapeDtypeStruct((TOP_K, N_PLANES, n), I32),
        name="dest",
    )(idx_t, rank_t, pstart_col)


def _sc_mesh():
    return plsc.VectorSubcoreMesh(core_axis_name="c", subcore_axis_name="s")


def _sc_worker():
    return lax.axis_index("s") * 2 + lax.axis_index("c")


def _sc_scatter_rows(table, didx, n_out_rows):
    m = table.shape[0]
    per_worker = m // SC_ROWS // SC_WORKERS
    assert per_worker % 2 == 0

    @functools.partial(
        pl.kernel, mesh=_sc_mesh(),
        out_type=jax.ShapeDtypeStruct((n_out_rows, LANES), I32),
        scratch_types=[pltpu.VMEM((2, SC_ROWS, LANES), I32), pltpu.VMEM((2, TOP_K, SC_ROWS), I32),
                       pltpu.SemaphoreType.DMA((2,))])
    def run(table_hbm, didx_hbm, out_hbm, rows, idx, ssem):
        wid = _sc_worker()

        def scatter(p, kk):
            return pltpu.make_async_copy(rows.at[p], out_hbm.at[idx.at[p].at[kk]], ssem.at[p])

        @pl.loop(0, per_worker, step=2)
        def _(s0):
            for p in range(2):
                @pl.when(s0 > 0)
                def _():
                    for kk in range(TOP_K):
                        scatter(p, kk).wait()

                off = pl.multiple_of((wid * per_worker + s0 + p) * SC_ROWS, SC_ROWS)
                pltpu.sync_copy(table_hbm.at[pl.ds(off, SC_ROWS)], rows.at[p])
                pltpu.sync_copy(didx_hbm.at[:, pl.ds(off, SC_ROWS)], idx.at[p])
                for kk in range(TOP_K):
                    scatter(p, kk).start()

        for p in range(2):
            for kk in range(TOP_K):
                scatter(p, kk).wait()

    return run(table, didx)


def _sc_gather_rows(table, didx):
    m = didx.shape[1]
    per_worker = m // SC_ROWS // SC_WORKERS
    assert per_worker % 2 == 0 and (2 * TOP_K) % SC_NBUF == 0
    items = 2 * TOP_K

    @functools.partial(
        pl.kernel, mesh=_sc_mesh(),
        out_type=jax.ShapeDtypeStruct((TOP_K, m, LANES), I32),
        scratch_types=[pltpu.VMEM((SC_NBUF, SC_ROWS, LANES), I32), pltpu.VMEM((2, TOP_K, SC_ROWS), I32),
                       pltpu.SemaphoreType.DMA((SC_NBUF,)), pltpu.SemaphoreType.DMA((SC_NBUF,)),
                       pltpu.SemaphoreType.DMA((2,))])
    def run(table_hbm, didx_hbm, out_hbm, rows, idx, gsem, wsem, isem):
        wid = _sc_worker()

        def off_of(step):
            return pl.multiple_of((wid * per_worker + step) * SC_ROWS, SC_ROWS)

        def idx_copy(step, p):
            return pltpu.make_async_copy(didx_hbm.at[:, pl.ds(off_of(step), SC_ROWS)], idx.at[p], isem.at[p])

        def gather(j):
            r = j % SC_NBUF
            return pltpu.make_async_copy(table_hbm.at[idx.at[j // TOP_K].at[j % TOP_K]], rows.at[r], gsem.at[r])

        def write(s0, j):
            r = j % SC_NBUF
            return pltpu.make_async_copy(rows.at[r], out_hbm.at[j % TOP_K, pl.ds(off_of(s0 + j // TOP_K), SC_ROWS)],
                                         wsem.at[r])

        def retire_write(s0, j):
            if j >= 0:
                write(s0, j).wait()
            else:
                @pl.when(s0 > 0)
                def _():
                    write(s0 - 2, j + items).wait()

        def finish_read(s0, j):
            if j >= 0:
                gather(j).wait()
                write(s0, j).start()
            else:
                @pl.when(s0 > 0)
                def _():
                    gather(j + items).wait()
                    write(s0 - 2, j + items).start()

        idx_copy(0, 0).start()

        @pl.loop(0, per_worker, step=2)
        def _(s0):
            for j in range(items):
                p, kk = j // TOP_K, j % TOP_K
                if kk == 0:
                    idx_copy(s0 + p, p).wait()
                retire_write(s0, j - SC_NBUF)
                gather(j).start()
                finish_read(s0, j - SC_LAG)
                if kk == SC_LAG:
                    @pl.when(s0 + p + 1 < per_worker)
                    def _():
                        idx_copy(s0 + p + 1, 1 - p).start()

        last = per_worker - 2
        for j in range(items - SC_LAG, items):
            gather(j).wait()
            write(last, j).start()
        for j in range(items - SC_NBUF, items):
            write(last, j).wait()

    return run(table, didx)


def _experts_kernel(cstart_ref, nchunk_ref, count_ref, total_ref, x_hbm, wg_ref, wu_ref, wd_ref, y_hbm,
                    wgu_s, wd_s, xbuf, ybuf, xsem, ysem):
    e = pl.program_id(0)
    nb = xbuf.shape[0]
    ch = xbuf.shape[2]
    total = total_ref[0]

    def x_copy(c):
        slot = lax.rem(c, nb)
        rows = pl.ds(pl.multiple_of(c * ch, ch), ch)
        return pltpu.make_async_copy(x_hbm.at[:, rows, :], xbuf.at[slot], xsem.at[slot])

    def y_copy(c):
        slot = lax.rem(c, nb)
        rows = pl.ds(pl.multiple_of(c * ch, ch), ch)
        return pltpu.make_async_copy(ybuf.at[slot], y_hbm.at[:, rows, :], ysem.at[slot])

    @pl.when(e == 0)
    def _():
        for c in range(nb - 1):
            @pl.when(c < total)
            def _():
                x_copy(c).start()

    for j in range(N_PLANES):
        hi = pl.ds(j * LANES, LANES)
        lo = pl.ds(HALF + j * LANES, LANES)
        r_hi = pl.ds(2 * j * LANES, LANES)
        r_lo = pl.ds((2 * j + 1) * LANES, LANES)
        wgu_s[r_hi, pl.ds(0, D_EXPERT)] = wg_ref[0, hi, :].astype(BF16)
        wgu_s[r_lo, pl.ds(0, D_EXPERT)] = wg_ref[0, lo, :].astype(BF16)
        wgu_s[r_hi, pl.ds(D_EXPERT, D_EXPERT)] = wu_ref[0, hi, :].astype(BF16)
        wgu_s[r_lo, pl.ds(D_EXPERT, D_EXPERT)] = wu_ref[0, lo, :].astype(BF16)
    wd_s[...] = wd_ref[0].astype(BF16)

    c0 = cstart_ref[e]
    count = count_ref[e]

    def chunk(i, carry):
        c = c0 + i
        slot = lax.rem(c, nb)

        @pl.when(c + nb - 1 < total)
        def _():
            x_copy(c + nb - 1).start()

        x_copy(c).wait()

        @pl.when(c >= nb)
        def _():
            y_copy(c - nb).wait()

        live = lax.broadcasted_iota(I32, (ch, LANES), 0) < count - i * ch
        xs = []
        for j in range(N_PLANES):
            hi, lo = _unpack_plane(jnp.where(live, xbuf[slot, j], 0))
            xs += [hi.astype(BF16), lo.astype(BF16)]
        acc = jnp.dot(jnp.concatenate(xs, axis=1), wgu_s[...], preferred_element_type=F32)
        gate, up = acc[:, :D_EXPERT], acc[:, D_EXPERT:]
        act = (gate * jax.nn.sigmoid(gate)) * up
        y = jnp.dot(act.astype(BF16), wd_s[...], preferred_element_type=F32)
        planes = _pack_planes(y)
        for j in range(N_PLANES):
            ybuf[slot, j] = planes[j]
        y_copy(c).start()
        return carry

    lax.fori_loop(0, nchunk_ref[e], chunk, 0)

    @pl.when(e == pl.num_programs(0) - 1)
    def _():
        for back in range(nb, 0, -1):
            @pl.when(total >= back)
            def _():
                y_copy(total - back).wait()


def _experts(xs_planes, w_gate, w_up, w_down, cstart, nchunk, counts, total):
    n_rows = xs_planes.shape[1]
    ch = EXPERT_CHUNK
    w_map = lambda e, *_: (e, 0, 0)
    any_space = pl.BlockSpec(memory_space=pl.ANY)
    return pl.pallas_call(
        _experts_kernel,
        grid_spec=pltpu.PrefetchScalarGridSpec(
            num_scalar_prefetch=4, grid=(N_EXPERTS,),
            in_specs=[any_space,
                      pl.BlockSpec((1, D_MODEL, D_EXPERT), w_map),
                      pl.BlockSpec((1, D_MODEL, D_EXPERT), w_map),
                      pl.BlockSpec((1, D_EXPERT, D_MODEL), w_map)],
            out_specs=any_space,
            scratch_shapes=[pltpu.VMEM((D_MODEL, 2 * D_EXPERT), BF16),
                            pltpu.VMEM((D_EXPERT, D_MODEL), BF16),
                            pltpu.VMEM((EXPERT_NBUF, N_PLANES, ch, LANES), I32),
                            pltpu.VMEM((EXPERT_NBUF, N_PLANES, ch, LANES), I32),
                            pltpu.SemaphoreType.DMA((EXPERT_NBUF,)),
                            pltpu.SemaphoreType.DMA((EXPERT_NBUF,))]),
        out_shape=jax.ShapeDtypeStruct((N_PLANES, n_rows, LANES), I32),
        compiler_params=pltpu.CompilerParams(dimension_semantics=("arbitrary",),
                                             vmem_limit_bytes=VMEM_LIMIT),
        name="experts",
    )(cstart, nchunk, counts, total, xs_planes, w_gate, w_up, w_down)


def _final_kernel(xa_ref, yk_ref, wts_ref, mod_ref, g_ref, o_ref):
    wts = wts_ref[...]
    his = [jnp.zeros((T_FIN, LANES), F32) for _ in range(N_PLANES)]
    los = [jnp.zeros((T_FIN, LANES), F32) for _ in range(N_PLANES)]
    for kk in range(TOP_K):
        wk = wts[:, kk:kk + 1]
        for j in range(N_PLANES):
            hi, lo = _unpack_plane(yk_ref[kk, j])
            his[j] = his[j] + wk * hi
            los[j] = los[j] + wk * lo
    routed = jnp.concatenate(his + los, axis=1)
    gt2 = mod_ref[0][:, 5 * D_MODEL:6 * D_MODEL]
    x = xa_ref[...] + gt2 * routed
    ms = jnp.mean(x * x, axis=-1, keepdims=True)
    o_ref[...] = x * lax.rsqrt(ms + EPS) * g_ref[...]


def _final_kernel_into(xa_ref, yk_ref, wts_ref, mod_ref, g_ref, prev_ref, o_ref):
    _final_kernel(xa_ref, yk_ref, wts_ref, mod_ref, g_ref, o_ref)


def _final(xa, yk, wts, mod3, g_final, tile_off, n_tok, seq_len, sid0, out_rows, out_tile_off, prev):
    tiles_per_seq = seq_len // T_FIN
    in_specs = [pl.BlockSpec((T_FIN, D_MODEL), lambda i: (i + tile_off, 0)),
                pl.BlockSpec((TOP_K, N_PLANES, T_FIN, LANES), lambda i: (0, 0, i + tile_off, 0)),
                pl.BlockSpec((T_FIN, LANES), lambda i: (i + tile_off, 0)),
                pl.BlockSpec((1, 1, 6 * D_MODEL), lambda i: (sid0 + i // tiles_per_seq, 0, 0)),
                pl.BlockSpec((1, D_MODEL), lambda i: (0, 0))]
    args = [xa, yk, wts, mod3, g_final]
    if prev is not None:
        in_specs.append(pl.BlockSpec(memory_space=pl.ANY))
        args.append(prev)
    return pl.pallas_call(
        _final_kernel if prev is None else _final_kernel_into,
        grid=(n_tok // T_FIN,),
        in_specs=in_specs,
        out_specs=pl.BlockSpec((T_FIN, D_MODEL), lambda i: (i + out_tile_off, 0)),
        out_shape=jax.ShapeDtypeStruct((out_rows, D_MODEL), F32),
        input_output_aliases={} if prev is None else {len(args) - 1: 0},
        compiler_params=pltpu.CompilerParams(dimension_semantics=("arbitrary",),
                                             vmem_limit_bytes=VMEM_LIMIT),
        name="final",
    )(*args)


def _token_groups(bp, sp, bs, ss):
    seqs = [(0, q * sp, sp, q) for q in range(bp)] + [(1, q * ss, ss, bp + q) for q in range(bs)]
    half = (bp * sp + bs * ss) / 2
    groups, cur, acc = [], [], 0
    for src, tok0, seq_len, sid in seqs:
        if cur and cur[-1][0] == src and cur[-1][3] == seq_len and cur[-1][1] + cur[-1][2] * seq_len == tok0:
            cur[-1] = (src, cur[-1][1], cur[-1][2] + 1, seq_len, cur[-1][4])
        else:
            cur.append((src, tok0, 1, seq_len, sid))
        acc += seq_len
        if not groups and acc >= half:
            groups.append(cur)
            cur = []
    if cur:
        groups.append(cur)
    return groups


def kernel(x_prompt, x_sample, c_prompt, c_sample, w_ada, b_ada, g_norm1, w_in, w_pool, b_pool, pool_scale, attn_sink, w_out, g_norm2, w_router, router_bias, w_gate, w_up, w_down, ws_gate, ws_up, ws_down, g_final):
    assert w_ada.shape[0] == 1, "one layer"
    bp, sp, d = x_prompt.shape
    bs, ss, _ = x_sample.shape
    assert d == D_MODEL and bp + bs <= ADA_ROWS
    n_p, n_s = bp * sp, bs * ss
    xp = x_prompt.reshape(n_p, d)
    xs = x_sample.reshape(n_s, d)

    c_all = jnp.concatenate([c_prompt, c_sample, jnp.zeros((ADA_ROWS - bp - bs, d), F32)], axis=0)
    mod3 = _ada(c_all, w_ada[0], b_ada[0]).reshape(ADA_ROWS, 1, 6 * d)

    q0, q1 = POOL_WIDTH, POOL_WIDTH + ATTN_WIDTH
    wq = w_in[0][:, q0:q1].reshape(d, N_KV_HEADS, Q_PER_KV, HEAD_DIM).transpose(0, 2, 1, 3).reshape(d, ATTN_WIDTH)
    w_in_p = jnp.concatenate([w_in[0][:, :q0], wq, w_in[0][:, q1:]], axis=1).astype(BF16)
    wo = w_out[0][q0:].reshape(N_KV_HEADS, Q_PER_KV, HEAD_DIM, d).transpose(1, 0, 2, 3).reshape(ATTN_WIDTH, d)
    w_out_p = jnp.concatenate([w_out[0][:q0], wo], axis=0).astype(BF16)
    wsgu = jnp.concatenate([ws_gate[0], ws_up[0]], axis=1).astype(BF16)
    wr_hi = w_router[0].astype(BF16)
    wr_lo = (w_router[0] - wr_hi.astype(F32)).astype(BF16)
    wr3 = jnp.concatenate([wr_hi, wr_lo, wr_hi], axis=0).T
    g1 = g_norm1[0].reshape(1, d)
    mixer_weights = (
        w_pool[0].astype(BF16), b_pool[0].reshape(1, POOL_WIDTH), pool_scale[0].reshape(1, POOL_WIDTH),
        attn_sink[0].reshape(1, N_HEADS), w_out_p, g_norm2[0].reshape(1, d),
        wr3, router_bias[0].reshape(N_EXPERTS, 1), wsgu, ws_down[0].astype(BF16))
    ch = EXPERT_CHUNK

    def run_group(segments):
        n = _group_tokens(segments)
        u, q, k, v = _inproj(xp, xs, mod3, g1, w_in_p, segments)
        xa, h2p, idx_t, rank_t, wts, cnt = _mixer(xp, xs, mod3, u, q, k, v, *mixer_weights, segments)
        n_rows = (-(-n * TOP_K // ch) + N_EXPERTS) * ch
        counts = cnt[:, 0].astype(I32)
        padded = (counts + ch - 1) // ch * ch
        pend = jnp.cumsum(padded)
        pstart = pend - padded
        didx = _dest(idx_t, rank_t, pstart.astype(F32).reshape(N_EXPERTS, 1), n_rows)
        didx = didx.reshape(TOP_K, N_PLANES * n)
        xs_rows = _sc_scatter_rows(h2p.reshape(N_PLANES * n, LANES), didx, N_PLANES * n_rows)
        ys_rows = _experts(xs_rows.reshape(N_PLANES, n_rows, LANES), w_gate[0], w_up[0], w_down[0],
                           pstart // ch, padded // ch, counts, pend[-1:] // ch)
        yk = _sc_gather_rows(ys_rows.reshape(N_PLANES * n_rows, LANES), didx)
        return xa, yk.reshape(TOP_K, N_PLANES, n, LANES), wts

    groups = _token_groups(bp, sp, bs, ss)
    results = [run_group(g) for g in groups]

    gf = g_final.reshape(1, d)
    outs = [None, None]
    out_rows = (n_p, n_s)
    for segments, (xa, yk, wts) in zip(groups, results):
        tok = 0
        for src, tok0, n_seqs, seq_len, sid0 in segments:
            n_tok = n_seqs * seq_len
            outs[src] = _final(xa, yk, wts, mod3, gf, tok // T_FIN, n_tok, seq_len, sid0,
                               out_rows[src], tok0 // T_FIN, outs[src])
            tok += n_tok
    return outs[0].reshape(bp, sp, d), outs[1].reshape(bs, ss, d)
```

```python
import functools

import numpy as np
import jax
import jax.numpy as jnp
from jax import lax
from jax.experimental import pallas as pl
from jax.experimental.pallas import tpu as pltpu
from jax.experimental.pallas import tpu_sc as plsc

F32 = jnp.float32
BF16 = jnp.bfloat16
I32 = jnp.int32

D_MODEL = 1024
POOL_WINDOWS = (2, 4, 8, 16)
POOL_WIDTH = 512
POOL_GROUP = 128
N_HEADS = 8
N_KV_HEADS = 2
HEAD_DIM = 64
Q_PER_KV = N_HEADS // N_KV_HEADS
ATTN_WIDTH = N_HEADS * HEAD_DIM
KV_WIDTH = N_KV_HEADS * HEAD_DIM
D_IN_PROJ = POOL_WIDTH + ATTN_WIDTH + 2 * KV_WIDTH
WINDOW = 128
N_EXPERTS = 256
TOP_K = 8
N_EXPERT_GROUPS = 8
GROUP_SIZE = N_EXPERTS // N_EXPERT_GROUPS
TOPK_GROUPS = 4
D_EXPERT = 256
D_SHARED = 256
ROUTED_SCALE = 2.5
EPS = 1e-6
NEG_INF = -1e30

LANES = 128
HALO = 8
N_PLANES = 4
HALF = D_MODEL // 2

T_IN = 512
T_MIX = 256
ATT_BLOCK = 128
T_DEST = 2048
T_DEST_SLAB = 512
EXPERT_CHUNK = 512
EXPERT_NBUF = 4
T_FIN = 256
SC_ROWS = 128
SC_WORKERS = 32
SC_NBUF = 4
SC_LAG = 2
ADA_ROWS = 16
VMEM_LIMIT = 48 * 1024 * 1024

LOG2E = 1.4426950408889634
ALIBI_SLOPES = tuple(float(2.0 ** (-8.0 * (h + 1) / N_HEADS)) for h in range(N_HEADS))


def _pack_planes(y):
    planes = []
    for j in range(N_PLANES):
        hi = y[:, j * LANES:(j + 1) * LANES].astype(BF16).astype(F32)
        lo = y[:, HALF + j * LANES:HALF + (j + 1) * LANES].astype(BF16).astype(F32)
        hb = lax.bitcast_convert_type(hi, jnp.uint32) & jnp.uint32(0xFFFF0000)
        lb = lax.bitcast_convert_type(lo, jnp.uint32) >> jnp.uint32(16)
        planes.append(lax.bitcast_convert_type(hb | lb, I32))
    return planes


def _unpack_plane(w):
    u = lax.bitcast_convert_type(w, jnp.uint32)
    hi = lax.bitcast_convert_type(u & jnp.uint32(0xFFFF0000), F32)
    lo = lax.bitcast_convert_type(u << jnp.uint32(16), F32)
    return hi, lo


def _ada_kernel(c_ref, w_ref, b_ref, o_ref):
    c = c_ref[...]
    s = c * jax.nn.sigmoid(c)
    o_ref[...] = jnp.dot(s, w_ref[...], precision=lax.Precision.HIGHEST,
                         preferred_element_type=F32) + b_ref[...]


def _ada(c_all, w_ada, b_ada):
    n_out = w_ada.shape[1]
    tn = 1024
    return pl.pallas_call(
        _ada_kernel,
        grid=(n_out // tn,),
        in_specs=[pl.BlockSpec((ADA_ROWS, D_MODEL), lambda j: (0, 0)),
                  pl.BlockSpec((D_MODEL, tn), lambda j: (0, j)),
                  pl.BlockSpec((1, tn), lambda j: (0, j))],
        out_specs=pl.BlockSpec((ADA_ROWS, tn), lambda j: (0, j)),
        out_shape=jax.ShapeDtypeStruct((ADA_ROWS, n_out), F32),
        name="ada",
    )(c_all, w_ada, b_ada.reshape(1, n_out))


def _rms_mod(x, g, scale, shift):
    ms = jnp.mean(x * x, axis=-1, keepdims=True)
    return (x * lax.rsqrt(ms + EPS) * g) * (1.0 + scale) + shift


def _inproj_kernel(meta_ref, xp_ref, xs_ref, mod_ref, g_ref, w_ref, u_ref, q_ref, k_ref, v_ref):
    x = jnp.where(meta_ref[7, pl.program_id(0)] == 0, xp_ref[...], xs_ref[...])
    mod = mod_ref[0]
    h = _rms_mod(x, g_ref[...], mod[:, D_MODEL:2 * D_MODEL], mod[:, 0:D_MODEL])
    z = jnp.dot(h.astype(BF16), w_ref[...], preferred_element_type=F32)
    u_ref[...] = z[:, :POOL_WIDTH]
    q_ref[...] = (z[:, POOL_WIDTH:POOL_WIDTH + ATTN_WIDTH] * (LOG2E * HEAD_DIM ** -0.5)).astype(BF16)
    k_ref[...] = z[:, POOL_WIDTH + ATTN_WIDTH:POOL_WIDTH + ATTN_WIDTH + KV_WIDTH].astype(BF16)
    v_ref[...] = z[:, POOL_WIDTH + ATTN_WIDTH + KV_WIDTH:].astype(BF16)


def _tile_meta(segments, tile):
    rows = []
    for src, tok0, n_seqs, seq_len, sid0 in segments:
        per = seq_len // tile
        for q in range(n_seqs):
            for t in range(per):
                blk = (tok0 + q * seq_len) // tile + t
                rows.append((sid0 + q, int(t == 0), int(t == per - 1), t * tile, seq_len,
                             blk if src == 0 else -1, blk if src == 1 else -1, src))
    meta = np.asarray(rows, np.int32)
    for col in (5, 6):
        known = meta[:, col] >= 0
        if not known.any():
            meta[:, col] = 0
            continue
        last = np.maximum.accumulate(np.where(known, np.arange(len(meta)), -1))
        last = np.where(last < 0, np.argmax(known), last)
        meta[:, col] = meta[last, col]
    return meta.T.copy()


def _group_tokens(segments):
    return sum(n_seqs * seq_len for _, _, n_seqs, seq_len, _ in segments)


def _inproj(xp, xs, mod3, g1, w_in_bf, segments):
    n = _group_tokens(segments)
    meta = jnp.asarray(_tile_meta(segments, T_IN))
    tok = lambda w: pl.BlockSpec((T_IN, w), lambda i, m: (i, 0))
    return pl.pallas_call(
        _inproj_kernel,
        grid_spec=pltpu.PrefetchScalarGridSpec(
            num_scalar_prefetch=1,
            grid=(n // T_IN,),
            in_specs=[
                pl.BlockSpec((T_IN, D_MODEL), lambda i, m: (m[5, i], 0)),
                pl.BlockSpec((T_IN, D_MODEL), lambda i, m: (m[6, i], 0)),
                pl.BlockSpec((1, 1, 6 * D_MODEL), lambda i, m: (m[0, i], 0, 0)),
                pl.BlockSpec((1, D_MODEL), lambda i, m: (0, 0)),
                pl.BlockSpec((D_MODEL, D_IN_PROJ), lambda i, m: (0, 0)),
            ],
            out_specs=[tok(POOL_WIDTH), tok(ATTN_WIDTH), tok(KV_WIDTH), tok(KV_WIDTH)],
        ),
        out_shape=[jax.ShapeDtypeStruct((n, POOL_WIDTH), F32),
                   jax.ShapeDtypeStruct((n, ATTN_WIDTH), BF16),
                   jax.ShapeDtypeStruct((n, KV_WIDTH), BF16),
                   jax.ShapeDtypeStruct((n, KV_WIDTH), BF16)],
        compiler_params=pltpu.CompilerParams(dimension_semantics=("arbitrary",),
                                             vmem_limit_bytes=VMEM_LIMIT),
        name="inproj",
    )(meta, xp, xs, mod3, g1, w_in_bf)


def _pool_mixer(uext_ref, uc, pos, seq_len, wpool_ref, bpool_ref, pscale_ref):
    outs = []
    for gi, w in enumerate(POOL_WINDOWS):
        c0 = gi * POOL_GROUP
        half = w // 2
        acc = uext_ref[pl.ds(HALO - half, T_MIX), pl.ds(c0, POOL_GROUP)]
        for o in range(-half + 1, half):
            acc = acc + uext_ref[pl.ds(HALO + o, T_MIX), pl.ds(c0, POOL_GROUP)]
        lo = jnp.maximum(pos - half, 0)
        hi = jnp.minimum(pos + half, seq_len)
        cnt = (hi - lo).astype(F32)
        d = acc / cnt - uc[:, c0:c0 + POOL_GROUP]
        y = jnp.dot(d.astype(BF16), wpool_ref[gi], preferred_element_type=F32)
        y = (y + bpool_ref[:, c0:c0 + POOL_GROUP]) * pscale_ref[:, c0:c0 + POOL_GROUP]
        outs.append(y)
    return jnp.concatenate(outs, axis=1)


def _attention_bias():
    t = ATT_BLOCK
    r = np.arange(t)[:, None]
    c = np.arange(3 * t)[None, :]
    dist = np.abs(r - (c - t))
    band = dist <= WINDOW
    out = np.empty((3, N_HEADS, t, 3 * t), np.float32)
    for var, valid in enumerate((band, band & (c >= t), band & (c < 2 * t))):
        for h in range(N_HEADS):
            out[var, h] = np.where(valid, -ALIBI_SLOPES[h] * LOG2E * dist, NEG_INF)
    return out.reshape(3, N_HEADS * t, 3 * t)


def _banded_attention(q, kw, vw, variant, bias_ref, sink_ref):
    t = ATT_BLOCK
    low = lax.broadcasted_iota(I32, (t, LANES), 1) < HEAD_DIM
    zero = jnp.zeros((t, LANES), BF16)
    qs = []
    for h in range(N_HEADS):
        blk = q[:, (h % Q_PER_KV) * LANES:(h % Q_PER_KV + 1) * LANES]
        qs.append(jnp.where(low, blk, zero) if h < Q_PER_KV else jnp.where(low, zero, blk))
    s = lax.dot_general(jnp.concatenate(qs, axis=0), kw, (((1,), (1,)), ((), ())),
                        preferred_element_type=F32)
    ps, dens = [], []
    for h in range(N_HEADS):
        sh = s[h * t:(h + 1) * t] + bias_ref[variant, pl.ds(h * t, t), :]
        sink = sink_ref[:, h:h + 1] * LOG2E
        m = jnp.maximum(jnp.max(sh, axis=-1, keepdims=True), sink)
        p = jnp.exp2(sh - m)
        dens.append(jnp.sum(p, axis=-1, keepdims=True) + jnp.exp2(sink - m))
        ps.append(p.astype(BF16))
    o = jnp.dot(jnp.concatenate(ps, axis=0), vw, preferred_element_type=F32)
    oh = [o[h * t:(h + 1) * t] / dens[h] for h in range(N_HEADS)]
    return jnp.concatenate([jnp.where(low, oh[cb], oh[Q_PER_KV + cb]) for cb in range(Q_PER_KV)], axis=1)


def _route(biased, scores):
    t = biased.shape[1]
    rowf = lax.broadcasted_iota(I32, (N_EXPERTS, t), 0).astype(F32)
    ninf = float("-inf")
    gs = []
    for g in range(N_EXPERT_GROUPS):
        blk = biased[g * GROUP_SIZE:(g + 1) * GROUP_SIZE, :]
        rf = rowf[g * GROUP_SIZE:(g + 1) * GROUP_SIZE, :]
        m1 = jnp.max(blk, axis=0, keepdims=True)
        i1 = jnp.min(jnp.where(blk == m1, rf, float(N_EXPERTS)), axis=0, keepdims=True)
        m2 = jnp.max(jnp.where(rf == i1, ninf, blk), axis=0, keepdims=True)
        gs.append(m1 + m2)
    keep = []
    for g in range(N_EXPERT_GROUPS):
        beat = jnp.zeros((1, t), F32)
        for g2 in range(N_EXPERT_GROUPS):
            if g2 == g:
                continue
            better = (gs[g2] >= gs[g]) if g2 < g else (gs[g2] > gs[g])
            beat = beat + better.astype(F32)
        keep.append(jnp.broadcast_to(beat < float(TOPK_GROUPS), (GROUP_SIZE, t)))
    emask = jnp.concatenate(keep, axis=0)
    masked = jnp.where(emask, biased, NEG_INF)
    idx_rows, w_rows = [], []
    sel_any = jnp.zeros((N_EXPERTS, t), F32)
    for _ in range(TOP_K):
        m = jnp.max(masked, axis=0, keepdims=True)
        ik = jnp.min(jnp.where(masked == m, rowf, float(N_EXPERTS)), axis=0, keepdims=True)
        sel = rowf == ik
        w_rows.append(jnp.sum(jnp.where(sel, scores, 0.0), axis=0, keepdims=True))
        idx_rows.append(ik)
        masked = jnp.where(sel, ninf, masked)
        sel_any = sel_any + sel.astype(F32)
    return idx_rows, w_rows, sel_any, rowf


def _mixer_kernel(meta_ref, xp_ref, xs_ref, mod_ref, up_ref, uc_ref, un_ref, q_ref,
                  kp_ref, kc_ref, kn_ref, vp_ref, vc_ref, vn_ref,
                  wpool_ref, bpool_ref, pscale_ref, sink_ref, wout_ref, g2_ref,
                  wr3_ref, rbias_ref, wsgu_ref, wsd_ref, tri_ref, abias_ref,
                  xa_ref, h2p_ref, idx_ref, rank_ref, wts_ref, cnt_ref,
                  uext_ref, kext_ref, vext_ref, base_ref):
    i = pl.program_id(0)
    first = meta_ref[1, i] == 1
    last = meta_ref[2, i] == 1
    pos0 = meta_ref[3, i]
    seq_len = meta_ref[4, i]
    mod = mod_ref[0]
    gt1 = mod[:, 2 * D_MODEL:3 * D_MODEL]
    sh2 = mod[:, 3 * D_MODEL:4 * D_MODEL]
    sc2 = mod[:, 4 * D_MODEL:5 * D_MODEL]
    gt2 = mod[:, 5 * D_MODEL:6 * D_MODEL]

    @pl.when(i == 0)
    def _():
        base_ref[...] = jnp.zeros_like(base_ref)

    uc = uc_ref[...]
    uext_ref[pl.ds(0, HALO), :] = jnp.where(first, 0.0, up_ref[...])
    uext_ref[pl.ds(HALO, T_MIX), :] = uc
    uext_ref[pl.ds(HALO + T_MIX, HALO), :] = jnp.where(last, 0.0, un_ref[...])
    pos = pos0 + lax.broadcasted_iota(I32, (T_MIX, 1), 0)
    a_pool = _pool_mixer(uext_ref, uc, pos, seq_len, wpool_ref, bpool_ref, pscale_ref)

    t = ATT_BLOCK
    kext_ref[pl.ds(0, t), :] = kp_ref[...]
    kext_ref[pl.ds(t, T_MIX), :] = kc_ref[...]
    kext_ref[pl.ds(t + T_MIX, t), :] = kn_ref[...]
    vext_ref[pl.ds(0, t), :] = vp_ref[...]
    vext_ref[pl.ds(t, T_MIX), :] = vc_ref[...]
    vext_ref[pl.ds(t + T_MIX, t), :] = vn_ref[...]
    n_sub = T_MIX // t
    assert n_sub >= 2, "a query block is never both first and last in its sequence"
    attn = []
    for sub in range(n_sub):
        variant = 0
        if sub == 0:
            variant = jnp.where(first, 1, variant)
        if sub == n_sub - 1:
            variant = jnp.where(last, 2, variant)
        attn.append(_banded_attention(
            q_ref[pl.ds(sub * t, t), :], kext_ref[pl.ds(sub * t, 3 * t), :], vext_ref[pl.ds(sub * t, 3 * t), :],
            variant, abias_ref, sink_ref))
    a_attn = jnp.concatenate(attn, axis=0)

    a = jnp.concatenate([a_pool, a_attn], axis=1).astype(BF16)
    mix = jnp.dot(a, wout_ref[...], preferred_element_type=F32)

    x = jnp.where(meta_ref[7, i] == 0, xp_ref[...], xs_ref[...])
    x1 = x + gt1 * mix
    h2 = _rms_mod(x1, g2_ref[...], sc2, sh2)
    h2b = h2.astype(BF16)
    planes = _pack_planes(h2)
    for j in range(N_PLANES):
        h2p_ref[j] = planes[j]
    gu = jnp.dot(h2b, wsgu_ref[...], preferred_element_type=F32)
    gate, up = gu[:, :D_SHARED], gu[:, D_SHARED:]
    act = (gate * jax.nn.sigmoid(gate)) * up
    shared = jnp.dot(act.astype(BF16), wsd_ref[...], preferred_element_type=F32)
    xa_ref[...] = x1 + gt2 * shared
    h2lo = (h2 - h2b.astype(F32)).astype(BF16)
    logits = lax.dot_general(wr3_ref[...], jnp.concatenate([h2b, h2b, h2lo], axis=1),
                             (((1,), (1,)), ((), ())), preferred_element_type=F32)
    scores = jax.nn.sigmoid(logits)
    biased = scores + rbias_ref[...]
    routed = [_route(biased[:, c * LANES:(c + 1) * LANES], scores[:, c * LANES:(c + 1) * LANES])
              for c in range(T_MIX // LANES)]
    for c, (idx_rows, w_rows, _, _) in enumerate(routed):
        wsum = w_rows[0]
        for wr in w_rows[1:]:
            wsum = wsum + wr
        wts_t = jnp.concatenate([wr / wsum * ROUTED_SCALE for wr in w_rows], axis=0)
        wpad = jnp.concatenate([wts_t, jnp.zeros((LANES - TOP_K, LANES), F32)], axis=0)
        wts_ref[pl.ds(c * LANES, LANES), :] = wpad.T
        idx_ref[:, pl.ds(c * LANES, LANES)] = jnp.concatenate(idx_rows, axis=0).astype(I32)
    sel_any = jnp.concatenate([r[2] for r in routed], axis=1)
    pref = jnp.dot(sel_any.astype(BF16), tri_ref[...], preferred_element_type=F32)
    before = base_ref[...] + pref[:, :T_MIX]
    for c, (idx_rows, _, _, rowf) in enumerate(routed):
        bc = before[:, c * LANES:(c + 1) * LANES]
        ranks = [jnp.sum(jnp.where(rowf == ik, bc, 0.0), axis=0, keepdims=True) for ik in idx_rows]
        rank_ref[:, pl.ds(c * LANES, LANES)] = jnp.concatenate(ranks, axis=0).astype(I32)
    new_base = base_ref[...] + pref[:, T_MIX:]
    base_ref[...] = new_base
    cnt_ref[...] = new_base


def _mixer(xp, xs, mod3, u, q, k, v, wpool_bf, bpool, pscale, sink, wout_bf, g2, wr3, rbias,
           wsgu_bf, wsd_bf, segments):
    n = _group_tokens(segments)
    t = T_MIX
    nt = n // t
    ab = ATT_BLOCK
    meta = jnp.asarray(_tile_meta(segments, t))
    tri = np.concatenate([np.triu(np.ones((t, t), np.float32), 1), np.ones((t, t), np.float32)], axis=1)
    tri = jnp.asarray(tri, BF16)
    abias = jnp.asarray(_attention_bias())
    cur = lambda w: pl.BlockSpec((t, w), lambda i, m: (i, 0))
    prev = lambda w: pl.BlockSpec((ab, w), lambda i, m: (jnp.maximum(i * (t // ab) - 1, 0), 0))
    nxt = lambda w: pl.BlockSpec((ab, w), lambda i, m: (jnp.minimum((i + 1) * (t // ab), n // ab - 1), 0))
    full = lambda a: pl.BlockSpec(a.shape, lambda i, m: (0,) * a.ndim)
    hb = t // HALO
    in_specs = [
        pl.BlockSpec((t, D_MODEL), lambda i, m: (m[5, i], 0)),
        pl.BlockSpec((t, D_MODEL), lambda i, m: (m[6, i], 0)),
        pl.BlockSpec((1, 1, 6 * D_MODEL), lambda i, m: (m[0, i], 0, 0)),
        pl.BlockSpec((HALO, POOL_WIDTH), lambda i, m: (jnp.maximum(i * hb - 1, 0), 0)),
        cur(POOL_WIDTH),
        pl.BlockSpec((HALO, POOL_WIDTH), lambda i, m: (jnp.minimum((i + 1) * hb, n // HALO - 1), 0)),
        cur(ATTN_WIDTH),
        prev(KV_WIDTH), cur(KV_WIDTH), nxt(KV_WIDTH),
        prev(KV_WIDTH), cur(KV_WIDTH), nxt(KV_WIDTH),
        full(wpool_bf), full(bpool), full(pscale), full(sink), full(wout_bf), full(g2),
        full(wr3), full(rbias), full(wsgu_bf), full(wsd_bf), full(tri), full(abias),
    ]
    out_specs = [
        cur(D_MODEL),
        pl.BlockSpec((N_PLANES, t, LANES), lambda i, m: (0, i, 0)),
        pl.BlockSpec((TOP_K, t), lambda i, m: (0, i)),
        pl.BlockSpec((TOP_K, t), lambda i, m: (0, i)),
        cur(LANES),
        pl.BlockSpec((N_EXPERTS, t), lambda i, m: (0, 0)),
    ]
    out_shape = [
        jax.ShapeDtypeStruct((n, D_MODEL), F32),
        jax.ShapeDtypeStruct((N_PLANES, n, LANES), I32),
        jax.ShapeDtypeStruct((TOP_K, n), I32),
        jax.ShapeDtypeStruct((TOP_K, n), I32),
        jax.ShapeDtypeStruct((n, LANES), F32),
        jax.ShapeDtypeStruct((N_EXPERTS, t), F32),
    ]
    return pl.pallas_call(
        _mixer_kernel,
        grid_spec=pltpu.PrefetchScalarGridSpec(
            num_scalar_prefetch=1, grid=(nt,), in_specs=in_specs, out_specs=out_specs,
            scratch_shapes=[pltpu.VMEM((t + 2 * HALO, POOL_WIDTH), F32),
                            pltpu.VMEM((t + 2 * ab, KV_WIDTH), BF16),
                            pltpu.VMEM((t + 2 * ab, KV_WIDTH), BF16),
                            pltpu.VMEM((N_EXPERTS, t), F32)]),
        out_shape=out_shape,
        compiler_params=pltpu.CompilerParams(dimension_semantics=("arbitrary",),
                                             vmem_limit_bytes=VMEM_LIMIT),
        name="mixer",
    )(meta, xp, xs, mod3, u, u, u, q, k, k, k, v, v, v, wpool_bf, bpool, pscale, sink, wout_bf, g2,
      wr3, rbias, wsgu_bf, wsd_bf, tri, abias)


def _dest_kernel(idx_ref, rank_ref, pstart_ref, o_ref, *, n_rows):
    slab = min(T_DEST_SLAB, idx_ref.shape[1])
    rowi = lax.broadcasted_iota(I32, (N_EXPERTS, slab), 0)
    pstart = pstart_ref[...]
    for c in range(idx_ref.shape[1] // slab):
        lanes = pl.ds(c * slab, slab)
        rows = []
        for kk in range(TOP_K):
            sel = rowi == idx_ref[kk:kk + 1, lanes]
            start = jnp.sum(jnp.where(sel, pstart, 0.0), axis=0, keepdims=True)
            rows.append(start.astype(I32) + rank_ref[kk:kk + 1, lanes])
        dest = jnp.concatenate(rows, axis=0)
        for j in range(N_PLANES):
            o_ref[:, j, lanes] = dest + j * n_rows


def _dest(idx_t, rank_t, pstart_col, n_rows):
    n = idx_t.shape[1]
    td = int(np.gcd(n, T_DEST))
    return pl.pallas_call(
        functools.partial(_dest_kernel, n_rows=n_rows),
        grid=(n // td,),
        in_specs=[pl.BlockSpec((TOP_K, td), lambda i: (0, i)),
                  pl.BlockSpec((TOP_K, td), lambda i: (0, i)),
                  pl.BlockSpec((N_EXPERTS, 1), lambda i: (0, 0))],
        out_specs=pl.BlockSpec((TOP_K, N_PLANES, td), lambda i: (0, 0, i)),
        out_shape=jax.ShapeDtypeStruct((TOP_K, N_PLANES, n), I32),
        name="dest",
    )(idx_t, rank_t, pstart_col)


def _sc_mesh():
    return plsc.VectorSubcoreMesh(core_axis_name="c", subcore_axis_name="s")


def _sc_worker():
    return lax.axis_index("s") * 2 + lax.axis_index("c")


def _sc_scatter_rows(table, didx, n_out_rows):
    m = table.shape[0]
    per_worker = m // SC_ROWS // SC_WORKERS
    assert per_worker % 2 == 0

    @functools.partial(
        pl.kernel, mesh=_sc_mesh(),
        out_type=jax.ShapeDtypeStruct((n_out_rows, LANES), I32),
        scratch_types=[pltpu.VMEM((2, SC_ROWS, LANES), I32), pltpu.VMEM((2, TOP_K, SC_ROWS), I32),
                       pltpu.SemaphoreType.DMA((2,))])
    def run(table_hbm, didx_hbm, out_hbm, rows, idx, ssem):
        wid = _sc_worker()

        def scatter(p, kk):
            return pltpu.make_async_copy(rows.at[p], out_hbm.at[idx.at[p].at[kk]], ssem.at[p])

        @pl.loop(0, per_worker, step=2)
        def _(s0):
            for p in range(2):
                @pl.when(s0 > 0)
                def _():
                    for kk in range(TOP_K):
                        scatter(p, kk).wait()

                off = pl.multiple_of((wid * per_worker + s0 + p) * SC_ROWS, SC_ROWS)
                pltpu.sync_copy(table_hbm.at[pl.ds(off, SC_ROWS)], rows.at[p])
                pltpu.sync_copy(didx_hbm.at[:, pl.ds(off, SC_ROWS)], idx.at[p])
                for kk in range(TOP_K):
                    scatter(p, kk).start()

        for p in range(2):
            for kk in range(TOP_K):
                scatter(p, kk).wait()

    return run(table, didx)


def _sc_gather_rows(table, didx):
    m = didx.shape[1]
    per_worker = m // SC_ROWS // SC_WORKERS
    assert per_worker % 2 == 0 and (2 * TOP_K) % SC_NBUF == 0
    items = 2 * TOP_K

    @functools.partial(
        pl.kernel, mesh=_sc_mesh(),
        out_type=jax.ShapeDtypeStruct((TOP_K, m, LANES), I32),
        scratch_types=[pltpu.VMEM((SC_NBUF, SC_ROWS, LANES), I32), pltpu.VMEM((2, TOP_K, SC_ROWS), I32),
                       pltpu.SemaphoreType.DMA((SC_NBUF,)), pltpu.SemaphoreType.DMA((SC_NBUF,)),
                       pltpu.SemaphoreType.DMA((2,))])
    def run(table_hbm, didx_hbm, out_hbm, rows, idx, gsem, wsem, isem):
        wid = _sc_worker()

        def off_of(step):
            return pl.multiple_of((wid * per_worker + step) * SC_ROWS, SC_ROWS)

        def idx_copy(step, p):
            return pltpu.make_async_copy(didx_hbm.at[:, pl.ds(off_of(step), SC_ROWS)], idx.at[p], isem.at[p])

        def gather(j):
            r = j % SC_NBUF
            return pltpu.make_async_copy(table_hbm.at[idx.at[j // TOP_K].at[j % TOP_K]], rows.at[r], gsem.at[r])

        def write(s0, j):
            r = j % SC_NBUF
            return pltpu.make_async_copy(rows.at[r], out_hbm.at[j % TOP_K, pl.ds(off_of(s0 + j // TOP_K), SC_ROWS)],
                                         wsem.at[r])

        def retire_write(s0, j):
            if j >= 0:
                write(s0, j).wait()
            else:
                @pl.when(s0 > 0)
                def _():
                    write(s0 - 2, j + items).wait()

        def finish_read(s0, j):
            if j >= 0:
                gather(j).wait()
                write(s0, j).start()
            else:
                @pl.when(s0 > 0)
                def _():
                    gather(j + items).wait()
                    write(s0 - 2, j + items).start()

        idx_copy(0, 0).start()

        @pl.loop(0, per_worker, step=2)
        def _(s0):
            for j in range(items):
                p, kk = j // TOP_K, j % TOP_K
                if kk == 0:
                    idx_copy(s0 + p, p).wait()
                retire_write(s0, j - SC_NBUF)
                gather(j).start()
                finish_read(s0, j - SC_LAG)
                if kk == SC_LAG:
                    @pl.when(s0 + p + 1 < per_worker)
                    def _():
                        idx_copy(s0 + p + 1, 1 - p).start()

        last = per_worker - 2
        for j in range(items - SC_LAG, items):
            gather(j).wait()
            write(last, j).start()
        for j in range(items - SC_NBUF, items):
            write(last, j).wait()

    return run(table, didx)


def _experts_kernel(cstart_ref, nchunk_ref, count_ref, total_ref, x_hbm, wg_ref, wu_ref, wd_ref, y_hbm,
                    wgu_s, wd_s, xbuf, ybuf, xsem, ysem):
    e = pl.program_id(0)
    nb = xbuf.shape[0]
    ch = xbuf.shape[2]
    total = total_ref[0]

    def x_copy(c):
        slot = lax.rem(c, nb)
        rows = pl.ds(pl.multiple_of(c * ch, ch), ch)
        return pltpu.make_async_copy(x_hbm.at[:, rows, :], xbuf.at[slot], xsem.at[slot])

    def y_copy(c):
        slot = lax.rem(c, nb)
        rows = pl.ds(pl.multiple_of(c * ch, ch), ch)
        return pltpu.make_async_copy(ybuf.at[slot], y_hbm.at[:, rows, :], ysem.at[slot])

    @pl.when(e == 0)
    def _():
        for c in range(nb - 1):
            @pl.when(c < total)
            def _():
                x_copy(c).start()

    for j in range(N_PLANES):
        hi = pl.ds(j * LANES, LANES)
        lo = pl.ds(HALF + j * LANES, LANES)
        r_hi = pl.ds(2 * j * LANES, LANES)
        r_lo = pl.ds((2 * j + 1) * LANES, LANES)
        wgu_s[r_hi, pl.ds(0, D_EXPERT)] = wg_ref[0, hi, :].astype(BF16)
        wgu_s[r_lo, pl.ds(0, D_EXPERT)] = wg_ref[0, lo, :].astype(BF16)
        wgu_s[r_hi, pl.ds(D_EXPERT, D_EXPERT)] = wu_ref[0, hi, :].astype(BF16)
        wgu_s[r_lo, pl.ds(D_EXPERT, D_EXPERT)] = wu_ref[0, lo, :].astype(BF16)
    wd_s[...] = wd_ref[0].astype(BF16)

    c0 = cstart_ref[e]
    count = count_ref[e]

    def chunk(i, carry):
        c = c0 + i
        slot = lax.rem(c, nb)

        @pl.when(c + nb - 1 < total)
        def _():
            x_copy(c + nb - 1).start()

        x_copy(c).wait()

        @pl.when(c >= nb)
        def _():
            y_copy(c - nb).wait()

        live = lax.broadcasted_iota(I32, (ch, LANES), 0) < count - i * ch
        xs = []
        for j in range(N_PLANES):
            hi, lo = _unpack_plane(jnp.where(live, xbuf[slot, j], 0))
            xs += [hi.astype(BF16), lo.astype(BF16)]
        acc = jnp.dot(jnp.concatenate(xs, axis=1), wgu_s[...], preferred_element_type=F32)
        gate, up = acc[:, :D_EXPERT], acc[:, D_EXPERT:]
        act = (gate * jax.nn.sigmoid(gate)) * up
        y = jnp.dot(act.astype(BF16), wd_s[...], preferred_element_type=F32)
        planes = _pack_planes(y)
        for j in range(N_PLANES):
            ybuf[slot, j] = planes[j]
        y_copy(c).start()
        return carry

    lax.fori_loop(0, nchunk_ref[e], chunk, 0)

    @pl.when(e == pl.num_programs(0) - 1)
    def _():
        for back in range(nb, 0, -1):
            @pl.when(total >= back)
            def _():
                y_copy(total - back).wait()


def _experts(xs_planes, w_gate, w_up, w_down, cstart, nchunk, counts, total):
    n_rows = xs_planes.shape[1]
    ch = EXPERT_CHUNK
    w_map = lambda e, *_: (e, 0, 0)
    any_space = pl.BlockSpec(memory_space=pl.ANY)
    return pl.pallas_call(
        _experts_kernel,
        grid_spec=pltpu.PrefetchScalarGridSpec(
            num_scalar_prefetch=4, grid=(N_EXPERTS,),
            in_specs=[any_space,
                      pl.BlockSpec((1, D_MODEL, D_EXPERT), w_map),
                      pl.BlockSpec((1, D_MODEL, D_EXPERT), w_map),
                      pl.BlockSpec((1, D_EXPERT, D_MODEL), w_map)],
            out_specs=any_space,
            scratch_shapes=[pltpu.VMEM((D_MODEL, 2 * D_EXPERT), BF16),
                            pltpu.VMEM((D_EXPERT, D_MODEL), BF16),
                            pltpu.VMEM((EXPERT_NBUF, N_PLANES, ch, LANES), I32),
                            pltpu.VMEM((EXPERT_NBUF, N_PLANES, ch, LANES), I32),
                            pltpu.SemaphoreType.DMA((EXPERT_NBUF,)),
                            pltpu.SemaphoreType.DMA((EXPERT_NBUF,))]),
        out_shape=jax.ShapeDtypeStruct((N_PLANES, n_rows, LANES), I32),
        compiler_params=pltpu.CompilerParams(dimension_semantics=("arbitrary",),
                                             vmem_limit_bytes=VMEM_LIMIT),
        name="experts",
    )(cstart, nchunk, counts, total, xs_planes, w_gate, w_up, w_down)


def _final_kernel(xa_ref, yk_ref, wts_ref, mod_ref, g_ref, o_ref):
    wts = wts_ref[...]
    his = [jnp.zeros((T_FIN, LANES), F32) for _ in range(N_PLANES)]
    los = [jnp.zeros((T_FIN, LANES), F32) for _ in range(N_PLANES)]
    for kk in range(TOP_K):
        wk = wts[:, kk:kk + 1]
        for j in range(N_PLANES):
            hi, lo = _unpack_plane(yk_ref[kk, j])
            his[j] = his[j] + wk * hi
            los[j] = los[j] + wk * lo
    routed = jnp.concatenate(his + los, axis=1)
    gt2 = mod_ref[0][:, 5 * D_MODEL:6 * D_MODEL]
    x = xa_ref[...] + gt2 * routed
    ms = jnp.mean(x * x, axis=-1, keepdims=True)
    o_ref[...] = x * lax.rsqrt(ms + EPS) * g_ref[...]


def _final_kernel_into(xa_ref, yk_ref, wts_ref, mod_ref, g_ref, prev_ref, o_ref):
    _final_kernel(xa_ref, yk_ref, wts_ref, mod_ref, g_ref, o_ref)


def _final(xa, yk, wts, mod3, g_final, tile_off, n_tok, seq_len, sid0, out_rows, out_tile_off, prev):
    tiles_per_seq = seq_len // T_FIN
    in_specs = [pl.BlockSpec((T_FIN, D_MODEL), lambda i: (i + tile_off, 0)),
                pl.BlockSpec((TOP_K, N_PLANES, T_FIN, LANES), lambda i: (0, 0, i + tile_off, 0)),
                pl.BlockSpec((T_FIN, LANES), lambda i: (i + tile_off, 0)),
                pl.BlockSpec((1, 1, 6 * D_MODEL), lambda i: (sid0 + i // tiles_per_seq, 0, 0)),
                pl.BlockSpec((1, D_MODEL), lambda i: (0, 0))]
    args = [xa, yk, wts, mod3, g_final]
    if prev is not None:
        in_specs.append(pl.BlockSpec(memory_space=pl.ANY))
        args.append(prev)
    return pl.pallas_call(
        _final_kernel if prev is None else _final_kernel_into,
        grid=(n_tok // T_FIN,),
        in_specs=in_specs,
        out_specs=pl.BlockSpec((T_FIN, D_MODEL), lambda i: (i + out_tile_off, 0)),
        out_shape=jax.ShapeDtypeStruct((out_rows, D_MODEL), F32),
        input_output_aliases={} if prev is None else {len(args) - 1: 0},
        compiler_params=pltpu.CompilerParams(dimension_semantics=("arbitrary",),
                                             vmem_limit_bytes=VMEM_LIMIT),
        name="final",
    )(*args)


def _token_groups(bp, sp, bs, ss):
    seqs = [(0, q * sp, sp, q) for q in range(bp)] + [(1, q * ss, ss, bp + q) for q in range(bs)]
    half = (bp * sp + bs * ss) / 2
    groups, cur, acc = [], [], 0
    for src, tok0, seq_len, sid in seqs:
        if cur and cur[-1][0] == src and cur[-1][3] == seq_len and cur[-1][1] + cur[-1][2] * seq_len == tok0:
            cur[-1] = (src, cur[-1][1], cur[-1][2] + 1, seq_len, cur[-1][4])
        else:
            cur.append((src, tok0, 1, seq_len, sid))
        acc += seq_len
        if not groups and acc >= half:
            groups.append(cur)
            cur = []
    if cur:
        groups.append(cur)
    return groups


def kernel(x_prompt, x_sample, c_prompt, c_sample, w_ada, b_ada, g_norm1, w_in, w_pool, b_pool, pool_scale, attn_sink, w_out, g_norm2, w_router, router_bias, w_gate, w_up, w_down, ws_gate, ws_up, ws_down, g_final):
    assert w_ada.shape[0] == 1, "one layer"
    bp, sp, d = x_prompt.shape
    bs, ss, _ = x_sample.shape
    assert d == D_MODEL and bp + bs <= ADA_ROWS
    n_p, n_s = bp * sp, bs * ss
    xp = x_prompt.reshape(n_p, d)
    xs = x_sample.reshape(n_s, d)

    c_all = jnp.concatenate([c_prompt, c_sample, jnp.zeros((ADA_ROWS - bp - bs, d), F32)], axis=0)
    mod3 = _ada(c_all, w_ada[0], b_ada[0]).reshape(ADA_ROWS, 1, 6 * d)

    q0, q1 = POOL_WIDTH, POOL_WIDTH + ATTN_WIDTH
    wq = w_in[0][:, q0:q1].reshape(d, N_KV_HEADS, Q_PER_KV, HEAD_DIM).transpose(0, 2, 1, 3).reshape(d, ATTN_WIDTH)
    w_in_p = jnp.concatenate([w_in[0][:, :q0], wq, w_in[0][:, q1:]], axis=1).astype(BF16)
    wo = w_out[0][q0:].reshape(N_KV_HEADS, Q_PER_KV, HEAD_DIM, d).transpose(1, 0, 2, 3).reshape(ATTN_WIDTH, d)
    w_out_p = jnp.concatenate([w_out[0][:q0], wo], axis=0).astype(BF16)
    wsgu = jnp.concatenate([ws_gate[0], ws_up[0]], axis=1).astype(BF16)
    wr_hi = w_router[0].astype(BF16)
    wr_lo = (w_router[0] - wr_hi.astype(F32)).astype(BF16)
    wr3 = jnp.concatenate([wr_hi, wr_lo, wr_hi], axis=0).T
    g1 = g_norm1[0].reshape(1, d)
    mixer_weights = (
        w_pool[0].astype(BF16), b_pool[0].reshape(1, POOL_WIDTH), pool_scale[0].reshape(1, POOL_WIDTH),
        attn_sink[0].reshape(1, N_HEADS), w_out_p, g_norm2[0].reshape(1, d),
        wr3, router_bias[0].reshape(N_EXPERTS, 1), wsgu, ws_down[0].astype(BF16))
    ch = EXPERT_CHUNK

    def run_group(segments, after):
        n = _group_tokens(segments)
        u, q, k, v = _inproj(xp, xs, mod3, g1, w_in_p, segments)
        mod3_g = mod3 if after is None else lax.optimization_barrier((mod3, after))[0]
        xa, h2p, idx_t, rank_t, wts, cnt = _mixer(xp, xs, mod3_g, u, q, k, v, *mixer_weights, segments)
        n_rows = (-(-n * TOP_K // ch) + N_EXPERTS) * ch
        counts = cnt[:, 0].astype(I32)
        padded = (counts + ch - 1) // ch * ch
        pend = jnp.cumsum(padded)
        pstart = pend - padded
        didx = _dest(idx_t, rank_t, pstart.astype(F32).reshape(N_EXPERTS, 1), n_rows)
        didx = didx.reshape(TOP_K, N_PLANES * n)
        xs_rows = _sc_scatter_rows(h2p.reshape(N_PLANES * n, LANES), didx, N_PLANES * n_rows)
        ys_rows = _experts(xs_rows.reshape(N_PLANES, n_rows, LANES), w_gate[0], w_up[0], w_down[0],
                           pstart // ch, padded // ch, counts, pend[-1:] // ch)
        yk = _sc_gather_rows(ys_rows.reshape(N_PLANES * n_rows, LANES), didx)
        return (xa, yk.reshape(TOP_K, N_PLANES, n, LANES), wts), ys_rows

    groups = _token_groups(bp, sp, bs, ss)
    results, after = [], None
    for g in groups:
        res, after = run_group(g, after)
        results.append(res)

    gf = g_final.reshape(1, d)
    outs = [None, None]
    out_rows = (n_p, n_s)
    for segments, (xa, yk, wts) in zip(groups, results):
        tok = 0
        for src, tok0, n_seqs, seq_len, sid0 in segments:
            n_tok = n_seqs * seq_len
            outs[src] = _final(xa, yk, wts, mod3, gf, tok // T_FIN, n_tok, seq_len, sid0,
                               out_rows[src], tok0 // T_FIN, outs[src])
            tok += n_tok
    return outs[0].reshape(bp, sp, d), outs[1].reshape(bs, ss, d)
```

```python
import functools

import numpy as np
import jax
import jax.numpy as jnp
from jax import lax
from jax.experimental import pallas as pl
from jax.experimental.pallas import tpu as pltpu
from jax.experimental.pallas import tpu_sc as plsc

F32 = jnp.float32
BF16 = jnp.bfloat16
I32 = jnp.int32

D_MODEL = 1024
POOL_WINDOWS = (2, 4, 8, 16)
POOL_WIDTH = 512
POOL_GROUP = 128
N_HEADS = 8
N_KV_HEADS = 2
HEAD_DIM = 64
Q_PER_KV = N_HEADS // N_KV_HEADS
ATTN_WIDTH = N_HEADS * HEAD_DIM
KV_WIDTH = N_KV_HEADS * HEAD_DIM
D_IN_PROJ = POOL_WIDTH + ATTN_WIDTH + 2 * KV_WIDTH
WINDOW = 128
N_EXPERTS = 256
TOP_K = 8
N_EXPERT_GROUPS = 8
GROUP_SIZE = N_EXPERTS // N_EXPERT_GROUPS
TOPK_GROUPS = 4
D_EXPERT = 256
D_SHARED = 256
ROUTED_SCALE = 2.5
EPS = 1e-6
NEG_INF = -1e30

LANES = 128
HALO = 8
N_PLANES = 4
HALF = D_MODEL // 2

T_IN = 512
T_MIX = 512
ATT_BLOCK = 128
T_DEST = 2048
T_DEST_SLAB = 512
EXPERT_CHUNK = 512
EXPERT_NBUF = 4
T_FIN = 256
SC_ROWS = 128
SC_WORKERS = 32
SC_NBUF = 4
SC_LAG = 2
ADA_ROWS = 16
VMEM_LIMIT = 48 * 1024 * 1024

LOG2E = 1.4426950408889634
ALIBI_SLOPES = tuple(float(2.0 ** (-8.0 * (h + 1) / N_HEADS)) for h in range(N_HEADS))


def _pack_planes(y):
    planes = []
    for j in range(N_PLANES):
        hi = y[:, j * LANES:(j + 1) * LANES].astype(BF16).astype(F32)
        lo = y[:, HALF + j * LANES:HALF + (j + 1) * LANES].astype(BF16).astype(F32)
        hb = lax.bitcast_convert_type(hi, jnp.uint32) & jnp.uint32(0xFFFF0000)
        lb = lax.bitcast_convert_type(lo, jnp.uint32) >> jnp.uint32(16)
        planes.append(lax.bitcast_convert_type(hb | lb, I32))
    return planes


def _unpack_plane(w):
    u = lax.bitcast_convert_type(w, jnp.uint32)
    hi = lax.bitcast_convert_type(u & jnp.uint32(0xFFFF0000), F32)
    lo = lax.bitcast_convert_type(u << jnp.uint32(16), F32)
    return hi, lo


def _ada_kernel(c_ref, w_ref, b_ref, o_ref):
    c = c_ref[...]
    s = c * jax.nn.sigmoid(c)
    o_ref[...] = jnp.dot(s, w_ref[...], precision=lax.Precision.HIGHEST,
                         preferred_element_type=F32) + b_ref[...]


def _ada(c_all, w_ada, b_ada):
    n_out = w_ada.shape[1]
    tn = 1024
    return pl.pallas_call(
        _ada_kernel,
        grid=(n_out // tn,),
        in_specs=[pl.BlockSpec((ADA_ROWS, D_MODEL), lambda j: (0, 0)),
                  pl.BlockSpec((D_MODEL, tn), lambda j: (0, j)),
                  pl.BlockSpec((1, tn), lambda j: (0, j))],
        out_specs=pl.BlockSpec((ADA_ROWS, tn), lambda j: (0, j)),
        out_shape=jax.ShapeDtypeStruct((ADA_ROWS, n_out), F32),
        name="ada",
    )(c_all, w_ada, b_ada.reshape(1, n_out))


def _rms_mod(x, g, scale, shift):
    ms = jnp.mean(x * x, axis=-1, keepdims=True)
    return (x * lax.rsqrt(ms + EPS) * g) * (1.0 + scale) + shift


def _inproj_kernel(meta_ref, xp_ref, xs_ref, mod_ref, g_ref, w_ref, u_ref, q_ref, k_ref, v_ref):
    x = jnp.where(meta_ref[7, pl.program_id(0)] == 0, xp_ref[...], xs_ref[...])
    mod = mod_ref[0]
    h = _rms_mod(x, g_ref[...], mod[:, D_MODEL:2 * D_MODEL], mod[:, 0:D_MODEL])
    z = jnp.dot(h.astype(BF16), w_ref[...], preferred_element_type=F32)
    u_ref[...] = z[:, :POOL_WIDTH]
    q_ref[...] = (z[:, POOL_WIDTH:POOL_WIDTH + ATTN_WIDTH] * (LOG2E * HEAD_DIM ** -0.5)).astype(BF16)
    k_ref[...] = z[:, POOL_WIDTH + ATTN_WIDTH:POOL_WIDTH + ATTN_WIDTH + KV_WIDTH].astype(BF16)
    v_ref[...] = z[:, POOL_WIDTH + ATTN_WIDTH + KV_WIDTH:].astype(BF16)


def _tile_meta(segments, tile):
    rows = []
    for src, tok0, n_seqs, seq_len, sid0 in segments:
        per = seq_len // tile
        for q in range(n_seqs):
            for t in range(per):
                blk = (tok0 + q * seq_len) // tile + t
                rows.append((sid0 + q, int(t == 0), int(t == per - 1), t * tile, seq_len,
                             blk if src == 0 else -1, blk if src == 1 else -1, src))
    meta = np.asarray(rows, np.int32)
    for col in (5, 6):
        known = meta[:, col] >= 0
        if not known.any():
            meta[:, col] = 0
            continue
        last = np.maximum.accumulate(np.where(known, np.arange(len(meta)), -1))
        last = np.where(last < 0, np.argmax(known), last)
        meta[:, col] = meta[last, col]
    return meta.T.copy()


def _group_tokens(segments):
    return sum(n_seqs * seq_len for _, _, n_seqs, seq_len, _ in segments)


def _inproj(xp, xs, mod3, g1, w_in_bf, segments):
    n = _group_tokens(segments)
    meta = jnp.asarray(_tile_meta(segments, T_IN))
    tok = lambda w: pl.BlockSpec((T_IN, w), lambda i, m: (i, 0))
    return pl.pallas_call(
        _inproj_kernel,
        grid_spec=pltpu.PrefetchScalarGridSpec(
            num_scalar_prefetch=1,
            grid=(n // T_IN,),
            in_specs=[
                pl.BlockSpec((T_IN, D_MODEL), lambda i, m: (m[5, i], 0)),
                pl.BlockSpec((T_IN, D_MODEL), lambda i, m: (m[6, i], 0)),
                pl.BlockSpec((1, 1, 6 * D_MODEL), lambda i, m: (m[0, i], 0, 0)),
                pl.BlockSpec((1, D_MODEL), lambda i, m: (0, 0)),
                pl.BlockSpec((D_MODEL, D_IN_PROJ), lambda i, m: (0, 0)),
            ],
            out_specs=[tok(POOL_WIDTH), tok(ATTN_WIDTH), tok(KV_WIDTH), tok(KV_WIDTH)],
        ),
        out_shape=[jax.ShapeDtypeStruct((n, POOL_WIDTH), F32),
                   jax.ShapeDtypeStruct((n, ATTN_WIDTH), BF16),
                   jax.ShapeDtypeStruct((n, KV_WIDTH), BF16),
                   jax.ShapeDtypeStruct((n, KV_WIDTH), BF16)],
        compiler_params=pltpu.CompilerParams(dimension_semantics=("arbitrary",),
                                             vmem_limit_bytes=VMEM_LIMIT),
        name="inproj",
    )(meta, xp, xs, mod3, g1, w_in_bf)


def _pool_mixer(uext_ref, uc, pos, seq_len, wpool_ref, bpool_ref, pscale_ref):
    outs = []
    for gi, w in enumerate(POOL_WINDOWS):
        c0 = gi * POOL_GROUP
        half = w // 2
        acc = uext_ref[pl.ds(HALO - half, T_MIX), pl.ds(c0, POOL_GROUP)]
        for o in range(-half + 1, half):
            acc = acc + uext_ref[pl.ds(HALO + o, T_MIX), pl.ds(c0, POOL_GROUP)]
        lo = jnp.maximum(pos - half, 0)
        hi = jnp.minimum(pos + half, seq_len)
        cnt = (hi - lo).astype(F32)
        d = acc / cnt - uc[:, c0:c0 + POOL_GROUP]
        y = jnp.dot(d.astype(BF16), wpool_ref[gi], preferred_element_type=F32)
        y = (y + bpool_ref[:, c0:c0 + POOL_GROUP]) * pscale_ref[:, c0:c0 + POOL_GROUP]
        outs.append(y)
    return jnp.concatenate(outs, axis=1)


def _attention_bias():
    t = ATT_BLOCK
    r = np.arange(t)[:, None]
    c = np.arange(3 * t)[None, :]
    dist = np.abs(r - (c - t))
    band = dist <= WINDOW
    out = np.empty((3, N_HEADS, t, 3 * t), np.float32)
    for var, valid in enumerate((band, band & (c >= t), band & (c < 2 * t))):
        for h in range(N_HEADS):
            out[var, h] = np.where(valid, -ALIBI_SLOPES[h] * LOG2E * dist, NEG_INF)
    return out.reshape(3, N_HEADS * t, 3 * t)


def _banded_attention(q, kw, vw, variant, bias_ref, sink_ref):
    t = ATT_BLOCK
    low = lax.broadcasted_iota(I32, (t, LANES), 1) < HEAD_DIM
    zero = jnp.zeros((t, LANES), BF16)
    qs = []
    for h in range(N_HEADS):
        blk = q[:, (h % Q_PER_KV) * LANES:(h % Q_PER_KV + 1) * LANES]
        qs.append(jnp.where(low, blk, zero) if h < Q_PER_KV else jnp.where(low, zero, blk))
    s = lax.dot_general(jnp.concatenate(qs, axis=0), kw, (((1,), (1,)), ((), ())),
                        preferred_element_type=F32)
    ps, dens = [], []
    for h in range(N_HEADS):
        sh = s[h * t:(h + 1) * t] + bias_ref[variant, pl.ds(h * t, t), :]
        sink = sink_ref[:, h:h + 1] * LOG2E
        m = jnp.maximum(jnp.max(sh, axis=-1, keepdims=True), sink)
        p = jnp.exp2(sh - m)
        dens.append(jnp.sum(p, axis=-1, keepdims=True) + jnp.exp2(sink - m))
        ps.append(p.astype(BF16))
    o = jnp.dot(jnp.concatenate(ps, axis=0), vw, preferred_element_type=F32)
    oh = [o[h * t:(h + 1) * t] / dens[h] for h in range(N_HEADS)]
    return jnp.concatenate([jnp.where(low, oh[cb], oh[Q_PER_KV + cb]) for cb in range(Q_PER_KV)], axis=1)


def _route(biased, scores):
    t = biased.shape[1]
    rowf = lax.broadcasted_iota(I32, (N_EXPERTS, t), 0).astype(F32)
    ninf = float("-inf")
    gs = []
    for g in range(N_EXPERT_GROUPS):
        blk = biased[g * GROUP_SIZE:(g + 1) * GROUP_SIZE, :]
        rf = rowf[g * GROUP_SIZE:(g + 1) * GROUP_SIZE, :]
        m1 = jnp.max(blk, axis=0, keepdims=True)
        i1 = jnp.min(jnp.where(blk == m1, rf, float(N_EXPERTS)), axis=0, keepdims=True)
        m2 = jnp.max(jnp.where(rf == i1, ninf, blk), axis=0, keepdims=True)
        gs.append(m1 + m2)
    keep = []
    for g in range(N_EXPERT_GROUPS):
        beat = jnp.zeros((1, t), F32)
        for g2 in range(N_EXPERT_GROUPS):
            if g2 == g:
                continue
            better = (gs[g2] >= gs[g]) if g2 < g else (gs[g2] > gs[g])
            beat = beat + better.astype(F32)
        keep.append(jnp.broadcast_to(beat < float(TOPK_GROUPS), (GROUP_SIZE, t)))
    emask = jnp.concatenate(keep, axis=0)
    masked = jnp.where(emask, biased, NEG_INF)
    idx_rows, w_rows = [], []
    sel_any = jnp.zeros((N_EXPERTS, t), F32)
    for _ in range(TOP_K):
        m = jnp.max(masked, axis=0, keepdims=True)
        ik = jnp.min(jnp.where(masked == m, rowf, float(N_EXPERTS)), axis=0, keepdims=True)
        sel = rowf == ik
        w_rows.append(jnp.sum(jnp.where(sel, scores, 0.0), axis=0, keepdims=True))
        idx_rows.append(ik)
        masked = jnp.where(sel, ninf, masked)
        sel_any = sel_any + sel.astype(F32)
    return idx_rows, w_rows, sel_any, rowf


def _mixer_kernel(meta_ref, xp_ref, xs_ref, mod_ref, up_ref, uc_ref, un_ref, q_ref,
                  kp_ref, kc_ref, kn_ref, vp_ref, vc_ref, vn_ref,
                  wpool_ref, bpool_ref, pscale_ref, sink_ref, wout_ref, g2_ref,
                  wr3_ref, rbias_ref, wsgu_ref, wsd_ref, tri_ref, abias_ref,
                  xa_ref, h2p_ref, idx_ref, rank_ref, wts_ref, cnt_ref,
                  uext_ref, kext_ref, vext_ref, base_ref):
    i = pl.program_id(0)
    first = meta_ref[1, i] == 1
    last = meta_ref[2, i] == 1
    pos0 = meta_ref[3, i]
    seq_len = meta_ref[4, i]
    mod = mod_ref[0]
    gt1 = mod[:, 2 * D_MODEL:3 * D_MODEL]
    sh2 = mod[:, 3 * D_MODEL:4 * D_MODEL]
    sc2 = mod[:, 4 * D_MODEL:5 * D_MODEL]
    gt2 = mod[:, 5 * D_MODEL:6 * D_MODEL]

    @pl.when(i == 0)
    def _():
        base_ref[...] = jnp.zeros_like(base_ref)

    uc = uc_ref[...]
    uext_ref[pl.ds(0, HALO), :] = jnp.where(first, 0.0, up_ref[...])
    uext_ref[pl.ds(HALO, T_MIX), :] = uc
    uext_ref[pl.ds(HALO + T_MIX, HALO), :] = jnp.where(last, 0.0, un_ref[...])
    pos = pos0 + lax.broadcasted_iota(I32, (T_MIX, 1), 0)
    a_pool = _pool_mixer(uext_ref, uc, pos, seq_len, wpool_ref, bpool_ref, pscale_ref)

    t = ATT_BLOCK
    kext_ref[pl.ds(0, t), :] = kp_ref[...]
    kext_ref[pl.ds(t, T_MIX), :] = kc_ref[...]
    kext_ref[pl.ds(t + T_MIX, t), :] = kn_ref[...]
    vext_ref[pl.ds(0, t), :] = vp_ref[...]
    vext_ref[pl.ds(t, T_MIX), :] = vc_ref[...]
    vext_ref[pl.ds(t + T_MIX, t), :] = vn_ref[...]
    n_sub = T_MIX // t
    assert n_sub >= 2, "a query block is never both first and last in its sequence"
    attn = []
    for sub in range(n_sub):
        variant = 0
        if sub == 0:
            variant = jnp.where(first, 1, variant)
        if sub == n_sub - 1:
            variant = jnp.where(last, 2, variant)
        attn.append(_banded_attention(
            q_ref[pl.ds(sub * t, t), :], kext_ref[pl.ds(sub * t, 3 * t), :], vext_ref[pl.ds(sub * t, 3 * t), :],
            variant, abias_ref, sink_ref))
    a_attn = jnp.concatenate(attn, axis=0)

    a = jnp.concatenate([a_pool, a_attn], axis=1).astype(BF16)
    mix = jnp.dot(a, wout_ref[...], preferred_element_type=F32)

    x = jnp.where(meta_ref[7, i] == 0, xp_ref[...], xs_ref[...])
    x1 = x + gt1 * mix
    h2 = _rms_mod(x1, g2_ref[...], sc2, sh2)
    h2b = h2.astype(BF16)
    planes = _pack_planes(h2)
    for j in range(N_PLANES):
        h2p_ref[j] = planes[j]
    gu = jnp.dot(h2b, wsgu_ref[...], preferred_element_type=F32)
    gate, up = gu[:, :D_SHARED], gu[:, D_SHARED:]
    act = (gate * jax.nn.sigmoid(gate)) * up
    shared = jnp.dot(act.astype(BF16), wsd_ref[...], preferred_element_type=F32)
    xa_ref[...] = x1 + gt2 * shared
    h2lo = (h2 - h2b.astype(F32)).astype(BF16)
    logits = lax.dot_general(wr3_ref[...], jnp.concatenate([h2b, h2b, h2lo], axis=1),
                             (((1,), (1,)), ((), ())), preferred_element_type=F32)
    scores = jax.nn.sigmoid(logits)
    biased = scores + rbias_ref[...]
    routed = [_route(biased[:, c * LANES:(c + 1) * LANES], scores[:, c * LANES:(c + 1) * LANES])
              for c in range(T_MIX // LANES)]
    for c, (idx_rows, w_rows, _, _) in enumerate(routed):
        wsum = w_rows[0]
        for wr in w_rows[1:]:
            wsum = wsum + wr
        wts_t = jnp.concatenate([wr / wsum * ROUTED_SCALE for wr in w_rows], axis=0)
        wpad = jnp.concatenate([wts_t, jnp.zeros((LANES - TOP_K, LANES), F32)], axis=0)
        wts_ref[pl.ds(c * LANES, LANES), :] = wpad.T
        idx_ref[:, pl.ds(c * LANES, LANES)] = jnp.concatenate(idx_rows, axis=0).astype(I32)
    sel_any = jnp.concatenate([r[2] for r in routed], axis=1)
    pref = jnp.dot(sel_any.astype(BF16), tri_ref[...], preferred_element_type=F32)
    before = base_ref[...] + pref[:, :T_MIX]
    for c, (idx_rows, _, _, rowf) in enumerate(routed):
        bc = before[:, c * LANES:(c + 1) * LANES]
        ranks = [jnp.sum(jnp.where(rowf == ik, bc, 0.0), axis=0, keepdims=True) for ik in idx_rows]
        rank_ref[:, pl.ds(c * LANES, LANES)] = jnp.concatenate(ranks, axis=0).astype(I32)
    new_base = base_ref[...] + pref[:, T_MIX:]
    base_ref[...] = new_base
    cnt_ref[...] = new_base


def _mixer(xp, xs, mod3, u, q, k, v, wpool_bf, bpool, pscale, sink, wout_bf, g2, wr3, rbias,
           wsgu_bf, wsd_bf, segments):
    n = _group_tokens(segments)
    t = T_MIX
    nt = n // t
    ab = ATT_BLOCK
    meta = jnp.asarray(_tile_meta(segments, t))
    tri = np.concatenate([np.triu(np.ones((t, t), np.float32), 1), np.ones((t, t), np.float32)], axis=1)
    tri = jnp.asarray(tri, BF16)
    abias = jnp.asarray(_attention_bias())
    cur = lambda w: pl.BlockSpec((t, w), lambda i, m: (i, 0))
    prev = lambda w: pl.BlockSpec((ab, w), lambda i, m: (jnp.maximum(i * (t // ab) - 1, 0), 0))
    nxt = lambda w: pl.BlockSpec((ab, w), lambda i, m: (jnp.minimum((i + 1) * (t // ab), n // ab - 1), 0))
    full = lambda a: pl.BlockSpec(a.shape, lambda i, m: (0,) * a.ndim)
    hb = t // HALO
    in_specs = [
        pl.BlockSpec((t, D_MODEL), lambda i, m: (m[5, i], 0)),
        pl.BlockSpec((t, D_MODEL), lambda i, m: (m[6, i], 0)),
        pl.BlockSpec((1, 1, 6 * D_MODEL), lambda i, m: (m[0, i], 0, 0)),
        pl.BlockSpec((HALO, POOL_WIDTH), lambda i, m: (jnp.maximum(i * hb - 1, 0), 0)),
        cur(POOL_WIDTH),
        pl.BlockSpec((HALO, POOL_WIDTH), lambda i, m: (jnp.minimum((i + 1) * hb, n // HALO - 1), 0)),
        cur(ATTN_WIDTH),
        prev(KV_WIDTH), cur(KV_WIDTH), nxt(KV_WIDTH),
        prev(KV_WIDTH), cur(KV_WIDTH), nxt(KV_WIDTH),
        full(wpool_bf), full(bpool), full(pscale), full(sink), full(wout_bf), full(g2),
        full(wr3), full(rbias), full(wsgu_bf), full(wsd_bf), full(tri), full(abias),
    ]
    out_specs = [
        cur(D_MODEL),
        pl.BlockSpec((N_PLANES, t, LANES), lambda i, m: (0, i, 0)),
        pl.BlockSpec((TOP_K, t), lambda i, m: (0, i)),
        pl.BlockSpec((TOP_K, t), lambda i, m: (0, i)),
        cur(LANES),
        pl.BlockSpec((N_EXPERTS, t), lambda i, m: (0, 0)),
    ]
    out_shape = [
        jax.ShapeDtypeStruct((n, D_MODEL), F32),
        jax.ShapeDtypeStruct((N_PLANES, n, LANES), I32),
        jax.ShapeDtypeStruct((TOP_K, n), I32),
        jax.ShapeDtypeStruct((TOP_K, n), I32),
        jax.ShapeDtypeStruct((n, LANES), F32),
        jax.ShapeDtypeStruct((N_EXPERTS, t), F32),
    ]
    return pl.pallas_call(
        _mixer_kernel,
        grid_spec=pltpu.PrefetchScalarGridSpec(
            num_scalar_prefetch=1, grid=(nt,), in_specs=in_specs, out_specs=out_specs,
            scratch_shapes=[pltpu.VMEM((t + 2 * HALO, POOL_WIDTH), F32),
                            pltpu.VMEM((t + 2 * ab, KV_WIDTH), BF16),
                            pltpu.VMEM((t + 2 * ab, KV_WIDTH), BF16),
                            pltpu.VMEM((N_EXPERTS, t), F32)]),
        out_shape=out_shape,
        compiler_params=pltpu.CompilerParams(dimension_semantics=("arbitrary",),
                                             vmem_limit_bytes=VMEM_LIMIT),
        name="mixer",
    )(meta, xp, xs, mod3, u, u, u, q, k, k, k, v, v, v, wpool_bf, bpool, pscale, sink, wout_bf, g2,
      wr3, rbias, wsgu_bf, wsd_bf, tri, abias)


def _dest_kernel(idx_ref, rank_ref, pstart_ref, o_ref, *, n_rows):
    slab = min(T_DEST_SLAB, idx_ref.shape[1])
    rowi = lax.broadcasted_iota(I32, (N_EXPERTS, slab), 0)
    pstart = pstart_ref[...]
    for c in range(idx_ref.shape[1] // slab):
        lanes = pl.ds(c * slab, slab)
        rows = []
        for kk in range(TOP_K):
            sel = rowi == idx_ref[kk:kk + 1, lanes]
            start = jnp.sum(jnp.where(sel, pstart, 0.0), axis=0, keepdims=True)
            rows.append(start.astype(I32) + rank_ref[kk:kk + 1, lanes])
        dest = jnp.concatenate(rows, axis=0)
        for j in range(N_PLANES):
            o_ref[:, j, lanes] = dest + j * n_rows


def _dest(idx_t, rank_t, pstart_col, n_rows):
    n = idx_t.shape[1]
    td = int(np.gcd(n, T_DEST))
    return pl.pallas_call(
        functools.partial(_dest_kernel, n_rows=n_rows),
        grid=(n // td,),
        in_specs=[pl.BlockSpec((TOP_K, td), lambda i: (0, i)),
                  pl.BlockSpec((TOP_K, td), lambda i: (0, i)),
                  pl.BlockSpec((N_EXPERTS, 1), lambda i: (0, 0))],
        out_specs=pl.BlockSpec((TOP_K, N_PLANES, td), lambda i: (0, 0, i)),
        out_shape=jax.ShapeDtypeStruct((TOP_K, N_PLANES, n), I32),
        name="dest",
    )(idx_t, rank_t, pstart_col)


def _sc_mesh():
    return plsc.VectorSubcoreMesh(core_axis_name="c", subcore_axis_name="s")


def _sc_worker():
    return lax.axis_index("s") * 2 + lax.axis_index("c")


def _sc_scatter_rows(table, didx, n_out_rows):
    m = table.shape[0]
    per_worker = m // SC_ROWS // SC_WORKERS
    assert per_worker % 2 == 0

    @functools.partial(
        pl.kernel, mesh=_sc_mesh(),
        out_type=jax.ShapeDtypeStruct((n_out_rows, LANES), I32),
        scratch_types=[pltpu.VMEM((2, SC_ROWS, LANES), I32), pltpu.VMEM((2, TOP_K, SC_ROWS), I32),
                       pltpu.SemaphoreType.DMA((2,)), pltpu.SemaphoreType.DMA((2,))])
    def run(table_hbm, didx_hbm, out_hbm, rows, idx, lsem, ssem):
        wid = _sc_worker()

        def loads(step, p):
            off = pl.multiple_of((wid * per_worker + step) * SC_ROWS, SC_ROWS)
            return (pltpu.make_async_copy(table_hbm.at[pl.ds(off, SC_ROWS)], rows.at[p], lsem.at[p]),
                    pltpu.make_async_copy(didx_hbm.at[:, pl.ds(off, SC_ROWS)], idx.at[p], lsem.at[p]))

        def scatter(p, kk):
            return pltpu.make_async_copy(rows.at[p], out_hbm.at[idx.at[p].at[kk]], ssem.at[p])

        for cp in loads(0, 0):
            cp.start()

        @pl.loop(0, per_worker, step=2)
        def _(s0):
            for p in range(2):
                step = s0 + p
                for cp in loads(step, p):
                    cp.wait()
                for kk in range(TOP_K):
                    scatter(p, kk).start()

                @pl.when(step > 0)
                def _():
                    for kk in range(TOP_K):
                        scatter(1 - p, kk).wait()

                @pl.when(step + 1 < per_worker)
                def _():
                    for cp in loads(step + 1, 1 - p):
                        cp.start()

        for kk in range(TOP_K):
            scatter(1, kk).wait()

    return run(table, didx)


def _sc_gather_rows(table, didx):
    m = didx.shape[1]
    per_worker = m // SC_ROWS // SC_WORKERS
    assert per_worker % 2 == 0 and (2 * TOP_K) % SC_NBUF == 0
    items = 2 * TOP_K

    @functools.partial(
        pl.kernel, mesh=_sc_mesh(),
        out_type=jax.ShapeDtypeStruct((TOP_K, m, LANES), I32),
        scratch_types=[pltpu.VMEM((SC_NBUF, SC_ROWS, LANES), I32), pltpu.VMEM((2, TOP_K, SC_ROWS), I32),
                       pltpu.SemaphoreType.DMA((SC_NBUF,)), pltpu.SemaphoreType.DMA((SC_NBUF,)),
                       pltpu.SemaphoreType.DMA((2,))])
    def run(table_hbm, didx_hbm, out_hbm, rows, idx, gsem, wsem, isem):
        wid = _sc_worker()

        def off_of(step):
            return pl.multiple_of((wid * per_worker + step) * SC_ROWS, SC_ROWS)

        def idx_copy(step, p):
            return pltpu.make_async_copy(didx_hbm.at[:, pl.ds(off_of(step), SC_ROWS)], idx.at[p], isem.at[p])

        def gather(j):
            r = j % SC_NBUF
            return pltpu.make_async_copy(table_hbm.at[idx.at[j // TOP_K].at[j % TOP_K]], rows.at[r], gsem.at[r])

        def write(s0, j):
            r = j % SC_NBUF
            return pltpu.make_async_copy(rows.at[r], out_hbm.at[j % TOP_K, pl.ds(off_of(s0 + j // TOP_K), SC_ROWS)],
                                         wsem.at[r])

        def retire_write(s0, j):
            if j >= 0:
                write(s0, j).wait()
            else:
                @pl.when(s0 > 0)
                def _():
                    write(s0 - 2, j + items).wait()

        def finish_read(s0, j):
            if j >= 0:
                gather(j).wait()
                write(s0, j).start()
            else:
                @pl.when(s0 > 0)
                def _():
                    gather(j + items).wait()
                    write(s0 - 2, j + items).start()

        idx_copy(0, 0).start()

        @pl.loop(0, per_worker, step=2)
        def _(s0):
            for j in range(items):
                p, kk = j // TOP_K, j % TOP_K
                if kk == 0:
                    idx_copy(s0 + p, p).wait()
                retire_write(s0, j - SC_NBUF)
                gather(j).start()
                finish_read(s0, j - SC_LAG)
                if kk == SC_LAG:
                    @pl.when(s0 + p + 1 < per_worker)
                    def _():
                        idx_copy(s0 + p + 1, 1 - p).start()

        last = per_worker - 2
        for j in range(items - SC_LAG, items):
            gather(j).wait()
            write(last, j).start()
        for j in range(items - SC_NBUF, items):
            write(last, j).wait()

    return run(table, didx)


def _experts_kernel(cstart_ref, nchunk_ref, count_ref, total_ref, x_hbm, wg_ref, wu_ref, wd_ref, y_hbm,
                    wgu_s, wd_s, xbuf, ybuf, xsem, ysem):
    e = pl.program_id(0)
    nb = xbuf.shape[0]
    ch = xbuf.shape[2]
    total = total_ref[0]

    def x_copy(c):
        slot = lax.rem(c, nb)
        rows = pl.ds(pl.multiple_of(c * ch, ch), ch)
        return pltpu.make_async_copy(x_hbm.at[:, rows, :], xbuf.at[slot], xsem.at[slot])

    def y_copy(c):
        slot = lax.rem(c, nb)
        rows = pl.ds(pl.multiple_of(c * ch, ch), ch)
        return pltpu.make_async_copy(ybuf.at[slot], y_hbm.at[:, rows, :], ysem.at[slot])

    @pl.when(e == 0)
    def _():
        for c in range(nb - 1):
            @pl.when(c < total)
            def _():
                x_copy(c).start()

    for j in range(N_PLANES):
        hi = pl.ds(j * LANES, LANES)
        lo = pl.ds(HALF + j * LANES, LANES)
        r_hi = pl.ds(2 * j * LANES, LANES)
        r_lo = pl.ds((2 * j + 1) * LANES, LANES)
        wgu_s[r_hi, pl.ds(0, D_EXPERT)] = wg_ref[0, hi, :].astype(BF16)
        wgu_s[r_lo, pl.ds(0, D_EXPERT)] = wg_ref[0, lo, :].astype(BF16)
        wgu_s[r_hi, pl.ds(D_EXPERT, D_EXPERT)] = wu_ref[0, hi, :].astype(BF16)
        wgu_s[r_lo, pl.ds(D_EXPERT, D_EXPERT)] = wu_ref[0, lo, :].astype(BF16)
    wd_s[...] = wd_ref[0].astype(BF16)

    c0 = cstart_ref[e]
    count = count_ref[e]

    def chunk(i, carry):
        c = c0 + i
        slot = lax.rem(c, nb)

        @pl.when(c + nb - 1 < total)
        def _():
            x_copy(c + nb - 1).start()

        x_copy(c).wait()

        @pl.when(c >= nb)
        def _():
            y_copy(c - nb).wait()

        live = lax.broadcasted_iota(I32, (ch, LANES), 0) < count - i * ch
        xs = []
        for j in range(N_PLANES):
            hi, lo = _unpack_plane(jnp.where(live, xbuf[slot, j], 0))
            xs += [hi.astype(BF16), lo.astype(BF16)]
        acc = jnp.dot(jnp.concatenate(xs, axis=1), wgu_s[...], preferred_element_type=F32)
        gate, up = acc[:, :D_EXPERT], acc[:, D_EXPERT:]
        act = (gate * jax.nn.sigmoid(gate)) * up
        y = jnp.dot(act.astype(BF16), wd_s[...], preferred_element_type=F32)
        planes = _pack_planes(y)
        for j in range(N_PLANES):
            ybuf[slot, j] = planes[j]
        y_copy(c).start()
        return carry

    lax.fori_loop(0, nchunk_ref[e], chunk, 0)

    @pl.when(e == pl.num_programs(0) - 1)
    def _():
        for back in range(nb, 0, -1):
            @pl.when(total >= back)
            def _():
                y_copy(total - back).wait()


def _experts(xs_planes, w_gate, w_up, w_down, cstart, nchunk, counts, total):
    n_rows = xs_planes.shape[1]
    ch = EXPERT_CHUNK
    w_map = lambda e, *_: (e, 0, 0)
    any_space = pl.BlockSpec(memory_space=pl.ANY)
    return pl.pallas_call(
        _experts_kernel,
        grid_spec=pltpu.PrefetchScalarGridSpec(
            num_scalar_prefetch=4, grid=(N_EXPERTS,),
            in_specs=[any_space,
                      pl.BlockSpec((1, D_MODEL, D_EXPERT), w_map),
                      pl.BlockSpec((1, D_MODEL, D_EXPERT), w_map),
                      pl.BlockSpec((1, D_EXPERT, D_MODEL), w_map)],
            out_specs=any_space,
            scratch_shapes=[pltpu.VMEM((D_MODEL, 2 * D_EXPERT), BF16),
                            pltpu.VMEM((D_EXPERT, D_MODEL), BF16),
                            pltpu.VMEM((EXPERT_NBUF, N_PLANES, ch, LANES), I32),
                            pltpu.VMEM((EXPERT_NBUF, N_PLANES, ch, LANES), I32),
                            pltpu.SemaphoreType.DMA((EXPERT_NBUF,)),
                            pltpu.SemaphoreType.DMA((EXPERT_NBUF,))]),
        out_shape=jax.ShapeDtypeStruct((N_PLANES, n_rows, LANES), I32),
        compiler_params=pltpu.CompilerParams(dimension_semantics=("arbitrary",),
                                             vmem_limit_bytes=VMEM_LIMIT),
        name="experts",
    )(cstart, nchunk, counts, total, xs_planes, w_gate, w_up, w_down)


def _final_kernel(xa_ref, yk_ref, wts_ref, mod_ref, g_ref, o_ref):
    wts = wts_ref[...]
    his = [jnp.zeros((T_FIN, LANES), F32) for _ in range(N_PLANES)]
    los = [jnp.zeros((T_FIN, LANES), F32) for _ in range(N_PLANES)]
    for kk in range(TOP_K):
        wk = wts[:, kk:kk + 1]
        for j in range(N_PLANES):
            hi, lo = _unpack_plane(yk_ref[kk, j])
            his[j] = his[j] + wk * hi
            los[j] = los[j] + wk * lo
    routed = jnp.concatenate(his + los, axis=1)
    gt2 = mod_ref[0][:, 5 * D_MODEL:6 * D_MODEL]
    x = xa_ref[...] + gt2 * routed
    ms = jnp.mean(x * x, axis=-1, keepdims=True)
    o_ref[...] = x * lax.rsqrt(ms + EPS) * g_ref[...]


def _final_kernel_into(xa_ref, yk_ref, wts_ref, mod_ref, g_ref, prev_ref, o_ref):
    _final_kernel(xa_ref, yk_ref, wts_ref, mod_ref, g_ref, o_ref)


def _final(xa, yk, wts, mod3, g_final, tile_off, n_tok, seq_len, sid0, out_rows, out_tile_off, prev):
    tiles_per_seq = seq_len // T_FIN
    in_specs = [pl.BlockSpec((T_FIN, D_MODEL), lambda i: (i + tile_off, 0)),
                pl.BlockSpec((TOP_K, N_PLANES, T_FIN, LANES), lambda i: (0, 0, i + tile_off, 0)),
                pl.BlockSpec((T_FIN, LANES), lambda i: (i + tile_off, 0)),
                pl.BlockSpec((1, 1, 6 * D_MODEL), lambda i: (sid0 + i // tiles_per_seq, 0, 0)),
                pl.BlockSpec((1, D_MODEL), lambda i: (0, 0))]
    args = [xa, yk, wts, mod3, g_final]
    if prev is not None:
        in_specs.append(pl.BlockSpec(memory_space=pl.ANY))
        args.append(prev)
    return pl.pallas_call(
        _final_kernel if prev is None else _final_kernel_into,
        grid=(n_tok // T_FIN,),
        in_specs=in_specs,
        out_specs=pl.BlockSpec((T_FIN, D_MODEL), lambda i: (i + out_tile_off, 0)),
        out_shape=jax.ShapeDtypeStruct((out_rows, D_MODEL), F32),
        input_output_aliases={} if prev is None else {len(args) - 1: 0},
        compiler_params=pltpu.CompilerParams(dimension_semantics=("arbitrary",),
                                             vmem_limit_bytes=VMEM_LIMIT),
        name="final",
    )(*args)


def kernel(x_prompt, x_sample, c_prompt, c_sample, w_ada, b_ada, g_norm1, w_in, w_pool, b_pool, pool_scale, attn_sink, w_out, g_norm2, w_router, router_bias, w_gate, w_up, w_down, ws_gate, ws_up, ws_down, g_final):
    assert w_ada.shape[0] == 1, "one layer"
    bp, sp, d = x_prompt.shape
    bs, ss, _ = x_sample.shape
    assert d == D_MODEL and bp + bs <= ADA_ROWS
    n_p, n_s = bp * sp, bs * ss
    xp = x_prompt.reshape(n_p, d)
    xs = x_sample.reshape(n_s, d)

    c_all = jnp.concatenate([c_prompt, c_sample, jnp.zeros((ADA_ROWS - bp - bs, d), F32)], axis=0)
    mod3 = _ada(c_all, w_ada[0], b_ada[0]).reshape(ADA_ROWS, 1, 6 * d)

    q0, q1 = POOL_WIDTH, POOL_WIDTH + ATTN_WIDTH
    wq = w_in[0][:, q0:q1].reshape(d, N_KV_HEADS, Q_PER_KV, HEAD_DIM).transpose(0, 2, 1, 3).reshape(d, ATTN_WIDTH)
    w_in_p = jnp.concatenate([w_in[0][:, :q0], wq, w_in[0][:, q1:]], axis=1).astype(BF16)
    wo = w_out[0][q0:].reshape(N_KV_HEADS, Q_PER_KV, HEAD_DIM, d).transpose(1, 0, 2, 3).reshape(ATTN_WIDTH, d)
    w_out_p = jnp.concatenate([w_out[0][:q0], wo], axis=0).astype(BF16)
    wsgu = jnp.concatenate([ws_gate[0], ws_up[0]], axis=1).astype(BF16)
    wr_hi = w_router[0].astype(BF16)
    wr_lo = (w_router[0] - wr_hi.astype(F32)).astype(BF16)
    wr3 = jnp.concatenate([wr_hi, wr_lo, wr_hi], axis=0).T
    g1 = g_norm1[0].reshape(1, d)
    mixer_weights = (
        w_pool[0].astype(BF16), b_pool[0].reshape(1, POOL_WIDTH), pool_scale[0].reshape(1, POOL_WIDTH),
        attn_sink[0].reshape(1, N_HEADS), w_out_p, g_norm2[0].reshape(1, d),
        wr3, router_bias[0].reshape(N_EXPERTS, 1), wsgu, ws_down[0].astype(BF16))
    ch = EXPERT_CHUNK

    segments = ((0, 0, bp, sp, 0), (1, 0, bs, ss, bp))
    n = _group_tokens(segments)
    u, q, k, v = _inproj(xp, xs, mod3, g1, w_in_p, segments)
    xa, h2p, idx_t, rank_t, wts, cnt = _mixer(xp, xs, mod3, u, q, k, v, *mixer_weights, segments)
    n_rows = (-(-n * TOP_K // ch) + N_EXPERTS) * ch
    counts = cnt[:, 0].astype(I32)
    padded = (counts + ch - 1) // ch * ch
    pend = jnp.cumsum(padded)
    pstart = pend - padded
    didx = _dest(idx_t, rank_t, pstart.astype(F32).reshape(N_EXPERTS, 1), n_rows)
    didx = didx.reshape(TOP_K, N_PLANES * n)
    xs_rows = _sc_scatter_rows(h2p.reshape(N_PLANES * n, LANES), didx, N_PLANES * n_rows)
    ys_rows = _experts(xs_rows.reshape(N_PLANES, n_rows, LANES), w_gate[0], w_up[0], w_down[0],
                       pstart // ch, padded // ch, counts, pend[-1:] // ch)
    yk = _sc_gather_rows(ys_rows.reshape(N_PLANES * n_rows, LANES), didx).reshape(TOP_K, N_PLANES, n, LANES)

    gf = g_final.reshape(1, d)
    outs = [None, None]
    out_rows = (n_p, n_s)
    tok = 0
    for src, tok0, n_seqs, seq_len, sid0 in segments:
        n_tok = n_seqs * seq_len
        outs[src] = _final(xa, yk, wts, mod3, gf, tok // T_FIN, n_tok, seq_len, sid0,
                           out_rows[src], tok0 // T_FIN, outs[src])
        tok += n_tok
    return outs[0].reshape(bp, sp, d), outs[1].reshape(bs, ss, d)
```

```python
import functools

import numpy as np
import jax
import jax.numpy as jnp
from jax import lax
from jax.experimental import pallas as pl
from jax.experimental.pallas import tpu as pltpu
from jax.experimental.pallas import tpu_sc as plsc

F32 = jnp.float32
BF16 = jnp.bfloat16
I32 = jnp.int32

D_MODEL = 1024
POOL_WINDOWS = (2, 4, 8, 16)
POOL_WIDTH = 512
POOL_GROUP = 128
N_HEADS = 8
N_KV_HEADS = 2
HEAD_DIM = 64
Q_PER_KV = N_HEADS // N_KV_HEADS
ATTN_WIDTH = N_HEADS * HEAD_DIM
KV_WIDTH = N_KV_HEADS * HEAD_DIM
D_IN_PROJ = POOL_WIDTH + ATTN_WIDTH + 2 * KV_WIDTH
WINDOW = 128
N_EXPERTS = 256
TOP_K = 8
N_EXPERT_GROUPS = 8
GROUP_SIZE = N_EXPERTS // N_EXPERT_GROUPS
TOPK_GROUPS = 4
D_EXPERT = 256
D_SHARED = 256
ROUTED_SCALE = 2.5
EPS = 1e-6
NEG_INF = -1e30

LANES = 128
HALO = 8
N_PLANES = 4
HALF = D_MODEL // 2

T_IN = 512
T_MIX = 512
ATT_BLOCK = 128
T_DEST = 2048
T_DEST_SLAB = 512
EXPERT_CHUNK = 512
EXPERT_NBUF = 4
T_FIN = 256
SC_ROWS = 128
SC_WORKERS = 32
SC_NBUF = 4
SC_LAG = 2
ADA_ROWS = 16
VMEM_LIMIT = 48 * 1024 * 1024

LOG2E = 1.4426950408889634
ALIBI_SLOPES = tuple(float(2.0 ** (-8.0 * (h + 1) / N_HEADS)) for h in range(N_HEADS))


def _pack_planes(y):
    planes = []
    for j in range(N_PLANES):
        hi = y[:, j * LANES:(j + 1) * LANES].astype(BF16).astype(F32)
        lo = y[:, HALF + j * LANES:HALF + (j + 1) * LANES].astype(BF16).astype(F32)
        hb = lax.bitcast_convert_type(hi, jnp.uint32) & jnp.uint32(0xFFFF0000)
        lb = lax.bitcast_convert_type(lo, jnp.uint32) >> jnp.uint32(16)
        planes.append(lax.bitcast_convert_type(hb | lb, I32))
    return planes


def _unpack_plane(w):
    u = lax.bitcast_convert_type(w, jnp.uint32)
    hi = lax.bitcast_convert_type(u & jnp.uint32(0xFFFF0000), F32)
    lo = lax.bitcast_convert_type(u << jnp.uint32(16), F32)
    return hi, lo


def _ada_kernel(c_ref, w_ref, b_ref, o_ref):
    c = c_ref[...]
    s = c * jax.nn.sigmoid(c)
    o_ref[...] = jnp.dot(s, w_ref[...], precision=lax.Precision.HIGHEST,
                         preferred_element_type=F32) + b_ref[...]


def _ada(c_all, w_ada, b_ada):
    n_out = w_ada.shape[1]
    tn = 1024
    return pl.pallas_call(
        _ada_kernel,
        grid=(n_out // tn,),
        in_specs=[pl.BlockSpec((ADA_ROWS, D_MODEL), lambda j: (0, 0)),
                  pl.BlockSpec((D_MODEL, tn), lambda j: (0, j)),
                  pl.BlockSpec((1, tn), lambda j: (0, j))],
        out_specs=pl.BlockSpec((ADA_ROWS, tn), lambda j: (0, j)),
        out_shape=jax.ShapeDtypeStruct((ADA_ROWS, n_out), F32),
        name="ada",
    )(c_all, w_ada, b_ada.reshape(1, n_out))


def _rms_mod(x, g, scale, shift):
    ms = jnp.mean(x * x, axis=-1, keepdims=True)
    return (x * lax.rsqrt(ms + EPS) * g) * (1.0 + scale) + shift


def _inproj_kernel(meta_ref, xp_ref, xs_ref, mod_ref, g_ref, w_ref, u_ref, q_ref, k_ref, v_ref):
    x = jnp.where(meta_ref[7, pl.program_id(0)] == 0, xp_ref[...], xs_ref[...])
    mod = mod_ref[0]
    h = _rms_mod(x, g_ref[...], mod[:, D_MODEL:2 * D_MODEL], mod[:, 0:D_MODEL])
    z = jnp.dot(h.astype(BF16), w_ref[...], preferred_element_type=F32)
    u_ref[...] = z[:, :POOL_WIDTH]
    q_ref[...] = (z[:, POOL_WIDTH:POOL_WIDTH + ATTN_WIDTH] * (LOG2E * HEAD_DIM ** -0.5)).astype(BF16)
    k_ref[...] = z[:, POOL_WIDTH + ATTN_WIDTH:POOL_WIDTH + ATTN_WIDTH + KV_WIDTH].astype(BF16)
    v_ref[...] = z[:, POOL_WIDTH + ATTN_WIDTH + KV_WIDTH:].astype(BF16)


def _tile_meta(segments, tile):
    rows = []
    for src, tok0, n_seqs, seq_len, sid0 in segments:
        per = seq_len // tile
        for q in range(n_seqs):
            for t in range(per):
                blk = (tok0 + q * seq_len) // tile + t
                rows.append((sid0 + q, int(t == 0), int(t == per - 1), t * tile, seq_len,
                             blk if src == 0 else -1, blk if src == 1 else -1, src))
    meta = np.asarray(rows, np.int32)
    for col in (5, 6):
        known = meta[:, col] >= 0
        if not known.any():
            meta[:, col] = 0
            continue
        last = np.maximum.accumulate(np.where(known, np.arange(len(meta)), -1))
        last = np.where(last < 0, np.argmax(known), last)
        meta[:, col] = meta[last, col]
    return meta.T.copy()


def _group_tokens(segments):
    return sum(n_seqs * seq_len for _, _, n_seqs, seq_len, _ in segments)


def _inproj(xp, xs, mod3, g1, w_in_bf, segments):
    n = _group_tokens(segments)
    meta = jnp.asarray(_tile_meta(segments, T_IN))
    tok = lambda w: pl.BlockSpec((T_IN, w), lambda i, m: (i, 0))
    return pl.pallas_call(
        _inproj_kernel,
        grid_spec=pltpu.PrefetchScalarGridSpec(
            num_scalar_prefetch=1,
            grid=(n // T_IN,),
            in_specs=[
                pl.BlockSpec((T_IN, D_MODEL), lambda i, m: (m[5, i], 0)),
                pl.BlockSpec((T_IN, D_MODEL), lambda i, m: (m[6, i], 0)),
                pl.BlockSpec((1, 1, 6 * D_MODEL), lambda i, m: (m[0, i], 0, 0)),
                pl.BlockSpec((1, D_MODEL), lambda i, m: (0, 0)),
                pl.BlockSpec((D_MODEL, D_IN_PROJ), lambda i, m: (0, 0)),
            ],
            out_specs=[tok(POOL_WIDTH), tok(ATTN_WIDTH), tok(KV_WIDTH), tok(KV_WIDTH)],
        ),
        out_shape=[jax.ShapeDtypeStruct((n, POOL_WIDTH), F32),
                   jax.ShapeDtypeStruct((n, ATTN_WIDTH), BF16),
                   jax.ShapeDtypeStruct((n, KV_WIDTH), BF16),
                   jax.ShapeDtypeStruct((n, KV_WIDTH), BF16)],
        compiler_params=pltpu.CompilerParams(dimension_semantics=("arbitrary",),
                                             vmem_limit_bytes=VMEM_LIMIT),
        name="inproj",
    )(meta, xp, xs, mod3, g1, w_in_bf)


def _pool_mixer(uext_ref, uc, pos, seq_len, wpool_ref, bpool_ref, pscale_ref):
    outs = []
    for gi, w in enumerate(POOL_WINDOWS):
        c0 = gi * POOL_GROUP
        half = w // 2
        acc = uext_ref[pl.ds(HALO - half, T_MIX), pl.ds(c0, POOL_GROUP)]
        for o in range(-half + 1, half):
            acc = acc + uext_ref[pl.ds(HALO + o, T_MIX), pl.ds(c0, POOL_GROUP)]
        lo = jnp.maximum(pos - half, 0)
        hi = jnp.minimum(pos + half, seq_len)
        cnt = (hi - lo).astype(F32)
        d = acc / cnt - uc[:, c0:c0 + POOL_GROUP]
        y = jnp.dot(d.astype(BF16), wpool_ref[gi], preferred_element_type=F32)
        y = (y + bpool_ref[:, c0:c0 + POOL_GROUP]) * pscale_ref[:, c0:c0 + POOL_GROUP]
        outs.append(y)
    return jnp.concatenate(outs, axis=1)


def _attention_bias():
    t = ATT_BLOCK
    r = np.arange(t)[:, None]
    c = np.arange(3 * t)[None, :]
    dist = np.abs(r - (c - t))
    band = dist <= WINDOW
    out = np.empty((3, N_HEADS, t, 3 * t), np.float32)
    for var, valid in enumerate((band, band & (c >= t), band & (c < 2 * t))):
        for h in range(N_HEADS):
            out[var, h] = np.where(valid, -ALIBI_SLOPES[h] * LOG2E * dist, NEG_INF)
    return out.reshape(3, N_HEADS * t, 3 * t)


def _attention_scores(q, kw):
    t = ATT_BLOCK
    low = lax.broadcasted_iota(I32, (t, LANES), 1) < HEAD_DIM
    zero = jnp.zeros((t, LANES), BF16)
    qs = []
    for h in range(N_HEADS):
        blk = q[:, (h % Q_PER_KV) * LANES:(h % Q_PER_KV + 1) * LANES]
        qs.append(jnp.where(low, blk, zero) if h < Q_PER_KV else jnp.where(low, zero, blk))
    return lax.dot_general(jnp.concatenate(qs, axis=0), kw, (((1,), (1,)), ((), ())),
                           preferred_element_type=F32)


def _attention_probs(s, variant, bias_ref, sink_ref):
    t = ATT_BLOCK
    sh = s + bias_ref[variant]
    sink = jnp.concatenate([jnp.broadcast_to(sink_ref[:, h:h + 1] * LOG2E, (t, 1)) for h in range(N_HEADS)], axis=0)
    m = jnp.maximum(jnp.max(sh, axis=-1, keepdims=True), sink)
    p = jnp.exp2(sh - m)
    den = jnp.sum(p, axis=-1, keepdims=True) + jnp.exp2(sink - m)
    return p.astype(BF16), den


def _attention_out(p, den, vw):
    t = ATT_BLOCK
    low = lax.broadcasted_iota(I32, (t, LANES), 1) < HEAD_DIM
    o = jnp.dot(p, vw, preferred_element_type=F32) / den
    return jnp.concatenate(
        [jnp.where(low, o[cb * t:(cb + 1) * t], o[(Q_PER_KV + cb) * t:(Q_PER_KV + cb + 1) * t])
         for cb in range(Q_PER_KV)], axis=1)


def _route(biased, scores):
    t = biased.shape[1]
    rowf = lax.broadcasted_iota(I32, (N_EXPERTS, t), 0).astype(F32)
    ninf = float("-inf")
    gs = []
    for g in range(N_EXPERT_GROUPS):
        blk = biased[g * GROUP_SIZE:(g + 1) * GROUP_SIZE, :]
        rf = rowf[g * GROUP_SIZE:(g + 1) * GROUP_SIZE, :]
        m1 = jnp.max(blk, axis=0, keepdims=True)
        i1 = jnp.min(jnp.where(blk == m1, rf, float(N_EXPERTS)), axis=0, keepdims=True)
        m2 = jnp.max(jnp.where(rf == i1, ninf, blk), axis=0, keepdims=True)
        gs.append(m1 + m2)
    keep = []
    for g in range(N_EXPERT_GROUPS):
        beat = jnp.zeros((1, t), F32)
        for g2 in range(N_EXPERT_GROUPS):
            if g2 == g:
                continue
            better = (gs[g2] >= gs[g]) if g2 < g else (gs[g2] > gs[g])
            beat = beat + better.astype(F32)
        keep.append(jnp.broadcast_to(beat < float(TOPK_GROUPS), (GROUP_SIZE, t)))
    emask = jnp.concatenate(keep, axis=0)
    masked = jnp.where(emask, biased, NEG_INF)
    idx_rows, w_rows = [], []
    sel_any = jnp.zeros((N_EXPERTS, t), F32)
    for _ in range(TOP_K):
        m = jnp.max(masked, axis=0, keepdims=True)
        ik = jnp.min(jnp.where(masked == m, rowf, float(N_EXPERTS)), axis=0, keepdims=True)
        sel = rowf == ik
        w_rows.append(jnp.sum(jnp.where(sel, scores, 0.0), axis=0, keepdims=True))
        idx_rows.append(ik)
        masked = jnp.where(sel, ninf, masked)
        sel_any = sel_any + sel.astype(F32)
    return idx_rows, w_rows, sel_any, rowf


def _mixer_kernel(meta_ref, xp_ref, xs_ref, mod_ref, up_ref, uc_ref, un_ref, q_ref,
                  kp_ref, kc_ref, kn_ref, vp_ref, vc_ref, vn_ref,
                  wpool_ref, bpool_ref, pscale_ref, sink_ref, wout_ref, g2_ref,
                  wr3_ref, rbias_ref, wsgu_ref, wsd_ref, tri_ref, abias_ref,
                  xa_ref, h2p_ref, idx_ref, rank_ref, wts_ref, cnt_ref,
                  uext_ref, kext_ref, vext_ref, base_ref):
    i = pl.program_id(0)
    first = meta_ref[1, i] == 1
    last = meta_ref[2, i] == 1
    pos0 = meta_ref[3, i]
    seq_len = meta_ref[4, i]
    mod = mod_ref[0]
    gt1 = mod[:, 2 * D_MODEL:3 * D_MODEL]
    sh2 = mod[:, 3 * D_MODEL:4 * D_MODEL]
    sc2 = mod[:, 4 * D_MODEL:5 * D_MODEL]
    gt2 = mod[:, 5 * D_MODEL:6 * D_MODEL]

    @pl.when(i == 0)
    def _():
        base_ref[...] = jnp.zeros_like(base_ref)

    uc = uc_ref[...]
    uext_ref[pl.ds(0, HALO), :] = jnp.where(first, 0.0, up_ref[...])
    uext_ref[pl.ds(HALO, T_MIX), :] = uc
    uext_ref[pl.ds(HALO + T_MIX, HALO), :] = jnp.where(last, 0.0, un_ref[...])
    pos = pos0 + lax.broadcasted_iota(I32, (T_MIX, 1), 0)
    a_pool = _pool_mixer(uext_ref, uc, pos, seq_len, wpool_ref, bpool_ref, pscale_ref)

    t = ATT_BLOCK
    kext_ref[pl.ds(0, t), :] = kp_ref[...]
    kext_ref[pl.ds(t, T_MIX), :] = kc_ref[...]
    kext_ref[pl.ds(t + T_MIX, t), :] = kn_ref[...]
    vext_ref[pl.ds(0, t), :] = vp_ref[...]
    vext_ref[pl.ds(t, T_MIX), :] = vc_ref[...]
    vext_ref[pl.ds(t + T_MIX, t), :] = vn_ref[...]
    n_sub = T_MIX // t
    assert n_sub >= 2, "a query block is never both first and last in its sequence"
    variants = []
    for sub in range(n_sub):
        variant = 0
        if sub == 0:
            variant = jnp.where(first, 1, variant)
        if sub == n_sub - 1:
            variant = jnp.where(last, 2, variant)
        variants.append(variant)
    scores = [_attention_scores(q_ref[pl.ds(sub * t, t), :], kext_ref[pl.ds(sub * t, 3 * t), :])
              for sub in range(n_sub)]
    probs = [_attention_probs(scores[sub], variants[sub], abias_ref, sink_ref) for sub in range(n_sub)]
    attn = [_attention_out(probs[sub][0], probs[sub][1], vext_ref[pl.ds(sub * t, 3 * t), :])
            for sub in range(n_sub)]
    a_attn = jnp.concatenate(attn, axis=0)

    a = jnp.concatenate([a_pool, a_attn], axis=1).astype(BF16)
    mix = jnp.dot(a, wout_ref[...], preferred_element_type=F32)

    x = jnp.where(meta_ref[7, i] == 0, xp_ref[...], xs_ref[...])
    x1 = x + gt1 * mix
    h2 = _rms_mod(x1, g2_ref[...], sc2, sh2)
    h2b = h2.astype(BF16)
    planes = _pack_planes(h2)
    for j in range(N_PLANES):
        h2p_ref[j] = planes[j]
    gu = jnp.dot(h2b, wsgu_ref[...], preferred_element_type=F32)
    gate, up = gu[:, :D_SHARED], gu[:, D_SHARED:]
    act = (gate * jax.nn.sigmoid(gate)) * up
    shared = jnp.dot(act.astype(BF16), wsd_ref[...], preferred_element_type=F32)
    xa_ref[...] = x1 + gt2 * shared
    h2lo = (h2 - h2b.astype(F32)).astype(BF16)
    logits = lax.dot_general(wr3_ref[...], jnp.concatenate([h2b, h2b, h2lo], axis=1),
                             (((1,), (1,)), ((), ())), preferred_element_type=F32)
    scores = jax.nn.sigmoid(logits)
    biased = scores + rbias_ref[...]
    routed = [_route(biased[:, c * LANES:(c + 1) * LANES], scores[:, c * LANES:(c + 1) * LANES])
              for c in range(T_MIX // LANES)]
    for c, (idx_rows, w_rows, _, _) in enumerate(routed):
        wsum = w_rows[0]
        for wr in w_rows[1:]:
            wsum = wsum + wr
        wts_t = jnp.concatenate([wr / wsum * ROUTED_SCALE for wr in w_rows], axis=0)
        wpad = jnp.concatenate([wts_t, jnp.zeros((LANES - TOP_K, LANES), F32)], axis=0)
        wts_ref[pl.ds(c * LANES, LANES), :] = wpad.T
        idx_ref[:, pl.ds(c * LANES, LANES)] = jnp.concatenate(idx_rows, axis=0).astype(I32)
    sel_any = jnp.concatenate([r[2] for r in routed], axis=1)
    pref = jnp.dot(sel_any.astype(BF16), tri_ref[...], preferred_element_type=F32)
    before = base_ref[...] + pref[:, :T_MIX]
    for c, (idx_rows, _, _, rowf) in enumerate(routed):
        bc = before[:, c * LANES:(c + 1) * LANES]
        ranks = [jnp.sum(jnp.where(rowf == ik, bc, 0.0), axis=0, keepdims=True) for ik in idx_rows]
        rank_ref[:, pl.ds(c * LANES, LANES)] = jnp.concatenate(ranks, axis=0).astype(I32)
    new_base = base_ref[...] + pref[:, T_MIX:]
    base_ref[...] = new_base
    cnt_ref[...] = new_base


def _mixer(xp, xs, mod3, u, q, k, v, wpool_bf, bpool, pscale, sink, wout_bf, g2, wr3, rbias,
           wsgu_bf, wsd_bf, segments):
    n = _group_tokens(segments)
    t = T_MIX
    nt = n // t
    ab = ATT_BLOCK
    meta = jnp.asarray(_tile_meta(segments, t))
    tri = np.concatenate([np.triu(np.ones((t, t), np.float32), 1), np.ones((t, t), np.float32)], axis=1)
    tri = jnp.asarray(tri, BF16)
    abias = jnp.asarray(_attention_bias())
    cur = lambda w: pl.BlockSpec((t, w), lambda i, m: (i, 0))
    prev = lambda w: pl.BlockSpec((ab, w), lambda i, m: (jnp.maximum(i * (t // ab) - 1, 0), 0))
    nxt = lambda w: pl.BlockSpec((ab, w), lambda i, m: (jnp.minimum((i + 1) * (t // ab), n // ab - 1), 0))
    full = lambda a: pl.BlockSpec(a.shape, lambda i, m: (0,) * a.ndim)
    hb = t // HALO
    in_specs = [
        pl.BlockSpec((t, D_MODEL), lambda i, m: (m[5, i], 0)),
        pl.BlockSpec((t, D_MODEL), lambda i, m: (m[6, i], 0)),
        pl.BlockSpec((1, 1, 6 * D_MODEL), lambda i, m: (m[0, i], 0, 0)),
        pl.BlockSpec((HALO, POOL_WIDTH), lambda i, m: (jnp.maximum(i * hb - 1, 0), 0)),
        cur(POOL_WIDTH),
        pl.BlockSpec((HALO, POOL_WIDTH), lambda i, m: (jnp.minimum((i + 1) * hb, n // HALO - 1), 0)),
        cur(ATTN_WIDTH),
        prev(KV_WIDTH), cur(KV_WIDTH), nxt(KV_WIDTH),
        prev(KV_WIDTH), cur(KV_WIDTH), nxt(KV_WIDTH),
        full(wpool_bf), full(bpool), full(pscale), full(sink), full(wout_bf), full(g2),
        full(wr3), full(rbias), full(wsgu_bf), full(wsd_bf), full(tri), full(abias),
    ]
    out_specs = [
        cur(D_MODEL),
        pl.BlockSpec((N_PLANES, t, LANES), lambda i, m: (0, i, 0)),
        pl.BlockSpec((TOP_K, t), lambda i, m: (0, i)),
        pl.BlockSpec((TOP_K, t), lambda i, m: (0, i)),
        cur(LANES),
        pl.BlockSpec((N_EXPERTS, t), lambda i, m: (0, 0)),
    ]
    out_shape = [
        jax.ShapeDtypeStruct((n, D_MODEL), F32),
        jax.ShapeDtypeStruct((N_PLANES, n, LANES), I32),
        jax.ShapeDtypeStruct((TOP_K, n), I32),
        jax.ShapeDtypeStruct((TOP_K, n), I32),
        jax.ShapeDtypeStruct((n, LANES), F32),
        jax.ShapeDtypeStruct((N_EXPERTS, t), F32),
    ]
    return pl.pallas_call(
        _mixer_kernel,
        grid_spec=pltpu.PrefetchScalarGridSpec(
            num_scalar_prefetch=1, grid=(nt,), in_specs=in_specs, out_specs=out_specs,
            scratch_shapes=[pltpu.VMEM((t + 2 * HALO, POOL_WIDTH), F32),
                            pltpu.VMEM((t + 2 * ab, KV_WIDTH), BF16),
                            pltpu.VMEM((t + 2 * ab, KV_WIDTH), BF16),
                            pltpu.VMEM((N_EXPERTS, t), F32)]),
        out_shape=out_shape,
        compiler_params=pltpu.CompilerParams(dimension_semantics=("arbitrary",),
                                             vmem_limit_bytes=VMEM_LIMIT),
        name="mixer",
    )(meta, xp, xs, mod3, u, u, u, q, k, k, k, v, v, v, wpool_bf, bpool, pscale, sink, wout_bf, g2,
      wr3, rbias, wsgu_bf, wsd_bf, tri, abias)


def _dest_kernel(idx_ref, rank_ref, pstart_ref, o_ref, *, n_rows):
    slab = min(T_DEST_SLAB, idx_ref.shape[1])
    rowi = lax.broadcasted_iota(I32, (N_EXPERTS, slab), 0)
    pstart = pstart_ref[...]
    for c in range(idx_ref.shape[1] // slab):
        lanes = pl.ds(c * slab, slab)
        rows = []
        for kk in range(TOP_K):
            sel = rowi == idx_ref[kk:kk + 1, lanes]
            start = jnp.sum(jnp.where(sel, pstart, 0.0), axis=0, keepdims=True)
            rows.append(start.astype(I32) + rank_ref[kk:kk + 1, lanes])
        dest = jnp.concatenate(rows, axis=0)
        for j in range(N_PLANES):
            o_ref[:, j, lanes] = dest + j * n_rows


def _dest(idx_t, rank_t, pstart_col, n_rows):
    n = idx_t.shape[1]
    td = int(np.gcd(n, T_DEST))
    return pl.pallas_call(
        functools.partial(_dest_kernel, n_rows=n_rows),
        grid=(n // td,),
        in_specs=[pl.BlockSpec((TOP_K, td), lambda i: (0, i)),
                  pl.BlockSpec((TOP_K, td), lambda i: (0, i)),
                  pl.BlockSpec((N_EXPERTS, 1), lambda i: (0, 0))],
        out_specs=pl.BlockSpec((TOP_K, N_PLANES, td), lambda i: (0, 0, i)),
        out_shape=jax.ShapeDtypeStruct((TOP_K, N_PLANES, n), I32),
        name="dest",
    )(idx_t, rank_t, pstart_col)


def _sc_mesh():
    return plsc.VectorSubcoreMesh(core_axis_name="c", subcore_axis_name="s")


def _sc_worker():
    return lax.axis_index("s") * 2 + lax.axis_index("c")


def _sc_scatter_rows(table, didx, n_out_rows):
    m = table.shape[0]
    per_worker = m // SC_ROWS // SC_WORKERS
    assert per_worker % 2 == 0

    @functools.partial(
        pl.kernel, mesh=_sc_mesh(),
        out_type=jax.ShapeDtypeStruct((n_out_rows, LANES), I32),
        scratch_types=[pltpu.VMEM((2, SC_ROWS, LANES), I32), pltpu.VMEM((2, TOP_K, SC_ROWS), I32),
                       pltpu.SemaphoreType.DMA((2,)), pltpu.SemaphoreType.DMA((2,))])
    def run(table_hbm, didx_hbm, out_hbm, rows, idx, lsem, ssem):
        wid = _sc_worker()

        def loads(step, p):
            off = pl.multiple_of((wid * per_worker + step) * SC_ROWS, SC_ROWS)
            return (pltpu.make_async_copy(table_hbm.at[pl.ds(off, SC_ROWS)], rows.at[p], lsem.at[p]),
                    pltpu.make_async_copy(didx_hbm.at[:, pl.ds(off, SC_ROWS)], idx.at[p], lsem.at[p]))

        def scatter(p, kk):
            return pltpu.make_async_copy(rows.at[p], out_hbm.at[idx.at[p].at[kk]], ssem.at[p])

        for cp in loads(0, 0):
            cp.start()

        @pl.loop(0, per_worker, step=2)
        def _(s0):
            for p in range(2):
                step = s0 + p
                for cp in loads(step, p):
                    cp.wait()
                for kk in range(TOP_K):
                    scatter(p, kk).start()

                @pl.when(step > 0)
                def _():
                    for kk in range(TOP_K):
                        scatter(1 - p, kk).wait()

                @pl.when(step + 1 < per_worker)
                def _():
                    for cp in loads(step + 1, 1 - p):
                        cp.start()

        for kk in range(TOP_K):
            scatter(1, kk).wait()

    return run(table, didx)


def _sc_gather_rows(table, didx):
    m = didx.shape[1]
    per_worker = m // SC_ROWS // SC_WORKERS
    assert per_worker % 2 == 0 and (2 * TOP_K) % SC_NBUF == 0
    items = 2 * TOP_K

    @functools.partial(
        pl.kernel, mesh=_sc_mesh(),
        out_type=jax.ShapeDtypeStruct((TOP_K, m, LANES), I32),
        scratch_types=[pltpu.VMEM((SC_NBUF, SC_ROWS, LANES), I32), pltpu.VMEM((2, TOP_K, SC_ROWS), I32),
                       pltpu.SemaphoreType.DMA((SC_NBUF,)), pltpu.SemaphoreType.DMA((SC_NBUF,)),
                       pltpu.SemaphoreType.DMA((2,))])
    def run(table_hbm, didx_hbm, out_hbm, rows, idx, gsem, wsem, isem):
        wid = _sc_worker()

        def off_of(step):
            return pl.multiple_of((wid * per_worker + step) * SC_ROWS, SC_ROWS)

        def idx_copy(step, p):
            return pltpu.make_async_copy(didx_hbm.at[:, pl.ds(off_of(step), SC_ROWS)], idx.at[p], isem.at[p])

        def gather(j):
            r = j % SC_NBUF
            return pltpu.make_async_copy(table_hbm.at[idx.at[j // TOP_K].at[j % TOP_K]], rows.at[r], gsem.at[r])

        def write(s0, j):
            r = j % SC_NBUF
            return pltpu.make_async_copy(rows.at[r], out_hbm.at[j % TOP_K, pl.ds(off_of(s0 + j // TOP_K), SC_ROWS)],
                                         wsem.at[r])

        def retire_write(s0, j):
            if j >= 0:
                write(s0, j).wait()
            else:
                @pl.when(s0 > 0)
                def _():
                    write(s0 - 2, j + items).wait()

        def finish_read(s0, j):
            if j >= 0:
                gather(j).wait()
                write(s0, j).start()
            else:
                @pl.when(s0 > 0)
                def _():
                    gather(j + items).wait()
                    write(s0 - 2, j + items).start()

        idx_copy(0, 0).start()

        @pl.loop(0, per_worker, step=2)
        def _(s0):
            for j in range(items):
                p, kk = j // TOP_K, j % TOP_K
                if kk == 0:
                    idx_copy(s0 + p, p).wait()
                retire_write(s0, j - SC_NBUF)
                gather(j).start()
                finish_read(s0, j - SC_LAG)
                if kk == SC_LAG:
                    @pl.when(s0 + p + 1 < per_worker)
                    def _():
                        idx_copy(s0 + p + 1, 1 - p).start()

        last = per_worker - 2
        for j in range(items - SC_LAG, items):
            gather(j).wait()
            write(last, j).start()
        for j in range(items - SC_NBUF, items):
            write(last, j).wait()

    return run(table, didx)


def _experts_kernel(cstart_ref, nchunk_ref, count_ref, total_ref, x_hbm, wg_ref, wu_ref, wd_ref, y_hbm,
                    wgu_s, wd_s, xbuf, ybuf, xsem, ysem):
    e = pl.program_id(0)
    nb = xbuf.shape[0]
    ch = xbuf.shape[2]
    total = total_ref[0]

    def x_copy(c):
        slot = lax.rem(c, nb)
        rows = pl.ds(pl.multiple_of(c * ch, ch), ch)
        return pltpu.make_async_copy(x_hbm.at[:, rows, :], xbuf.at[slot], xsem.at[slot])

    def y_copy(c):
        slot = lax.rem(c, nb)
        rows = pl.ds(pl.multiple_of(c * ch, ch), ch)
        return pltpu.make_async_copy(ybuf.at[slot], y_hbm.at[:, rows, :], ysem.at[slot])

    @pl.when(e == 0)
    def _():
        for c in range(nb - 1):
            @pl.when(c < total)
            def _():
                x_copy(c).start()

    for j in range(N_PLANES):
        hi = pl.ds(j * LANES, LANES)
        lo = pl.ds(HALF + j * LANES, LANES)
        r_hi = pl.ds(2 * j * LANES, LANES)
        r_lo = pl.ds((2 * j + 1) * LANES, LANES)
        wgu_s[r_hi, pl.ds(0, D_EXPERT)] = wg_ref[0, hi, :].astype(BF16)
        wgu_s[r_lo, pl.ds(0, D_EXPERT)] = wg_ref[0, lo, :].astype(BF16)
        wgu_s[r_hi, pl.ds(D_EXPERT, D_EXPERT)] = wu_ref[0, hi, :].astype(BF16)
        wgu_s[r_lo, pl.ds(D_EXPERT, D_EXPERT)] = wu_ref[0, lo, :].astype(BF16)
    wd_s[...] = wd_ref[0].astype(BF16)

    c0 = cstart_ref[e]
    count = count_ref[e]

    def chunk(i, carry):
        c = c0 + i
        slot = lax.rem(c, nb)

        @pl.when(c + nb - 1 < total)
        def _():
            x_copy(c + nb - 1).start()

        x_copy(c).wait()

        @pl.when(c >= nb)
        def _():
            y_copy(c - nb).wait()

        live = lax.broadcasted_iota(I32, (ch, LANES), 0) < count - i * ch
        xs = []
        for j in range(N_PLANES):
            hi, lo = _unpack_plane(jnp.where(live, xbuf[slot, j], 0))
            xs += [hi.astype(BF16), lo.astype(BF16)]
        acc = jnp.dot(jnp.concatenate(xs, axis=1), wgu_s[...], preferred_element_type=F32)
        gate, up = acc[:, :D_EXPERT], acc[:, D_EXPERT:]
        act = (gate * jax.nn.sigmoid(gate)) * up
        y = jnp.dot(act.astype(BF16), wd_s[...], preferred_element_type=F32)
        planes = _pack_planes(y)
        for j in range(N_PLANES):
            ybuf[slot, j] = planes[j]
        y_copy(c).start()
        return carry

    lax.fori_loop(0, nchunk_ref[e], chunk, 0)

    @pl.when(e == pl.num_programs(0) - 1)
    def _():
        for back in range(nb, 0, -1):
            @pl.when(total >= back)
            def _():
                y_copy(total - back).wait()


def _experts(xs_planes, w_gate, w_up, w_down, cstart, nchunk, counts, total):
    n_rows = xs_planes.shape[1]
    ch = EXPERT_CHUNK
    w_map = lambda e, *_: (e, 0, 0)
    any_space = pl.BlockSpec(memory_space=pl.ANY)
    return pl.pallas_call(
        _experts_kernel,
        grid_spec=pltpu.PrefetchScalarGridSpec(
            num_scalar_prefetch=4, grid=(N_EXPERTS,),
            in_specs=[any_space,
                      pl.BlockSpec((1, D_MODEL, D_EXPERT), w_map),
                      pl.BlockSpec((1, D_MODEL, D_EXPERT), w_map),
                      pl.BlockSpec((1, D_EXPERT, D_MODEL), w_map)],
            out_specs=any_space,
            scratch_shapes=[pltpu.VMEM((D_MODEL, 2 * D_EXPERT), BF16),
                            pltpu.VMEM((D_EXPERT, D_MODEL), BF16),
                            pltpu.VMEM((EXPERT_NBUF, N_PLANES, ch, LANES), I32),
                            pltpu.VMEM((EXPERT_NBUF, N_PLANES, ch, LANES), I32),
                            pltpu.SemaphoreType.DMA((EXPERT_NBUF,)),
                            pltpu.SemaphoreType.DMA((EXPERT_NBUF,))]),
        out_shape=jax.ShapeDtypeStruct((N_PLANES, n_rows, LANES), I32),
        compiler_params=pltpu.CompilerParams(dimension_semantics=("arbitrary",),
                                             vmem_limit_bytes=VMEM_LIMIT),
        name="experts",
    )(cstart, nchunk, counts, total, xs_planes, w_gate, w_up, w_down)


def _final_kernel(xa_ref, yk_ref, wts_ref, mod_ref, g_ref, o_ref):
    wts = wts_ref[...]
    his = [jnp.zeros((T_FIN, LANES), F32) for _ in range(N_PLANES)]
    los = [jnp.zeros((T_FIN, LANES), F32) for _ in range(N_PLANES)]
    for kk in range(TOP_K):
        wk = wts[:, kk:kk + 1]
        for j in range(N_PLANES):
            hi, lo = _unpack_plane(yk_ref[kk, j])
            his[j] = his[j] + wk * hi
            los[j] = los[j] + wk * lo
    routed = jnp.concatenate(his + los, axis=1)
    gt2 = mod_ref[0][:, 5 * D_MODEL:6 * D_MODEL]
    x = xa_ref[...] + gt2 * routed
    ms = jnp.mean(x * x, axis=-1, keepdims=True)
    o_ref[...] = x * lax.rsqrt(ms + EPS) * g_ref[...]


def _final_kernel_into(xa_ref, yk_ref, wts_ref, mod_ref, g_ref, prev_ref, o_ref):
    _final_kernel(xa_ref, yk_ref, wts_ref, mod_ref, g_ref, o_ref)


def _final(xa, yk, wts, mod3, g_final, tile_off, n_tok, seq_len, sid0, out_rows, out_tile_off, prev):
    tiles_per_seq = seq_len // T_FIN
    in_specs = [pl.BlockSpec((T_FIN, D_MODEL), lambda i: (i + tile_off, 0)),
                pl.BlockSpec((TOP_K, N_PLANES, T_FIN, LANES), lambda i: (0, 0, i + tile_off, 0)),
                pl.BlockSpec((T_FIN, LANES), lambda i: (i + tile_off, 0)),
                pl.BlockSpec((1, 1, 6 * D_MODEL), lambda i: (sid0 + i // tiles_per_seq, 0, 0)),
                pl.BlockSpec((1, D_MODEL), lambda i: (0, 0))]
    args = [xa, yk, wts, mod3, g_final]
    if prev is not None:
        in_specs.append(pl.BlockSpec(memory_space=pl.ANY))
        args.append(prev)
    return pl.pallas_call(
        _final_kernel if prev is None else _final_kernel_into,
        grid=(n_tok // T_FIN,),
        in_specs=in_specs,
        out_specs=pl.BlockSpec((T_FIN, D_MODEL), lambda i: (i + out_tile_off, 0)),
        out_shape=jax.ShapeDtypeStruct((out_rows, D_MODEL), F32),
        input_output_aliases={} if prev is None else {len(args) - 1: 0},
        compiler_params=pltpu.CompilerParams(dimension_semantics=("arbitrary",),
                                             vmem_limit_bytes=VMEM_LIMIT),
        name="final",
    )(*args)


def kernel(x_prompt, x_sample, c_prompt, c_sample, w_ada, b_ada, g_norm1, w_in, w_pool, b_pool, pool_scale, attn_sink, w_out, g_norm2, w_router, router_bias, w_gate, w_up, w_down, ws_gate, ws_up, ws_down, g_final):
    assert w_ada.shape[0] == 1, "one layer"
    bp, sp, d = x_prompt.shape
    bs, ss, _ = x_sample.shape
    assert d == D_MODEL and bp + bs <= ADA_ROWS
    n_p, n_s = bp * sp, bs * ss
    xp = x_prompt.reshape(n_p, d)
    xs = x_sample.reshape(n_s, d)

    c_all = jnp.concatenate([c_prompt, c_sample, jnp.zeros((ADA_ROWS - bp - bs, d), F32)], axis=0)
    mod3 = _ada(c_all, w_ada[0], b_ada[0]).reshape(ADA_ROWS, 1, 6 * d)

    q0, q1 = POOL_WIDTH, POOL_WIDTH + ATTN_WIDTH
    wq = w_in[0][:, q0:q1].reshape(d, N_KV_HEADS, Q_PER_KV, HEAD_DIM).transpose(0, 2, 1, 3).reshape(d, ATTN_WIDTH)
    w_in_p = jnp.concatenate([w_in[0][:, :q0], wq, w_in[0][:, q1:]], axis=1).astype(BF16)
    wo = w_out[0][q0:].reshape(N_KV_HEADS, Q_PER_KV, HEAD_DIM, d).transpose(1, 0, 2, 3).reshape(ATTN_WIDTH, d)
    w_out_p = jnp.concatenate([w_out[0][:q0], wo], axis=0).astype(BF16)
    wsgu = jnp.concatenate([ws_gate[0], ws_up[0]], axis=1).astype(BF16)
    wr_hi = w_router[0].astype(BF16)
    wr_lo = (w_router[0] - wr_hi.astype(F32)).astype(BF16)
    wr3 = jnp.concatenate([wr_hi, wr_lo, wr_hi], axis=0).T
    g1 = g_norm1[0].reshape(1, d)
    mixer_weights = (
        w_pool[0].astype(BF16), b_pool[0].reshape(1, POOL_WIDTH), pool_scale[0].reshape(1, POOL_WIDTH),
        attn_sink[0].reshape(1, N_HEADS), w_out_p, g_norm2[0].reshape(1, d),
        wr3, router_bias[0].reshape(N_EXPERTS, 1), wsgu, ws_down[0].astype(BF16))
    ch = EXPERT_CHUNK

    segments = ((0, 0, bp, sp, 0), (1, 0, bs, ss, bp))
    n = _group_tokens(segments)
    u, q, k, v = _inproj(xp, xs, mod3, g1, w_in_p, segments)
    xa, h2p, idx_t, rank_t, wts, cnt = _mixer(xp, xs, mod3, u, q, k, v, *mixer_weights, segments)
    n_rows = (-(-n * TOP_K // ch) + N_EXPERTS) * ch
    counts = cnt[:, 0].astype(I32)
    padded = (counts + ch - 1) // ch * ch
    pend = jnp.cumsum(padded)
    pstart = pend - padded
    didx = _dest(idx_t, rank_t, pstart.astype(F32).reshape(N_EXPERTS, 1), n_rows)
    didx = didx.reshape(TOP_K, N_PLANES * n)
    xs_rows = _sc_scatter_rows(h2p.reshape(N_PLANES * n, LANES), didx, N_PLANES * n_rows)
    ys_rows = _experts(xs_rows.reshape(N_PLANES, n_rows, LANES), w_gate[0], w_up[0], w_down[0],
                       pstart // ch, padded // ch, counts, pend[-1:] // ch)
    yk = _sc_gather_rows(ys_rows.reshape(N_PLANES * n_rows, LANES), didx).reshape(TOP_K, N_PLANES, n, LANES)

    gf = g_final.reshape(1, d)
    outs = [None, None]
    out_rows = (n_p, n_s)
    tok = 0
    for src, tok0, n_seqs, seq_len, sid0 in segments:
        n_tok = n_seqs * seq_len
        outs[src] = _final(xa, yk, wts, mod3, gf, tok // T_FIN, n_tok, seq_len, sid0,
                           out_rows[src], tok0 // T_FIN, outs[src])
        tok += n_tok
    return outs[0].reshape(bp, sp, d), outs[1].reshape(bs, ss, d)
```

```python
import functools

import numpy as np
import jax
import jax.numpy as jnp
from jax import lax
from jax.experimental import pallas as pl
from jax.experimental.pallas import tpu as pltpu
from jax.experimental.pallas import tpu_sc as plsc

F32 = jnp.float32
BF16 = jnp.bfloat16
I32 = jnp.int32

D_MODEL = 1024
POOL_WINDOWS = (2, 4, 8, 16)
POOL_WIDTH = 512
POOL_GROUP = 128
N_HEADS = 8
N_KV_HEADS = 2
HEAD_DIM = 64
Q_PER_KV = N_HEADS // N_KV_HEADS
ATTN_WIDTH = N_HEADS * HEAD_DIM
KV_WIDTH = N_KV_HEADS * HEAD_DIM
D_IN_PROJ = POOL_WIDTH + ATTN_WIDTH + 2 * KV_WIDTH
WINDOW = 128
N_EXPERTS = 256
TOP_K = 8
N_EXPERT_GROUPS = 8
GROUP_SIZE = N_EXPERTS // N_EXPERT_GROUPS
TOPK_GROUPS = 4
D_EXPERT = 256
D_SHARED = 256
ROUTED_SCALE = 2.5
EPS = 1e-6
NEG_INF = -1e30

LANES = 128
HALO = 8
N_PLANES = 4
HALF = D_MODEL // 2

T_IN = 1024
T_MIX = 512
ATT_BLOCK = 128
T_DEST = 2048
T_DEST_SLAB = 512
EXPERT_CHUNK = 512
EXPERT_NBUF = 4
T_FIN = 256
SC_ROWS = 128
SC_WORKERS = 32
SC_NBUF = 4
SC_LAG = 2
ADA_ROWS = 16
VMEM_LIMIT = 48 * 1024 * 1024

LOG2E = 1.4426950408889634
ALIBI_SLOPES = tuple(float(2.0 ** (-8.0 * (h + 1) / N_HEADS)) for h in range(N_HEADS))


def _pack_planes(y):
    planes = []
    for j in range(N_PLANES):
        hi = y[:, j * LANES:(j + 1) * LANES].astype(BF16).astype(F32)
        lo = y[:, HALF + j * LANES:HALF + (j + 1) * LANES].astype(BF16).astype(F32)
        hb = lax.bitcast_convert_type(hi, jnp.uint32) & jnp.uint32(0xFFFF0000)
        lb = lax.bitcast_convert_type(lo, jnp.uint32) >> jnp.uint32(16)
        planes.append(lax.bitcast_convert_type(hb | lb, I32))
    return planes


def _unpack_plane(w):
    u = lax.bitcast_convert_type(w, jnp.uint32)
    hi = lax.bitcast_convert_type(u & jnp.uint32(0xFFFF0000), F32)
    lo = lax.bitcast_convert_type(u << jnp.uint32(16), F32)
    return hi, lo


def _ada_kernel(c_ref, w_ref, b_ref, o_ref):
    c = c_ref[...]
    s = c * jax.nn.sigmoid(c)
    o_ref[...] = jnp.dot(s, w_ref[...], precision=lax.Precision.HIGHEST,
                         preferred_element_type=F32) + b_ref[...]


def _ada(c_all, w_ada, b_ada):
    n_out = w_ada.shape[1]
    tn = 1024
    return pl.pallas_call(
        _ada_kernel,
        grid=(n_out // tn,),
        in_specs=[pl.BlockSpec((ADA_ROWS, D_MODEL), lambda j: (0, 0)),
                  pl.BlockSpec((D_MODEL, tn), lambda j: (0, j)),
                  pl.BlockSpec((1, tn), lambda j: (0, j))],
        out_specs=pl.BlockSpec((ADA_ROWS, tn), lambda j: (0, j)),
        out_shape=jax.ShapeDtypeStruct((ADA_ROWS, n_out), F32),
        name="ada",
    )(c_all, w_ada, b_ada.reshape(1, n_out))


def _rms_mod(x, g, scale, shift):
    ms = jnp.mean(x * x, axis=-1, keepdims=True)
    return (x * lax.rsqrt(ms + EPS) * g) * (1.0 + scale) + shift


def _inproj_kernel(meta_ref, xp_ref, xs_ref, mod_ref, g_ref, w_ref, u_ref, q_ref, k_ref, v_ref):
    x = jnp.where(meta_ref[7, pl.program_id(0)] == 0, xp_ref[...], xs_ref[...])
    mod = mod_ref[0]
    h = _rms_mod(x, g_ref[...], mod[:, D_MODEL:2 * D_MODEL], mod[:, 0:D_MODEL])
    z = jnp.dot(h.astype(BF16), w_ref[...], preferred_element_type=F32)
    u_ref[...] = z[:, :POOL_WIDTH]
    q_ref[...] = (z[:, POOL_WIDTH:POOL_WIDTH + ATTN_WIDTH] * (LOG2E * HEAD_DIM ** -0.5)).astype(BF16)
    k_ref[...] = z[:, POOL_WIDTH + ATTN_WIDTH:POOL_WIDTH + ATTN_WIDTH + KV_WIDTH].astype(BF16)
    v_ref[...] = z[:, POOL_WIDTH + ATTN_WIDTH + KV_WIDTH:].astype(BF16)


def _tile_meta(segments, tile):
    rows = []
    for src, tok0, n_seqs, seq_len, sid0 in segments:
        per = seq_len // tile
        for q in range(n_seqs):
            for t in range(per):
                blk = (tok0 + q * seq_len) // tile + t
                rows.append((sid0 + q, int(t == 0), int(t == per - 1), t * tile, seq_len,
                             blk if src == 0 else -1, blk if src == 1 else -1, src))
    meta = np.asarray(rows, np.int32)
    for col in (5, 6):
        known = meta[:, col] >= 0
        if not known.any():
            meta[:, col] = 0
            continue
        last = np.maximum.accumulate(np.where(known, np.arange(len(meta)), -1))
        last = np.where(last < 0, np.argmax(known), last)
        meta[:, col] = meta[last, col]
    return meta.T.copy()


def _group_tokens(segments):
    return sum(n_seqs * seq_len for _, _, n_seqs, seq_len, _ in segments)


def _inproj(xp, xs, mod3, g1, w_in_bf, segments):
    n = _group_tokens(segments)
    t_in = int(np.gcd.reduce([T_IN] + [seq_len for _, _, _, seq_len, _ in segments]))
    meta = jnp.asarray(_tile_meta(segments, t_in))
    tok = lambda w: pl.BlockSpec((t_in, w), lambda i, m: (i, 0))
    return pl.pallas_call(
        _inproj_kernel,
        grid_spec=pltpu.PrefetchScalarGridSpec(
            num_scalar_prefetch=1,
            grid=(n // t_in,),
            in_specs=[
                pl.BlockSpec((t_in, D_MODEL), lambda i, m: (m[5, i], 0)),
                pl.BlockSpec((t_in, D_MODEL), lambda i, m: (m[6, i], 0)),
                pl.BlockSpec((1, 1, 6 * D_MODEL), lambda i, m: (m[0, i], 0, 0)),
                pl.BlockSpec((1, D_MODEL), lambda i, m: (0, 0)),
                pl.BlockSpec((D_MODEL, D_IN_PROJ), lambda i, m: (0, 0)),
            ],
            out_specs=[tok(POOL_WIDTH), tok(ATTN_WIDTH), tok(KV_WIDTH), tok(KV_WIDTH)],
        ),
        out_shape=[jax.ShapeDtypeStruct((n, POOL_WIDTH), F32),
                   jax.ShapeDtypeStruct((n, ATTN_WIDTH), BF16),
                   jax.ShapeDtypeStruct((n, KV_WIDTH), BF16),
                   jax.ShapeDtypeStruct((n, KV_WIDTH), BF16)],
        compiler_params=pltpu.CompilerParams(dimension_semantics=("arbitrary",),
                                             vmem_limit_bytes=VMEM_LIMIT),
        name="inproj",
    )(meta, xp, xs, mod3, g1, w_in_bf)


def _pool_mixer(uext_ref, uc, pos, seq_len, wpool_ref, bpool_ref, pscale_ref):
    outs = []
    for gi, w in enumerate(POOL_WINDOWS):
        c0 = gi * POOL_GROUP
        half = w // 2
        acc = uext_ref[pl.ds(HALO - half, T_MIX), pl.ds(c0, POOL_GROUP)]
        for o in range(-half + 1, half):
            acc = acc + uext_ref[pl.ds(HALO + o, T_MIX), pl.ds(c0, POOL_GROUP)]
        lo = jnp.maximum(pos - half, 0)
        hi = jnp.minimum(pos + half, seq_len)
        cnt = (hi - lo).astype(F32)
        d = acc / cnt - uc[:, c0:c0 + POOL_GROUP]
        y = jnp.dot(d.astype(BF16), wpool_ref[gi], preferred_element_type=F32)
        y = (y + bpool_ref[:, c0:c0 + POOL_GROUP]) * pscale_ref[:, c0:c0 + POOL_GROUP]
        outs.append(y)
    return jnp.concatenate(outs, axis=1)


def _attention_bias():
    t = ATT_BLOCK
    r = np.arange(t)[:, None]
    c = np.arange(3 * t)[None, :]
    dist = np.abs(r - (c - t))
    band = dist <= WINDOW
    out = np.empty((3, N_HEADS, t, 3 * t), np.float32)
    for var, valid in enumerate((band, band & (c >= t), band & (c < 2 * t))):
        for h in range(N_HEADS):
            out[var, h] = np.where(valid, -ALIBI_SLOPES[h] * LOG2E * dist, NEG_INF)
    return out.reshape(3, N_HEADS * t, 3 * t)


def _attention_scores(q, kw):
    t = ATT_BLOCK
    low = lax.broadcasted_iota(I32, (t, LANES), 1) < HEAD_DIM
    zero = jnp.zeros((t, LANES), BF16)
    qs = []
    for h in range(N_HEADS):
        blk = q[:, (h % Q_PER_KV) * LANES:(h % Q_PER_KV + 1) * LANES]
        qs.append(jnp.where(low, blk, zero) if h < Q_PER_KV else jnp.where(low, zero, blk))
    return lax.dot_general(jnp.concatenate(qs, axis=0), kw, (((1,), (1,)), ((), ())),
                           preferred_element_type=F32)


def _attention_probs(s, variant, bias_ref, sink_ref):
    t = ATT_BLOCK
    sh = s + bias_ref[variant]
    sink = jnp.concatenate([jnp.broadcast_to(sink_ref[:, h:h + 1] * LOG2E, (t, 1)) for h in range(N_HEADS)], axis=0)
    m = jnp.maximum(jnp.max(sh, axis=-1, keepdims=True), sink)
    p = jnp.exp2(sh - m)
    den = jnp.sum(p, axis=-1, keepdims=True) + jnp.exp2(sink - m)
    return p.astype(BF16), den


def _attention_out(p, den, vw):
    t = ATT_BLOCK
    low = lax.broadcasted_iota(I32, (t, LANES), 1) < HEAD_DIM
    o = jnp.dot(p, vw, preferred_element_type=F32) / den
    return jnp.concatenate(
        [jnp.where(low, o[cb * t:(cb + 1) * t], o[(Q_PER_KV + cb) * t:(Q_PER_KV + cb + 1) * t])
         for cb in range(Q_PER_KV)], axis=1)


def _route(biased, scores):
    t = biased.shape[1]
    rowf = lax.broadcasted_iota(I32, (N_EXPERTS, t), 0).astype(F32)
    ninf = float("-inf")
    gs = []
    for g in range(N_EXPERT_GROUPS):
        blk = biased[g * GROUP_SIZE:(g + 1) * GROUP_SIZE, :]
        rf = rowf[g * GROUP_SIZE:(g + 1) * GROUP_SIZE, :]
        m1 = jnp.max(blk, axis=0, keepdims=True)
        i1 = jnp.min(jnp.where(blk == m1, rf, float(N_EXPERTS)), axis=0, keepdims=True)
        m2 = jnp.max(jnp.where(rf == i1, ninf, blk), axis=0, keepdims=True)
        gs.append(m1 + m2)
    keep = []
    for g in range(N_EXPERT_GROUPS):
        beat = jnp.zeros((1, t), F32)
        for g2 in range(N_EXPERT_GROUPS):
            if g2 == g:
                continue
            better = (gs[g2] >= gs[g]) if g2 < g else (gs[g2] > gs[g])
            beat = beat + better.astype(F32)
        keep.append(jnp.broadcast_to(beat < float(TOPK_GROUPS), (GROUP_SIZE, t)))
    emask = jnp.concatenate(keep, axis=0)
    masked = jnp.where(emask, biased, NEG_INF)
    idx_rows, w_rows = [], []
    sel_any = jnp.zeros((N_EXPERTS, t), F32)
    for _ in range(TOP_K):
        m = jnp.max(masked, axis=0, keepdims=True)
        ik = jnp.min(jnp.where(masked == m, rowf, float(N_EXPERTS)), axis=0, keepdims=True)
        sel = rowf == ik
        w_rows.append(jnp.sum(jnp.where(sel, scores, 0.0), axis=0, keepdims=True))
        idx_rows.append(ik)
        masked = jnp.where(sel, ninf, masked)
        sel_any = sel_any + sel.astype(F32)
    return idx_rows, w_rows, sel_any, rowf


def _mixer_kernel(meta_ref, xp_ref, xs_ref, mod_ref, up_ref, uc_ref, un_ref, q_ref,
                  kp_ref, kc_ref, kn_ref, vp_ref, vc_ref, vn_ref,
                  wpool_ref, bpool_ref, pscale_ref, sink_ref, wout_ref, g2_ref,
                  wr3_ref, rbias_ref, wsgu_ref, wsd_ref, tri_ref, abias_ref,
                  xa_ref, h2p_ref, idx_ref, rank_ref, wts_ref, cnt_ref,
                  uext_ref, kext_ref, vext_ref, base_ref):
    i = pl.program_id(0)
    first = meta_ref[1, i] == 1
    last = meta_ref[2, i] == 1
    pos0 = meta_ref[3, i]
    seq_len = meta_ref[4, i]
    mod = mod_ref[0]
    gt1 = mod[:, 2 * D_MODEL:3 * D_MODEL]
    sh2 = mod[:, 3 * D_MODEL:4 * D_MODEL]
    sc2 = mod[:, 4 * D_MODEL:5 * D_MODEL]
    gt2 = mod[:, 5 * D_MODEL:6 * D_MODEL]

    @pl.when(i == 0)
    def _():
        base_ref[...] = jnp.zeros_like(base_ref)

    uc = uc_ref[...]
    uext_ref[pl.ds(0, HALO), :] = jnp.where(first, 0.0, up_ref[...])
    uext_ref[pl.ds(HALO, T_MIX), :] = uc
    uext_ref[pl.ds(HALO + T_MIX, HALO), :] = jnp.where(last, 0.0, un_ref[...])
    pos = pos0 + lax.broadcasted_iota(I32, (T_MIX, 1), 0)
    a_pool = _pool_mixer(uext_ref, uc, pos, seq_len, wpool_ref, bpool_ref, pscale_ref)

    t = ATT_BLOCK
    kext_ref[pl.ds(0, t), :] = kp_ref[...]
    kext_ref[pl.ds(t, T_MIX), :] = kc_ref[...]
    kext_ref[pl.ds(t + T_MIX, t), :] = kn_ref[...]
    vext_ref[pl.ds(0, t), :] = vp_ref[...]
    vext_ref[pl.ds(t, T_MIX), :] = vc_ref[...]
    vext_ref[pl.ds(t + T_MIX, t), :] = vn_ref[...]
    n_sub = T_MIX // t
    assert n_sub >= 2, "a query block is never both first and last in its sequence"
    variants = []
    for sub in range(n_sub):
        variant = 0
        if sub == 0:
            variant = jnp.where(first, 1, variant)
        if sub == n_sub - 1:
            variant = jnp.where(last, 2, variant)
        variants.append(variant)
    scores = [_attention_scores(q_ref[pl.ds(sub * t, t), :], kext_ref[pl.ds(sub * t, 3 * t), :])
              for sub in range(n_sub)]
    probs = [_attention_probs(scores[sub], variants[sub], abias_ref, sink_ref) for sub in range(n_sub)]
    attn = [_attention_out(probs[sub][0], probs[sub][1], vext_ref[pl.ds(sub * t, 3 * t), :])
            for sub in range(n_sub)]
    a_attn = jnp.concatenate(attn, axis=0)

    a = jnp.concatenate([a_pool, a_attn], axis=1).astype(BF16)
    mix = jnp.dot(a, wout_ref[...], preferred_element_type=F32)

    x = jnp.where(meta_ref[7, i] == 0, xp_ref[...], xs_ref[...])
    x1 = x + gt1 * mix
    h2 = _rms_mod(x1, g2_ref[...], sc2, sh2)
    h2b = h2.astype(BF16)
    planes = _pack_planes(h2)
    for j in range(N_PLANES):
        h2p_ref[j] = planes[j]
    gu = jnp.dot(h2b, wsgu_ref[...], preferred_element_type=F32)
    gate, up = gu[:, :D_SHARED], gu[:, D_SHARED:]
    act = (gate * jax.nn.sigmoid(gate)) * up
    shared = jnp.dot(act.astype(BF16), wsd_ref[...], preferred_element_type=F32)
    xa_ref[...] = x1 + gt2 * shared
    h2lo = (h2 - h2b.astype(F32)).astype(BF16)
    logits = lax.dot_general(wr3_ref[...], jnp.concatenate([h2b, h2b, h2lo], axis=1),
                             (((1,), (1,)), ((), ())), preferred_element_type=F32)
    scores = jax.nn.sigmoid(logits)
    biased = scores + rbias_ref[...]
    routed = [_route(biased[:, c * LANES:(c + 1) * LANES], scores[:, c * LANES:(c + 1) * LANES])
              for c in range(T_MIX // LANES)]
    for c, (idx_rows, w_rows, _, _) in enumerate(routed):
        wsum = w_rows[0]
        for wr in w_rows[1:]:
            wsum = wsum + wr
        wts_t = jnp.concatenate([wr / wsum * ROUTED_SCALE for wr in w_rows], axis=0)
        wpad = jnp.concatenate([wts_t, jnp.zeros((LANES - TOP_K, LANES), F32)], axis=0)
        wts_ref[pl.ds(c * LANES, LANES), :] = wpad.T
        idx_ref[:, pl.ds(c * LANES, LANES)] = jnp.concatenate(idx_rows, axis=0).astype(I32)
    sel_any = jnp.concatenate([r[2] for r in routed], axis=1)
    pref = jnp.dot(sel_any.astype(BF16), tri_ref[...], preferred_element_type=F32)
    before = base_ref[...] + pref[:, :T_MIX]
    for c, (idx_rows, _, _, rowf) in enumerate(routed):
        bc = before[:, c * LANES:(c + 1) * LANES]
        ranks = [jnp.sum(jnp.where(rowf == ik, bc, 0.0), axis=0, keepdims=True) for ik in idx_rows]
        rank_ref[:, pl.ds(c * LANES, LANES)] = jnp.concatenate(ranks, axis=0).astype(I32)
    new_base = base_ref[...] + pref[:, T_MIX:]
    base_ref[...] = new_base
    cnt_ref[...] = new_base


def _mixer(xp, xs, mod3, u, q, k, v, wpool_bf, bpool, pscale, sink, wout_bf, g2, wr3, rbias,
           wsgu_bf, wsd_bf, segments):
    n = _group_tokens(segments)
    t = T_MIX
    nt = n // t
    ab = ATT_BLOCK
    meta = jnp.asarray(_tile_meta(segments, t))
    tri = np.concatenate([np.triu(np.ones((t, t), np.float32), 1), np.ones((t, t), np.float32)], axis=1)
    tri = jnp.asarray(tri, BF16)
    abias = jnp.asarray(_attention_bias())
    cur = lambda w: pl.BlockSpec((t, w), lambda i, m: (i, 0))
    prev = lambda w: pl.BlockSpec((ab, w), lambda i, m: (jnp.maximum(i * (t // ab) - 1, 0), 0))
    nxt = lambda w: pl.BlockSpec((ab, w), lambda i, m: (jnp.minimum((i + 1) * (t // ab), n // ab - 1), 0))
    full = lambda a: pl.BlockSpec(a.shape, lambda i, m: (0,) * a.ndim)
    hb = t // HALO
    in_specs = [
        pl.BlockSpec((t, D_MODEL), lambda i, m: (m[5, i], 0)),
        pl.BlockSpec((t, D_MODEL), lambda i, m: (m[6, i], 0)),
        pl.BlockSpec((1, 1, 6 * D_MODEL), lambda i, m: (m[0, i], 0, 0)),
        pl.BlockSpec((HALO, POOL_WIDTH), lambda i, m: (jnp.maximum(i * hb - 1, 0), 0)),
        cur(POOL_WIDTH),
        pl.BlockSpec((HALO, POOL_WIDTH), lambda i, m: (jnp.minimum((i + 1) * hb, n // HALO - 1), 0)),
        cur(ATTN_WIDTH),
        prev(KV_WIDTH), cur(KV_WIDTH), nxt(KV_WIDTH),
        prev(KV_WIDTH), cur(KV_WIDTH), nxt(KV_WIDTH),
        full(wpool_bf), full(bpool), full(pscale), full(sink), full(wout_bf), full(g2),
        full(wr3), full(rbias), full(wsgu_bf), full(wsd_bf), full(tri), full(abias),
    ]
    out_specs = [
        cur(D_MODEL),
        pl.BlockSpec((N_PLANES, t, LANES), lambda i, m: (0, i, 0)),
        pl.BlockSpec((TOP_K, t), lambda i, m: (0, i)),
        pl.BlockSpec((TOP_K, t), lambda i, m: (0, i)),
        cur(LANES),
        pl.BlockSpec((N_EXPERTS, t), lambda i, m: (0, 0)),
    ]
    out_shape = [
        jax.ShapeDtypeStruct((n, D_MODEL), F32),
        jax.ShapeDtypeStruct((N_PLANES, n, LANES), I32),
        jax.ShapeDtypeStruct((TOP_K, n), I32),
        jax.ShapeDtypeStruct((TOP_K, n), I32),
        jax.ShapeDtypeStruct((n, LANES), F32),
        jax.ShapeDtypeStruct((N_EXPERTS, t), F32),
    ]
    return pl.pallas_call(
        _mixer_kernel,
        grid_spec=pltpu.PrefetchScalarGridSpec(
            num_scalar_prefetch=1, grid=(nt,), in_specs=in_specs, out_specs=out_specs,
            scratch_shapes=[pltpu.VMEM((t + 2 * HALO, POOL_WIDTH), F32),
                            pltpu.VMEM((t + 2 * ab, KV_WIDTH), BF16),
                            pltpu.VMEM((t + 2 * ab, KV_WIDTH), BF16),
                            pltpu.VMEM((N_EXPERTS, t), F32)]),
        out_shape=out_shape,
        compiler_params=pltpu.CompilerParams(dimension_semantics=("arbitrary",),
                                             vmem_limit_bytes=VMEM_LIMIT),
        name="mixer",
    )(meta, xp, xs, mod3, u, u, u, q, k, k, k, v, v, v, wpool_bf, bpool, pscale, sink, wout_bf, g2,
      wr3, rbias, wsgu_bf, wsd_bf, tri, abias)


def _dest_kernel(idx_ref, rank_ref, pstart_ref, o_ref, *, n_rows):
    slab = min(T_DEST_SLAB, idx_ref.shape[1])
    rowi = lax.broadcasted_iota(I32, (N_EXPERTS, slab), 0)
    pstart = pstart_ref[...]
    for c in range(idx_ref.shape[1] // slab):
        lanes = pl.ds(c * slab, slab)
        rows = []
        for kk in range(TOP_K):
            sel = rowi == idx_ref[kk:kk + 1, lanes]
            start = jnp.sum(jnp.where(sel, pstart, 0.0), axis=0, keepdims=True)
            rows.append(start.astype(I32) + rank_ref[kk:kk + 1, lanes])
        dest = jnp.concatenate(rows, axis=0)
        for j in range(N_PLANES):
            o_ref[:, j, lanes] = dest + j * n_rows


def _dest(idx_t, rank_t, pstart_col, n_rows):
    n = idx_t.shape[1]
    td = int(np.gcd(n, T_DEST))
    return pl.pallas_call(
        functools.partial(_dest_kernel, n_rows=n_rows),
        grid=(n // td,),
        in_specs=[pl.BlockSpec((TOP_K, td), lambda i: (0, i)),
                  pl.BlockSpec((TOP_K, td), lambda i: (0, i)),
                  pl.BlockSpec((N_EXPERTS, 1), lambda i: (0, 0))],
        out_specs=pl.BlockSpec((TOP_K, N_PLANES, td), lambda i: (0, 0, i)),
        out_shape=jax.ShapeDtypeStruct((TOP_K, N_PLANES, n), I32),
        name="dest",
    )(idx_t, rank_t, pstart_col)


def _sc_mesh():
    return plsc.VectorSubcoreMesh(core_axis_name="c", subcore_axis_name="s")


def _sc_worker():
    return lax.axis_index("s") * 2 + lax.axis_index("c")


def _sc_scatter_rows(table, didx, n_out_rows):
    m = table.shape[0]
    per_worker = m // SC_ROWS // SC_WORKERS
    assert per_worker % 2 == 0

    @functools.partial(
        pl.kernel, mesh=_sc_mesh(),
        out_type=jax.ShapeDtypeStruct((n_out_rows, LANES), I32),
        scratch_types=[pltpu.VMEM((2, SC_ROWS, LANES), I32), pltpu.VMEM((2, TOP_K, SC_ROWS), I32),
                       pltpu.SemaphoreType.DMA((2,)), pltpu.SemaphoreType.DMA((2,))])
    def run(table_hbm, didx_hbm, out_hbm, rows, idx, lsem, ssem):
        wid = _sc_worker()

        def loads(step, p):
            off = pl.multiple_of((wid * per_worker + step) * SC_ROWS, SC_ROWS)
            return (pltpu.make_async_copy(table_hbm.at[pl.ds(off, SC_ROWS)], rows.at[p], lsem.at[p]),
                    pltpu.make_async_copy(didx_hbm.at[:, pl.ds(off, SC_ROWS)], idx.at[p], lsem.at[p]))

        def scatter(p, kk):
            return pltpu.make_async_copy(rows.at[p], out_hbm.at[idx.at[p].at[kk]], ssem.at[p])

        for cp in loads(0, 0):
            cp.start()

        @pl.loop(0, per_worker, step=2)
        def _(s0):
            for p in range(2):
                step = s0 + p
                for cp in loads(step, p):
                    cp.wait()
                for kk in range(TOP_K):
                    scatter(p, kk).start()

                @pl.when(step > 0)
                def _():
                    for kk in range(TOP_K):
                        scatter(1 - p, kk).wait()

                @pl.when(step + 1 < per_worker)
                def _():
                    for cp in loads(step + 1, 1 - p):
                        cp.start()

        for kk in range(TOP_K):
            scatter(1, kk).wait()

    return run(table, didx)


def _sc_gather_rows(table, didx):
    m = didx.shape[1]
    per_worker = m // SC_ROWS // SC_WORKERS
    assert per_worker % 2 == 0 and (2 * TOP_K) % SC_NBUF == 0
    items = 2 * TOP_K

    @functools.partial(
        pl.kernel, mesh=_sc_mesh(),
        out_type=jax.ShapeDtypeStruct((TOP_K, m, LANES), I32),
        scratch_types=[pltpu.VMEM((SC_NBUF, SC_ROWS, LANES), I32), pltpu.VMEM((2, TOP_K, SC_ROWS), I32),
                       pltpu.SemaphoreType.DMA((SC_NBUF,)), pltpu.SemaphoreType.DMA((SC_NBUF,)),
                       pltpu.SemaphoreType.DMA((2,))])
    def run(table_hbm, didx_hbm, out_hbm, rows, idx, gsem, wsem, isem):
        wid = _sc_worker()

        def off_of(step):
            return pl.multiple_of((wid * per_worker + step) * SC_ROWS, SC_ROWS)

        def idx_copy(step, p):
            return pltpu.make_async_copy(didx_hbm.at[:, pl.ds(off_of(step), SC_ROWS)], idx.at[p], isem.at[p])

        def gather(j):
            r = j % SC_NBUF
            return pltpu.make_async_copy(table_hbm.at[idx.at[j // TOP_K].at[j % TOP_K]], rows.at[r], gsem.at[r])

        def write(s0, j):
            r = j % SC_NBUF
            return pltpu.make_async_copy(rows.at[r], out_hbm.at[j % TOP_K, pl.ds(off_of(s0 + j // TOP_K), SC_ROWS)],
                                         wsem.at[r])

        def retire_write(s0, j):
            if j >= 0:
                write(s0, j).wait()
            else:
                @pl.when(s0 > 0)
                def _():
                    write(s0 - 2, j + items).wait()

        def finish_read(s0, j):
            if j >= 0:
                gather(j).wait()
                write(s0, j).start()
            else:
                @pl.when(s0 > 0)
                def _():
                    gather(j + items).wait()
                    write(s0 - 2, j + items).start()

        idx_copy(0, 0).start()

        @pl.loop(0, per_worker, step=2)
        def _(s0):
            for j in range(items):
                p, kk = j // TOP_K, j % TOP_K
                if kk == 0:
                    idx_copy(s0 + p, p).wait()
                retire_write(s0, j - SC_NBUF)
                gather(j).start()
                finish_read(s0, j - SC_LAG)
                if kk == SC_LAG:
                    @pl.when(s0 + p + 1 < per_worker)
                    def _():
                        idx_copy(s0 + p + 1, 1 - p).start()

        last = per_worker - 2
        for j in range(items - SC_LAG, items):
            gather(j).wait()
            write(last, j).start()
        for j in range(items - SC_NBUF, items):
            write(last, j).wait()

    return run(table, didx)


def _experts_kernel(c0_ref, ntot_ref, tsrc_ref, trow_ref, tvalid_ref, total_ref, *refs, n_groups):
    x_hbms = refs[:n_groups]
    wg_ref, wu_ref, wd_ref = refs[n_groups:n_groups + 3]
    y_hbms = refs[n_groups + 3:2 * n_groups + 3]
    wgu_s, wd_s, xbuf, ybuf, xsem, ysem = refs[2 * n_groups + 3:]
    e = pl.program_id(0)
    nb = xbuf.shape[0]
    ch = xbuf.shape[2]
    total = total_ref[0]

    def x_copy(c, g, row):
        slot = lax.rem(c, nb)
        return pltpu.make_async_copy(x_hbms[g].at[:, pl.ds(row, ch), :], xbuf.at[slot], xsem.at[slot])

    def y_copy(c, g, row):
        slot = lax.rem(c, nb)
        return pltpu.make_async_copy(ybuf.at[slot], y_hbms[g].at[:, pl.ds(row, ch), :], ysem.at[slot])

    def start(copy, c):
        row = pl.multiple_of(trow_ref[c], ch)
        for g in range(n_groups):
            @pl.when(tsrc_ref[c] == g)
            def _():
                copy(c, g, row).start()

    def wait(copy, c):
        copy(c, 0, 0).wait()

    @pl.when(e == 0)
    def _():
        for c in range(nb - 1):
            @pl.when(c < total)
            def _():
                start(x_copy, c)

    for j in range(N_PLANES):
        hi = pl.ds(j * LANES, LANES)
        lo = pl.ds(HALF + j * LANES, LANES)
        r_hi = pl.ds(2 * j * LANES, LANES)
        r_lo = pl.ds((2 * j + 1) * LANES, LANES)
        wgu_s[r_hi, pl.ds(0, D_EXPERT)] = wg_ref[0, hi, :].astype(BF16)
        wgu_s[r_lo, pl.ds(0, D_EXPERT)] = wg_ref[0, lo, :].astype(BF16)
        wgu_s[r_hi, pl.ds(D_EXPERT, D_EXPERT)] = wu_ref[0, hi, :].astype(BF16)
        wgu_s[r_lo, pl.ds(D_EXPERT, D_EXPERT)] = wu_ref[0, lo, :].astype(BF16)
    wd_s[...] = wd_ref[0].astype(BF16)

    c0 = c0_ref[e]

    def chunk(i, carry):
        c = c0 + i
        slot = lax.rem(c, nb)

        @pl.when(c + nb - 1 < total)
        def _():
            start(x_copy, c + nb - 1)

        wait(x_copy, c)

        @pl.when(c >= nb)
        def _():
            wait(y_copy, c - nb)

        live = lax.broadcasted_iota(I32, (ch, LANES), 0) < tvalid_ref[c]
        xs = []
        for j in range(N_PLANES):
            hi, lo = _unpack_plane(jnp.where(live, xbuf[slot, j], 0))
            xs += [hi.astype(BF16), lo.astype(BF16)]
        acc = jnp.dot(jnp.concatenate(xs, axis=1), wgu_s[...], preferred_element_type=F32)
        gate, up = acc[:, :D_EXPERT], acc[:, D_EXPERT:]
        act = (gate * jax.nn.sigmoid(gate)) * up
        y = jnp.dot(act.astype(BF16), wd_s[...], preferred_element_type=F32)
        planes = _pack_planes(y)
        for j in range(N_PLANES):
            ybuf[slot, j] = planes[j]
        start(y_copy, c)
        return carry

    lax.fori_loop(0, ntot_ref[e], chunk, 0)

    @pl.when(e == pl.num_programs(0) - 1)
    def _():
        for back in range(nb, 0, -1):
            @pl.when(total >= back)
            def _():
                wait(y_copy, total - back)


def _chunk_tables(nchunks, cstarts, counts, max_chunks):
    ch = EXPERT_CHUNK
    ntot = sum(nchunks)
    cum = jnp.cumsum(ntot)
    c0 = cum - ntot
    c = jnp.arange(max_chunks, dtype=I32)
    e = jnp.minimum(jnp.sum((c[:, None] >= cum[None, :]).astype(I32), axis=1), N_EXPERTS - 1)
    own = e[:, None] == jnp.arange(N_EXPERTS, dtype=I32)[None, :]
    pick = lambda v: jnp.sum(jnp.where(own, v[None, :], 0), axis=1)
    i = c - pick(c0)
    src = jnp.zeros_like(c)
    row = jnp.zeros_like(c)
    valid = jnp.zeros_like(c)
    lo = jnp.zeros_like(c)
    for g in range(len(nchunks)):
        n_g = pick(nchunks[g])
        inside = (i >= lo) & (i < lo + n_g) & (c < cum[-1])
        loc = i - lo
        src = jnp.where(inside, g, src)
        row = jnp.where(inside, (pick(cstarts[g]) + loc) * ch, row)
        valid = jnp.where(inside, jnp.clip(pick(counts[g]) - loc * ch, 0, ch), valid)
        lo = lo + n_g
    return c0.astype(I32), ntot.astype(I32), src, row, valid, cum[-1:].astype(I32)


def _experts(xs_planes, w_gate, w_up, w_down, nchunks, cstarts, counts):
    ng = len(xs_planes)
    ch = EXPERT_CHUNK
    tables = _chunk_tables(nchunks, cstarts, counts, sum(x.shape[1] // ch for x in xs_planes))
    w_map = lambda e, *_: (e, 0, 0)
    any_space = pl.BlockSpec(memory_space=pl.ANY)
    return pl.pallas_call(
        functools.partial(_experts_kernel, n_groups=ng),
        grid_spec=pltpu.PrefetchScalarGridSpec(
            num_scalar_prefetch=6, grid=(N_EXPERTS,),
            in_specs=[any_space] * ng + [pl.BlockSpec((1, D_MODEL, D_EXPERT), w_map),
                                         pl.BlockSpec((1, D_MODEL, D_EXPERT), w_map),
                                         pl.BlockSpec((1, D_EXPERT, D_MODEL), w_map)],
            out_specs=[any_space] * ng,
            scratch_shapes=[pltpu.VMEM((D_MODEL, 2 * D_EXPERT), BF16),
                            pltpu.VMEM((D_EXPERT, D_MODEL), BF16),
                            pltpu.VMEM((EXPERT_NBUF, N_PLANES, ch, LANES), I32),
                            pltpu.VMEM((EXPERT_NBUF, N_PLANES, ch, LANES), I32),
                            pltpu.SemaphoreType.DMA((EXPERT_NBUF,)),
                            pltpu.SemaphoreType.DMA((EXPERT_NBUF,))]),
        out_shape=[jax.ShapeDtypeStruct(x.shape, I32) for x in xs_planes],
        compiler_params=pltpu.CompilerParams(dimension_semantics=("arbitrary",),
                                             vmem_limit_bytes=VMEM_LIMIT),
        name="experts",
    )(*tables, *xs_planes, w_gate, w_up, w_down)


def _final_kernel(xa_ref, yk_ref, wts_ref, mod_ref, g_ref, o_ref):
    wts = wts_ref[...]
    his = [jnp.zeros((T_FIN, LANES), F32) for _ in range(N_PLANES)]
    los = [jnp.zeros((T_FIN, LANES), F32) for _ in range(N_PLANES)]
    for kk in range(TOP_K):
        wk = wts[:, kk:kk + 1]
        for j in range(N_PLANES):
            hi, lo = _unpack_plane(yk_ref[kk, j])
            his[j] = his[j] + wk * hi
            los[j] = los[j] + wk * lo
    routed = jnp.concatenate(his + los, axis=1)
    gt2 = mod_ref[0][:, 5 * D_MODEL:6 * D_MODEL]
    x = xa_ref[...] + gt2 * routed
    ms = jnp.mean(x * x, axis=-1, keepdims=True)
    o_ref[...] = x * lax.rsqrt(ms + EPS) * g_ref[...]


def _final_kernel_into(xa_ref, yk_ref, wts_ref, mod_ref, g_ref, prev_ref, o_ref):
    _final_kernel(xa_ref, yk_ref, wts_ref, mod_ref, g_ref, o_ref)


def _final(xa, yk, wts, mod3, g_final, tile_off, n_tok, seq_len, sid0, out_rows, out_tile_off, prev):
    tiles_per_seq = seq_len // T_FIN
    in_specs = [pl.BlockSpec((T_FIN, D_MODEL), lambda i: (i + tile_off, 0)),
                pl.BlockSpec((TOP_K, N_PLANES, T_FIN, LANES), lambda i: (0, 0, i + tile_off, 0)),
                pl.BlockSpec((T_FIN, LANES), lambda i: (i + tile_off, 0)),
                pl.BlockSpec((1, 1, 6 * D_MODEL), lambda i: (sid0 + i // tiles_per_seq, 0, 0)),
                pl.BlockSpec((1, D_MODEL), lambda i: (0, 0))]
    args = [xa, yk, wts, mod3, g_final]
    if prev is not None:
        in_specs.append(pl.BlockSpec(memory_space=pl.ANY))
        args.append(prev)
    return pl.pallas_call(
        _final_kernel if prev is None else _final_kernel_into,
        grid=(n_tok // T_FIN,),
        in_specs=in_specs,
        out_specs=pl.BlockSpec((T_FIN, D_MODEL), lambda i: (i + out_tile_off, 0)),
        out_shape=jax.ShapeDtypeStruct((out_rows, D_MODEL), F32),
        input_output_aliases={} if prev is None else {len(args) - 1: 0},
        compiler_params=pltpu.CompilerParams(dimension_semantics=("arbitrary",),
                                             vmem_limit_bytes=VMEM_LIMIT),
        name="final",
    )(*args)


def _token_groups(bp, sp, bs, ss):
    seqs = [(0, q * sp, sp, q) for q in range(bp)] + [(1, q * ss, ss, bp + q) for q in range(bs)]
    half = (bp * sp + bs * ss) / 2
    groups, cur, acc = [], [], 0
    for src, tok0, seq_len, sid in seqs:
        if cur and cur[-1][0] == src and cur[-1][3] == seq_len and cur[-1][1] + cur[-1][2] * seq_len == tok0:
            cur[-1] = (src, cur[-1][1], cur[-1][2] + 1, seq_len, cur[-1][4])
        else:
            cur.append((src, tok0, 1, seq_len, sid))
        acc += seq_len
        if not groups and acc >= half:
            groups.append(cur)
            cur = []
    if cur:
        groups.append(cur)
    return groups


def kernel(x_prompt, x_sample, c_prompt, c_sample, w_ada, b_ada, g_norm1, w_in, w_pool, b_pool, pool_scale, attn_sink, w_out, g_norm2, w_router, router_bias, w_gate, w_up, w_down, ws_gate, ws_up, ws_down, g_final):
    assert w_ada.shape[0] == 1, "one layer"
    bp, sp, d = x_prompt.shape
    bs, ss, _ = x_sample.shape
    assert d == D_MODEL and bp + bs <= ADA_ROWS
    n_p, n_s = bp * sp, bs * ss
    xp = x_prompt.reshape(n_p, d)
    xs = x_sample.reshape(n_s, d)

    c_all = jnp.concatenate([c_prompt, c_sample, jnp.zeros((ADA_ROWS - bp - bs, d), F32)], axis=0)
    mod3 = _ada(c_all, w_ada[0], b_ada[0]).reshape(ADA_ROWS, 1, 6 * d)

    q0, q1 = POOL_WIDTH, POOL_WIDTH + ATTN_WIDTH
    wq = w_in[0][:, q0:q1].reshape(d, N_KV_HEADS, Q_PER_KV, HEAD_DIM).transpose(0, 2, 1, 3).reshape(d, ATTN_WIDTH)
    w_in_p = jnp.concatenate([w_in[0][:, :q0], wq, w_in[0][:, q1:]], axis=1).astype(BF16)
    wo = w_out[0][q0:].reshape(N_KV_HEADS, Q_PER_KV, HEAD_DIM, d).transpose(1, 0, 2, 3).reshape(ATTN_WIDTH, d)
    w_out_p = jnp.concatenate([w_out[0][:q0], wo], axis=0).astype(BF16)
    wsgu = jnp.concatenate([ws_gate[0], ws_up[0]], axis=1).astype(BF16)
    wr_hi = w_router[0].astype(BF16)
    wr_lo = (w_router[0] - wr_hi.astype(F32)).astype(BF16)
    wr3 = jnp.concatenate([wr_hi, wr_lo, wr_hi], axis=0).T
    g1 = g_norm1[0].reshape(1, d)
    mixer_weights = (
        w_pool[0].astype(BF16), b_pool[0].reshape(1, POOL_WIDTH), pool_scale[0].reshape(1, POOL_WIDTH),
        attn_sink[0].reshape(1, N_HEADS), w_out_p, g_norm2[0].reshape(1, d),
        wr3, router_bias[0].reshape(N_EXPERTS, 1), wsgu, ws_down[0].astype(BF16))
    ch = EXPERT_CHUNK

    groups = _token_groups(bp, sp, bs, ss)
    staged = []
    for segments in groups:
        n = _group_tokens(segments)
        u, q, k, v = _inproj(xp, xs, mod3, g1, w_in_p, segments)
        xa, h2p, idx_t, rank_t, wts, cnt = _mixer(xp, xs, mod3, u, q, k, v, *mixer_weights, segments)
        n_rows = (-(-n * TOP_K // ch) + N_EXPERTS) * ch
        counts = cnt[:, 0].astype(I32)
        padded = (counts + ch - 1) // ch * ch
        pstart = jnp.cumsum(padded) - padded
        didx = _dest(idx_t, rank_t, pstart.astype(F32).reshape(N_EXPERTS, 1), n_rows)
        didx = didx.reshape(TOP_K, N_PLANES * n)
        xs_rows = _sc_scatter_rows(h2p.reshape(N_PLANES * n, LANES), didx, N_PLANES * n_rows)
        staged.append((xa, wts, didx, xs_rows.reshape(N_PLANES, n_rows, LANES), padded // ch, pstart // ch, counts))

    ys = _experts([st[3] for st in staged], w_gate[0], w_up[0], w_down[0],
                  [st[4] for st in staged], [st[5] for st in staged], [st[6] for st in staged])

    gf = g_final.reshape(1, d)
    outs = [None, None]
    out_rows = (n_p, n_s)
    for segments, (xa, wts, didx, xs_rows, _, _, _), ys_rows in zip(groups, staged, ys):
        n = _group_tokens(segments)
        yk = _sc_gather_rows(ys_rows.reshape(N_PLANES * ys_rows.shape[1], LANES), didx)
        yk = yk.reshape(TOP_K, N_PLANES, n, LANES)
        tok = 0
        for src, tok0, n_seqs, seq_len, sid0 in segments:
            n_tok = n_seqs * seq_len
            outs[src] = _final(xa, yk, wts, mod3, gf, tok // T_FIN, n_tok, seq_len, sid0,
                               out_rows[src], tok0 // T_FIN, outs[src])
            tok += n_tok
    return outs[0].reshape(bp, sp, d), outs[1].reshape(bs, ss, d)
```

```python
import functools

import numpy as np
import jax
import jax.numpy as jnp
from jax import lax
from jax.experimental import pallas as pl
from jax.experimental.pallas import tpu as pltpu
from jax.experimental.pallas import tpu_sc as plsc

F32 = jnp.float32
BF16 = jnp.bfloat16
I32 = jnp.int32

D_MODEL = 1024
POOL_WINDOWS = (2, 4, 8, 16)
POOL_WIDTH = 512
POOL_GROUP = 128
N_HEADS = 8
N_KV_HEADS = 2
HEAD_DIM = 64
Q_PER_KV = N_HEADS // N_KV_HEADS
ATTN_WIDTH = N_HEADS * HEAD_DIM
KV_WIDTH = N_KV_HEADS * HEAD_DIM
D_IN_PROJ = POOL_WIDTH + ATTN_WIDTH + 2 * KV_WIDTH
WINDOW = 128
N_EXPERTS = 256
TOP_K = 8
N_EXPERT_GROUPS = 8
GROUP_SIZE = N_EXPERTS // N_EXPERT_GROUPS
TOPK_GROUPS = 4
D_EXPERT = 256
D_SHARED = 256
ROUTED_SCALE = 2.5
EPS = 1e-6
NEG_INF = -1e30

LANES = 128
HALO = 8
N_PLANES = 4
HALF = D_MODEL // 2

T_IN = 1024
T_MIX = 512
ATT_BLOCK = 128
T_DEST = 2048
T_DEST_SLAB = 512
EXPERT_CHUNK = 512
EXPERT_NBUF = 4
T_FIN = 256
SC_ROWS = 128
SC_WORKERS = 32
SC_NBUF = 4
SC_LAG = 2
ADA_ROWS = 16
VMEM_LIMIT = 48 * 1024 * 1024

LOG2E = 1.4426950408889634
ALIBI_SLOPES = tuple(float(2.0 ** (-8.0 * (h + 1) / N_HEADS)) for h in range(N_HEADS))


def _pack_planes(y):
    planes = []
    for j in range(N_PLANES):
        hi = y[:, j * LANES:(j + 1) * LANES].astype(BF16).astype(F32)
        lo = y[:, HALF + j * LANES:HALF + (j + 1) * LANES].astype(BF16).astype(F32)
        hb = lax.bitcast_convert_type(hi, jnp.uint32) & jnp.uint32(0xFFFF0000)
        lb = lax.bitcast_convert_type(lo, jnp.uint32) >> jnp.uint32(16)
        planes.append(lax.bitcast_convert_type(hb | lb, I32))
    return planes


def _unpack_plane(w):
    u = lax.bitcast_convert_type(w, jnp.uint32)
    hi = lax.bitcast_convert_type(u & jnp.uint32(0xFFFF0000), F32)
    lo = lax.bitcast_convert_type(u << jnp.uint32(16), F32)
    return hi, lo


def _ada_kernel(c_ref, w_ref, b_ref, o_ref):
    c = c_ref[...]
    s = c * jax.nn.sigmoid(c)
    o_ref[...] = jnp.dot(s, w_ref[...], precision=lax.Precision.HIGHEST,
                         preferred_element_type=F32) + b_ref[...]


def _ada(c_all, w_ada, b_ada):
    n_out = w_ada.shape[1]
    tn = 1024
    return pl.pallas_call(
        _ada_kernel,
        grid=(n_out // tn,),
        in_specs=[pl.BlockSpec((ADA_ROWS, D_MODEL), lambda j: (0, 0)),
                  pl.BlockSpec((D_MODEL, tn), lambda j: (0, j)),
                  pl.BlockSpec((1, tn), lambda j: (0, j))],
        out_specs=pl.BlockSpec((ADA_ROWS, tn), lambda j: (0, j)),
        out_shape=jax.ShapeDtypeStruct((ADA_ROWS, n_out), F32),
        name="ada",
    )(c_all, w_ada, b_ada.reshape(1, n_out))


def _rms_mod(x, g, scale, shift):
    ms = jnp.mean(x * x, axis=-1, keepdims=True)
    return (x * lax.rsqrt(ms + EPS) * g) * (1.0 + scale) + shift


def _inproj_kernel(meta_ref, xp_ref, xs_ref, mod_ref, g_ref, w_ref, u_ref, q_ref, k_ref, v_ref):
    x = jnp.where(meta_ref[7, pl.program_id(0)] == 0, xp_ref[...], xs_ref[...])
    mod = mod_ref[0]
    h = _rms_mod(x, g_ref[...], mod[:, D_MODEL:2 * D_MODEL], mod[:, 0:D_MODEL])
    z = jnp.dot(h.astype(BF16), w_ref[...], preferred_element_type=F32)
    u_ref[...] = z[:, :POOL_WIDTH]
    q_ref[...] = (z[:, POOL_WIDTH:POOL_WIDTH + ATTN_WIDTH] * (LOG2E * HEAD_DIM ** -0.5)).astype(BF16)
    k_ref[...] = z[:, POOL_WIDTH + ATTN_WIDTH:POOL_WIDTH + ATTN_WIDTH + KV_WIDTH].astype(BF16)
    v_ref[...] = z[:, POOL_WIDTH + ATTN_WIDTH + KV_WIDTH:].astype(BF16)


def _tile_meta(segments, tile):
    rows = []
    for src, tok0, n_seqs, seq_len, sid0 in segments:
        per = seq_len // tile
        for q in range(n_seqs):
            for t in range(per):
                blk = (tok0 + q * seq_len) // tile + t
                rows.append((sid0 + q, int(t == 0), int(t == per - 1), t * tile, seq_len,
                             blk if src == 0 else -1, blk if src == 1 else -1, src))
    meta = np.asarray(rows, np.int32)
    for col in (5, 6):
        known = meta[:, col] >= 0
        if not known.any():
            meta[:, col] = 0
            continue
        last = np.maximum.accumulate(np.where(known, np.arange(len(meta)), -1))
        last = np.where(last < 0, np.argmax(known), last)
        meta[:, col] = meta[last, col]
    return meta.T.copy()


def _group_tokens(segments):
    return sum(n_seqs * seq_len for _, _, n_seqs, seq_len, _ in segments)


def _inproj(xp, xs, mod3, g1, w_in_bf, segments):
    n = _group_tokens(segments)
    t_in = int(np.gcd.reduce([T_IN] + [seq_len for _, _, _, seq_len, _ in segments]))
    meta = jnp.asarray(_tile_meta(segments, t_in))
    tok = lambda w: pl.BlockSpec((t_in, w), lambda i, m: (i, 0))
    return pl.pallas_call(
        _inproj_kernel,
        grid_spec=pltpu.PrefetchScalarGridSpec(
            num_scalar_prefetch=1,
            grid=(n // t_in,),
            in_specs=[
                pl.BlockSpec((t_in, D_MODEL), lambda i, m: (m[5, i], 0)),
                pl.BlockSpec((t_in, D_MODEL), lambda i, m: (m[6, i], 0)),
                pl.BlockSpec((1, 1, 6 * D_MODEL), lambda i, m: (m[0, i], 0, 0)),
                pl.BlockSpec((1, D_MODEL), lambda i, m: (0, 0)),
                pl.BlockSpec((D_MODEL, D_IN_PROJ), lambda i, m: (0, 0)),
            ],
            out_specs=[tok(POOL_WIDTH), tok(ATTN_WIDTH), tok(KV_WIDTH), tok(KV_WIDTH)],
        ),
        out_shape=[jax.ShapeDtypeStruct((n, POOL_WIDTH), F32),
                   jax.ShapeDtypeStruct((n, ATTN_WIDTH), BF16),
                   jax.ShapeDtypeStruct((n, KV_WIDTH), BF16),
                   jax.ShapeDtypeStruct((n, KV_WIDTH), BF16)],
        compiler_params=pltpu.CompilerParams(dimension_semantics=("arbitrary",),
                                             vmem_limit_bytes=VMEM_LIMIT),
        name="inproj",
    )(meta, xp, xs, mod3, g1, w_in_bf)


def _pool_mixer(uext_ref, uc, pos, seq_len, wpool_ref, bpool_ref, pscale_ref):
    outs = []
    for gi, w in enumerate(POOL_WINDOWS):
        c0 = gi * POOL_GROUP
        half = w // 2
        acc = uext_ref[pl.ds(HALO - half, T_MIX), pl.ds(c0, POOL_GROUP)]
        for o in range(-half + 1, half):
            acc = acc + uext_ref[pl.ds(HALO + o, T_MIX), pl.ds(c0, POOL_GROUP)]
        lo = jnp.maximum(pos - half, 0)
        hi = jnp.minimum(pos + half, seq_len)
        cnt = (hi - lo).astype(F32)
        d = acc / cnt - uc[:, c0:c0 + POOL_GROUP]
        y = jnp.dot(d.astype(BF16), wpool_ref[gi], preferred_element_type=F32)
        y = (y + bpool_ref[:, c0:c0 + POOL_GROUP]) * pscale_ref[:, c0:c0 + POOL_GROUP]
        outs.append(y)
    return jnp.concatenate(outs, axis=1)


def _attention_bias():
    t = ATT_BLOCK
    r = np.arange(t)[:, None]
    c = np.arange(3 * t)[None, :]
    dist = np.abs(r - (c - t))
    band = dist <= WINDOW
    out = np.empty((3, N_HEADS, t, 3 * t), np.float32)
    for var, valid in enumerate((band, band & (c >= t), band & (c < 2 * t))):
        for h in range(N_HEADS):
            out[var, h] = np.where(valid, -ALIBI_SLOPES[h] * LOG2E * dist, NEG_INF)
    return out.reshape(3, N_HEADS * t, 3 * t)


def _attention_scores(q, kw):
    t = ATT_BLOCK
    low = lax.broadcasted_iota(I32, (t, LANES), 1) < HEAD_DIM
    zero = jnp.zeros((t, LANES), BF16)
    qs = []
    for h in range(N_HEADS):
        blk = q[:, (h % Q_PER_KV) * LANES:(h % Q_PER_KV + 1) * LANES]
        qs.append(jnp.where(low, blk, zero) if h < Q_PER_KV else jnp.where(low, zero, blk))
    return lax.dot_general(jnp.concatenate(qs, axis=0), kw, (((1,), (1,)), ((), ())),
                           preferred_element_type=F32)


def _attention_probs(s, variant, bias_ref, sink_ref):
    t = ATT_BLOCK
    sh = s + bias_ref[variant]
    sink = jnp.concatenate([jnp.broadcast_to(sink_ref[:, h:h + 1] * LOG2E, (t, 1)) for h in range(N_HEADS)], axis=0)
    m = jnp.maximum(jnp.max(sh, axis=-1, keepdims=True), sink)
    p = jnp.exp2(sh - m)
    den = jnp.sum(p, axis=-1, keepdims=True) + jnp.exp2(sink - m)
    return p.astype(BF16), den


def _attention_out(p, den, vw):
    t = ATT_BLOCK
    low = lax.broadcasted_iota(I32, (t, LANES), 1) < HEAD_DIM
    o = jnp.dot(p, vw, preferred_element_type=F32) / den
    return jnp.concatenate(
        [jnp.where(low, o[cb * t:(cb + 1) * t], o[(Q_PER_KV + cb) * t:(Q_PER_KV + cb + 1) * t])
         for cb in range(Q_PER_KV)], axis=1)


def _route(biased, scores):
    t = biased.shape[1]
    rowf = lax.broadcasted_iota(I32, (N_EXPERTS, t), 0).astype(F32)
    ninf = float("-inf")
    gs = []
    for g in range(N_EXPERT_GROUPS):
        blk = biased[g * GROUP_SIZE:(g + 1) * GROUP_SIZE, :]
        rf = rowf[g * GROUP_SIZE:(g + 1) * GROUP_SIZE, :]
        m1 = jnp.max(blk, axis=0, keepdims=True)
        i1 = jnp.min(jnp.where(blk == m1, rf, float(N_EXPERTS)), axis=0, keepdims=True)
        m2 = jnp.max(jnp.where(rf == i1, ninf, blk), axis=0, keepdims=True)
        gs.append(m1 + m2)
    keep = []
    for g in range(N_EXPERT_GROUPS):
        beat = jnp.zeros((1, t), F32)
        for g2 in range(N_EXPERT_GROUPS):
            if g2 == g:
                continue
            better = (gs[g2] >= gs[g]) if g2 < g else (gs[g2] > gs[g])
            beat = beat + better.astype(F32)
        keep.append(jnp.broadcast_to(beat < float(TOPK_GROUPS), (GROUP_SIZE, t)))
    emask = jnp.concatenate(keep, axis=0)
    masked = jnp.where(emask, biased, NEG_INF)
    idx_rows, w_rows = [], []
    sel_any = jnp.zeros((N_EXPERTS, t), F32)
    for _ in range(TOP_K):
        m = jnp.max(masked, axis=0, keepdims=True)
        ik = jnp.min(jnp.where(masked == m, rowf, float(N_EXPERTS)), axis=0, keepdims=True)
        sel = rowf == ik
        w_rows.append(jnp.sum(jnp.where(sel, scores, 0.0), axis=0, keepdims=True))
        idx_rows.append(ik)
        masked = jnp.where(sel, ninf, masked)
        sel_any = sel_any + sel.astype(F32)
    return idx_rows, w_rows, sel_any, rowf


def _mixer_kernel(meta_ref, xp_ref, xs_ref, mod_ref, up_ref, uc_ref, un_ref, q_ref,
                  kp_ref, kc_ref, kn_ref, vp_ref, vc_ref, vn_ref,
                  wpool_ref, bpool_ref, pscale_ref, sink_ref, wout_ref, g2_ref,
                  wr3_ref, rbias_ref, wsgu_ref, wsd_ref, tri_ref, abias_ref,
                  xa_ref, h2p_ref, idx_ref, rank_ref, wts_ref, cnt_ref,
                  uext_ref, kext_ref, vext_ref, base_ref):
    i = pl.program_id(0)
    first = meta_ref[1, i] == 1
    last = meta_ref[2, i] == 1
    pos0 = meta_ref[3, i]
    seq_len = meta_ref[4, i]
    mod = mod_ref[0]
    gt1 = mod[:, 2 * D_MODEL:3 * D_MODEL]
    sh2 = mod[:, 3 * D_MODEL:4 * D_MODEL]
    sc2 = mod[:, 4 * D_MODEL:5 * D_MODEL]
    gt2 = mod[:, 5 * D_MODEL:6 * D_MODEL]

    @pl.when(i == 0)
    def _():
        base_ref[...] = jnp.zeros_like(base_ref)

    uc = uc_ref[...]
    uext_ref[pl.ds(0, HALO), :] = jnp.where(first, 0.0, up_ref[...])
    uext_ref[pl.ds(HALO, T_MIX), :] = uc
    uext_ref[pl.ds(HALO + T_MIX, HALO), :] = jnp.where(last, 0.0, un_ref[...])
    pos = pos0 + lax.broadcasted_iota(I32, (T_MIX, 1), 0)
    a_pool = _pool_mixer(uext_ref, uc, pos, seq_len, wpool_ref, bpool_ref, pscale_ref)

    t = ATT_BLOCK
    kext_ref[pl.ds(0, t), :] = kp_ref[...]
    kext_ref[pl.ds(t, T_MIX), :] = kc_ref[...]
    kext_ref[pl.ds(t + T_MIX, t), :] = kn_ref[...]
    vext_ref[pl.ds(0, t), :] = vp_ref[...]
    vext_ref[pl.ds(t, T_MIX), :] = vc_ref[...]
    vext_ref[pl.ds(t + T_MIX, t), :] = vn_ref[...]
    n_sub = T_MIX // t
    assert n_sub >= 2, "a query block is never both first and last in its sequence"
    variants = []
    for sub in range(n_sub):
        variant = 0
        if sub == 0:
            variant = jnp.where(first, 1, variant)
        if sub == n_sub - 1:
            variant = jnp.where(last, 2, variant)
        variants.append(variant)
    scores = [_attention_scores(q_ref[pl.ds(sub * t, t), :], kext_ref[pl.ds(sub * t, 3 * t), :])
              for sub in range(n_sub)]
    probs = [_attention_probs(scores[sub], variants[sub], abias_ref, sink_ref) for sub in range(n_sub)]
    attn = [_attention_out(probs[sub][0], probs[sub][1], vext_ref[pl.ds(sub * t, 3 * t), :])
            for sub in range(n_sub)]
    a_attn = jnp.concatenate(attn, axis=0)

    a = jnp.concatenate([a_pool, a_attn], axis=1).astype(BF16)
    mix = jnp.dot(a, wout_ref[...], preferred_element_type=F32)

    x = jnp.where(meta_ref[7, i] == 0, xp_ref[...], xs_ref[...])
    x1 = x + gt1 * mix
    h2 = _rms_mod(x1, g2_ref[...], sc2, sh2)
    h2b = h2.astype(BF16)
    planes = _pack_planes(h2)
    for j in range(N_PLANES):
        h2p_ref[j] = planes[j]
    gu = jnp.dot(h2b, wsgu_ref[...], preferred_element_type=F32)
    gate, up = gu[:, :D_SHARED], gu[:, D_SHARED:]
    act = (gate * jax.nn.sigmoid(gate)) * up
    shared = jnp.dot(act.astype(BF16), wsd_ref[...], preferred_element_type=F32)
    xa_ref[...] = x1 + gt2 * shared
    h2lo = (h2 - h2b.astype(F32)).astype(BF16)
    logits = lax.dot_general(wr3_ref[...], jnp.concatenate([h2b, h2b, h2lo], axis=1),
                             (((1,), (1,)), ((), ())), preferred_element_type=F32)
    scores = jax.nn.sigmoid(logits)
    biased = scores + rbias_ref[...]
    routed = [_route(biased[:, c * LANES:(c + 1) * LANES], scores[:, c * LANES:(c + 1) * LANES])
              for c in range(T_MIX // LANES)]
    for c, (idx_rows, w_rows, _, _) in enumerate(routed):
        wsum = w_rows[0]
        for wr in w_rows[1:]:
            wsum = wsum + wr
        wts_t = jnp.concatenate([wr / wsum * ROUTED_SCALE for wr in w_rows], axis=0)
        wpad = jnp.concatenate([wts_t, jnp.zeros((LANES - TOP_K, LANES), F32)], axis=0)
        wts_ref[pl.ds(c * LANES, LANES), :] = wpad.T
        idx_ref[:, pl.ds(c * LANES, LANES)] = jnp.concatenate(idx_rows, axis=0).astype(I32)
    sel_any = jnp.concatenate([r[2] for r in routed], axis=1)
    pref = jnp.dot(sel_any.astype(BF16), tri_ref[...], preferred_element_type=F32)
    before = base_ref[...] + pref[:, :T_MIX]
    for c, (idx_rows, _, _, rowf) in enumerate(routed):
        bc = before[:, c * LANES:(c + 1) * LANES]
        ranks = [jnp.sum(jnp.where(rowf == ik, bc, 0.0), axis=0, keepdims=True) for ik in idx_rows]
        rank_ref[:, pl.ds(c * LANES, LANES)] = jnp.concatenate(ranks, axis=0).astype(I32)
    new_base = base_ref[...] + pref[:, T_MIX:]
    base_ref[...] = new_base
    cnt_ref[...] = new_base


def _mixer(xp, xs, mod3, u, q, k, v, wpool_bf, bpool, pscale, sink, wout_bf, g2, wr3, rbias,
           wsgu_bf, wsd_bf, segments):
    n = _group_tokens(segments)
    t = T_MIX
    nt = n // t
    ab = ATT_BLOCK
    meta = jnp.asarray(_tile_meta(segments, t))
    tri = np.concatenate([np.triu(np.ones((t, t), np.float32), 1), np.ones((t, t), np.float32)], axis=1)
    tri = jnp.asarray(tri, BF16)
    abias = jnp.asarray(_attention_bias())
    cur = lambda w: pl.BlockSpec((t, w), lambda i, m: (i, 0))
    prev = lambda w: pl.BlockSpec((ab, w), lambda i, m: (jnp.maximum(i * (t // ab) - 1, 0), 0))
    nxt = lambda w: pl.BlockSpec((ab, w), lambda i, m: (jnp.minimum((i + 1) * (t // ab), n // ab - 1), 0))
    full = lambda a: pl.BlockSpec(a.shape, lambda i, m: (0,) * a.ndim)
    hb = t // HALO
    in_specs = [
        pl.BlockSpec((t, D_MODEL), lambda i, m: (m[5, i], 0)),
        pl.BlockSpec((t, D_MODEL), lambda i, m: (m[6, i], 0)),
        pl.BlockSpec((1, 1, 6 * D_MODEL), lambda i, m: (m[0, i], 0, 0)),
        pl.BlockSpec((HALO, POOL_WIDTH), lambda i, m: (jnp.maximum(i * hb - 1, 0), 0)),
        cur(POOL_WIDTH),
        pl.BlockSpec((HALO, POOL_WIDTH), lambda i, m: (jnp.minimum((i + 1) * hb, n // HALO - 1), 0)),
        cur(ATTN_WIDTH),
        prev(KV_WIDTH), cur(KV_WIDTH), nxt(KV_WIDTH),
        prev(KV_WIDTH), cur(KV_WIDTH), nxt(KV_WIDTH),
        full(wpool_bf), full(bpool), full(pscale), full(sink), full(wout_bf), full(g2),
        full(wr3), full(rbias), full(wsgu_bf), full(wsd_bf), full(tri), full(abias),
    ]
    out_specs = [
        cur(D_MODEL),
        pl.BlockSpec((N_PLANES, t, LANES), lambda i, m: (0, i, 0)),
        pl.BlockSpec((TOP_K, t), lambda i, m: (0, i)),
        pl.BlockSpec((TOP_K, t), lambda i, m: (0, i)),
        cur(LANES),
        pl.BlockSpec((N_EXPERTS, t), lambda i, m: (0, 0)),
    ]
    out_shape = [
        jax.ShapeDtypeStruct((n, D_MODEL), F32),
        jax.ShapeDtypeStruct((N_PLANES, n, LANES), I32),
        jax.ShapeDtypeStruct((TOP_K, n), I32),
        jax.ShapeDtypeStruct((TOP_K, n), I32),
        jax.ShapeDtypeStruct((n, LANES), F32),
        jax.ShapeDtypeStruct((N_EXPERTS, t), F32),
    ]
    return pl.pallas_call(
        _mixer_kernel,
        grid_spec=pltpu.PrefetchScalarGridSpec(
            num_scalar_prefetch=1, grid=(nt,), in_specs=in_specs, out_specs=out_specs,
            scratch_shapes=[pltpu.VMEM((t + 2 * HALO, POOL_WIDTH), F32),
                            pltpu.VMEM((t + 2 * ab, KV_WIDTH), BF16),
                            pltpu.VMEM((t + 2 * ab, KV_WIDTH), BF16),
                            pltpu.VMEM((N_EXPERTS, t), F32)]),
        out_shape=out_shape,
        compiler_params=pltpu.CompilerParams(dimension_semantics=("arbitrary",),
                                             vmem_limit_bytes=VMEM_LIMIT),
        name="mixer",
    )(meta, xp, xs, mod3, u, u, u, q, k, k, k, v, v, v, wpool_bf, bpool, pscale, sink, wout_bf, g2,
      wr3, rbias, wsgu_bf, wsd_bf, tri, abias)


def _dest_kernel(idx_ref, rank_ref, pstart_ref, o_ref, *, n_rows):
    slab = min(T_DEST_SLAB, idx_ref.shape[1])
    rowi = lax.broadcasted_iota(I32, (N_EXPERTS, slab), 0)
    pstart = pstart_ref[...]
    for c in range(idx_ref.shape[1] // slab):
        lanes = pl.ds(c * slab, slab)
        rows = []
        for kk in range(TOP_K):
            sel = rowi == idx_ref[kk:kk + 1, lanes]
            start = jnp.sum(jnp.where(sel, pstart, 0.0), axis=0, keepdims=True)
            rows.append(start.astype(I32) + rank_ref[kk:kk + 1, lanes])
        dest = jnp.concatenate(rows, axis=0)
        for j in range(N_PLANES):
            o_ref[:, j, lanes] = dest + j * n_rows


def _dest(idx_t, rank_t, pstart_col, n_rows):
    n = idx_t.shape[1]
    td = int(np.gcd(n, T_DEST))
    return pl.pallas_call(
        functools.partial(_dest_kernel, n_rows=n_rows),
        grid=(n // td,),
        in_specs=[pl.BlockSpec((TOP_K, td), lambda i: (0, i)),
                  pl.BlockSpec((TOP_K, td), lambda i: (0, i)),
                  pl.BlockSpec((N_EXPERTS, 1), lambda i: (0, 0))],
        out_specs=pl.BlockSpec((TOP_K, N_PLANES, td), lambda i: (0, 0, i)),
        out_shape=jax.ShapeDtypeStruct((TOP_K, N_PLANES, n), I32),
        name="dest",
    )(idx_t, rank_t, pstart_col)


def _sc_mesh():
    return plsc.VectorSubcoreMesh(core_axis_name="c", subcore_axis_name="s")


def _sc_worker():
    return lax.axis_index("s") * 2 + lax.axis_index("c")


def _sc_scatter_rows(table, didx, n_out_rows):
    m = table.shape[0]
    per_worker = m // SC_ROWS // SC_WORKERS
    assert per_worker % 2 == 0

    @functools.partial(
        pl.kernel, mesh=_sc_mesh(),
        out_type=jax.ShapeDtypeStruct((n_out_rows, LANES), I32),
        scratch_types=[pltpu.VMEM((2, SC_ROWS, LANES), I32), pltpu.VMEM((2, TOP_K, SC_ROWS), I32),
                       pltpu.SemaphoreType.DMA((2,)), pltpu.SemaphoreType.DMA((2,))])
    def run(table_hbm, didx_hbm, out_hbm, rows, idx, lsem, ssem):
        wid = _sc_worker()

        def loads(step, p):
            off = pl.multiple_of((wid * per_worker + step) * SC_ROWS, SC_ROWS)
            return (pltpu.make_async_copy(table_hbm.at[pl.ds(off, SC_ROWS)], rows.at[p], lsem.at[p]),
                    pltpu.make_async_copy(didx_hbm.at[:, pl.ds(off, SC_ROWS)], idx.at[p], lsem.at[p]))

        def scatter(p, kk):
            return pltpu.make_async_copy(rows.at[p], out_hbm.at[idx.at[p].at[kk]], ssem.at[p])

        for cp in loads(0, 0):
            cp.start()

        @pl.loop(0, per_worker, step=2)
        def _(s0):
            for p in range(2):
                step = s0 + p
                for cp in loads(step, p):
                    cp.wait()
                for kk in range(TOP_K):
                    scatter(p, kk).start()

                @pl.when(step > 0)
                def _():
                    for kk in range(TOP_K):
                        scatter(1 - p, kk).wait()

                @pl.when(step + 1 < per_worker)
                def _():
                    for cp in loads(step + 1, 1 - p):
                        cp.start()

        for kk in range(TOP_K):
            scatter(1, kk).wait()

    return run(table, didx)


def _sc_gather_rows(table, didx):
    m = didx.shape[1]
    per_worker = m // SC_ROWS // SC_WORKERS
    assert per_worker % 2 == 0 and (2 * TOP_K) % SC_NBUF == 0
    items = 2 * TOP_K

    @functools.partial(
        pl.kernel, mesh=_sc_mesh(),
        out_type=jax.ShapeDtypeStruct((TOP_K, m, LANES), I32),
        scratch_types=[pltpu.VMEM((SC_NBUF, SC_ROWS, LANES), I32), pltpu.VMEM((2, TOP_K, SC_ROWS), I32),
                       pltpu.SemaphoreType.DMA((SC_NBUF,)), pltpu.SemaphoreType.DMA((SC_NBUF,)),
                       pltpu.SemaphoreType.DMA((2,))])
    def run(table_hbm, didx_hbm, out_hbm, rows, idx, gsem, wsem, isem):
        wid = _sc_worker()

        def off_of(step):
            return pl.multiple_of((wid * per_worker + step) * SC_ROWS, SC_ROWS)

        def idx_copy(step, p):
            return pltpu.make_async_copy(didx_hbm.at[:, pl.ds(off_of(step), SC_ROWS)], idx.at[p], isem.at[p])

        def gather(j):
            r = j % SC_NBUF
            return pltpu.make_async_copy(table_hbm.at[idx.at[j // TOP_K].at[j % TOP_K]], rows.at[r], gsem.at[r])

        def write(s0, j):
            r = j % SC_NBUF
            return pltpu.make_async_copy(rows.at[r], out_hbm.at[j % TOP_K, pl.ds(off_of(s0 + j // TOP_K), SC_ROWS)],
                                         wsem.at[r])

        def retire_write(s0, j):
            if j >= 0:
                write(s0, j).wait()
            else:
                @pl.when(s0 > 0)
                def _():
                    write(s0 - 2, j + items).wait()

        def finish_read(s0, j):
            if j >= 0:
                gather(j).wait()
                write(s0, j).start()
            else:
                @pl.when(s0 > 0)
                def _():
                    gather(j + items).wait()
                    write(s0 - 2, j + items).start()

        idx_copy(0, 0).start()

        @pl.loop(0, per_worker, step=2)
        def _(s0):
            for j in range(items):
                p, kk = j // TOP_K, j % TOP_K
                if kk == 0:
                    idx_copy(s0 + p, p).wait()
                retire_write(s0, j - SC_NBUF)
                gather(j).start()
                finish_read(s0, j - SC_LAG)
                if kk == SC_LAG:
                    @pl.when(s0 + p + 1 < per_worker)
                    def _():
                        idx_copy(s0 + p + 1, 1 - p).start()

        last = per_worker - 2
        for j in range(items - SC_LAG, items):
            gather(j).wait()
            write(last, j).start()
        for j in range(items - SC_NBUF, items):
            write(last, j).wait()

    return run(table, didx)


def _experts_kernel(c0_ref, ntot_ref, tsrc_ref, trow_ref, tvalid_ref, total_ref, *refs, n_groups):
    x_hbms = refs[:n_groups]
    wg_ref, wu_ref, wd_ref = refs[n_groups:n_groups + 3]
    y_hbms = refs[n_groups + 3:2 * n_groups + 3]
    wgu_s, wd_s, xbuf, ybuf, xsem, ysem = refs[2 * n_groups + 3:]
    e = pl.program_id(0)
    nb = xbuf.shape[0]
    ch = xbuf.shape[2]
    total = total_ref[0]

    def x_copy(c, g, row):
        slot = lax.rem(c, nb)
        return pltpu.make_async_copy(x_hbms[g].at[:, pl.ds(row, ch), :], xbuf.at[slot], xsem.at[slot])

    def y_copy(c, g, row):
        slot = lax.rem(c, nb)
        return pltpu.make_async_copy(ybuf.at[slot], y_hbms[g].at[:, pl.ds(row, ch), :], ysem.at[slot])

    def start(copy, c):
        row = pl.multiple_of(trow_ref[c], ch)
        for g in range(n_groups):
            @pl.when(tsrc_ref[c] == g)
            def _():
                copy(c, g, row).start()

    def wait(copy, c):
        copy(c, 0, 0).wait()

    @pl.when(e == 0)
    def _():
        for c in range(nb - 1):
            @pl.when(c < total)
            def _():
                start(x_copy, c)

    for j in range(N_PLANES):
        hi = pl.ds(j * LANES, LANES)
        lo = pl.ds(HALF + j * LANES, LANES)
        r_hi = pl.ds(2 * j * LANES, LANES)
        r_lo = pl.ds((2 * j + 1) * LANES, LANES)
        wgu_s[r_hi, pl.ds(0, D_EXPERT)] = wg_ref[0, hi, :].astype(BF16)
        wgu_s[r_lo, pl.ds(0, D_EXPERT)] = wg_ref[0, lo, :].astype(BF16)
        wgu_s[r_hi, pl.ds(D_EXPERT, D_EXPERT)] = wu_ref[0, hi, :].astype(BF16)
        wgu_s[r_lo, pl.ds(D_EXPERT, D_EXPERT)] = wu_ref[0, lo, :].astype(BF16)
    wd_s[...] = wd_ref[0].astype(BF16)

    c0 = c0_ref[e]

    def chunk(i, carry):
        c = c0 + i
        slot = lax.rem(c, nb)

        @pl.when(c + nb - 1 < total)
        def _():
            start(x_copy, c + nb - 1)

        wait(x_copy, c)

        @pl.when(c >= nb)
        def _():
            wait(y_copy, c - nb)

        live = lax.broadcasted_iota(I32, (ch, LANES), 0) < tvalid_ref[c]
        xs = []
        for j in range(N_PLANES):
            hi, lo = _unpack_plane(jnp.where(live, xbuf[slot, j], 0))
            xs += [hi.astype(BF16), lo.astype(BF16)]
        acc = jnp.dot(jnp.concatenate(xs, axis=1), wgu_s[...], preferred_element_type=F32)
        gate, up = acc[:, :D_EXPERT], acc[:, D_EXPERT:]
        act = (gate * jax.nn.sigmoid(gate)) * up
        y = jnp.dot(act.astype(BF16), wd_s[...], preferred_element_type=F32)
        planes = _pack_planes(y)
        for j in range(N_PLANES):
            ybuf[slot, j] = planes[j]
        start(y_copy, c)
        return carry

    lax.fori_loop(0, ntot_ref[e], chunk, 0)

    @pl.when(e == pl.num_programs(0) - 1)
    def _():
        for back in range(nb, 0, -1):
            @pl.when(total >= back)
            def _():
                wait(y_copy, total - back)


def _chunk_tables(nchunks, cstarts, counts, max_chunks):
    ch = EXPERT_CHUNK
    ntot = sum(nchunks)
    cum = jnp.cumsum(ntot)
    c0 = cum - ntot
    c = jnp.arange(max_chunks, dtype=I32)
    e = jnp.minimum(jnp.sum((c[:, None] >= cum[None, :]).astype(I32), axis=1), N_EXPERTS - 1)
    own = e[:, None] == jnp.arange(N_EXPERTS, dtype=I32)[None, :]
    pick = lambda v: jnp.sum(jnp.where(own, v[None, :], 0), axis=1)
    i = c - pick(c0)
    src = jnp.zeros_like(c)
    row = jnp.zeros_like(c)
    valid = jnp.zeros_like(c)
    lo = jnp.zeros_like(c)
    for g in range(len(nchunks)):
        n_g = pick(nchunks[g])
        inside = (i >= lo) & (i < lo + n_g) & (c < cum[-1])
        loc = i - lo
        src = jnp.where(inside, g, src)
        row = jnp.where(inside, (pick(cstarts[g]) + loc) * ch, row)
        valid = jnp.where(inside, jnp.clip(pick(counts[g]) - loc * ch, 0, ch), valid)
        lo = lo + n_g
    return c0.astype(I32), ntot.astype(I32), src, row, valid, cum[-1:].astype(I32)


def _experts(xs_planes, w_gate, w_up, w_down, nchunks, cstarts, counts):
    ng = len(xs_planes)
    ch = EXPERT_CHUNK
    tables = _chunk_tables(nchunks, cstarts, counts, sum(x.shape[1] // ch for x in xs_planes))
    w_map = lambda e, *_: (e, 0, 0)
    any_space = pl.BlockSpec(memory_space=pl.ANY)
    return pl.pallas_call(
        functools.partial(_experts_kernel, n_groups=ng),
        grid_spec=pltpu.PrefetchScalarGridSpec(
            num_scalar_prefetch=6, grid=(N_EXPERTS,),
            in_specs=[any_space] * ng + [pl.BlockSpec((1, D_MODEL, D_EXPERT), w_map),
                                         pl.BlockSpec((1, D_MODEL, D_EXPERT), w_map),
                                         pl.BlockSpec((1, D_EXPERT, D_MODEL), w_map)],
            out_specs=[any_space] * ng,
            scratch_shapes=[pltpu.VMEM((D_MODEL, 2 * D_EXPERT), BF16),
                            pltpu.VMEM((D_EXPERT, D_MODEL), BF16),
                            pltpu.VMEM((EXPERT_NBUF, N_PLANES, ch, LANES), I32),
                            pltpu.VMEM((EXPERT_NBUF, N_PLANES, ch, LANES), I32),
                            pltpu.SemaphoreType.DMA((EXPERT_NBUF,)),
                            pltpu.SemaphoreType.DMA((EXPERT_NBUF,))]),
        out_shape=[jax.ShapeDtypeStruct(x.shape, I32) for x in xs_planes],
        compiler_params=pltpu.CompilerParams(dimension_semantics=("arbitrary",),
                                             vmem_limit_bytes=VMEM_LIMIT),
        name="experts",
    )(*tables, *xs_planes, w_gate, w_up, w_down)


def _final_kernel(xa_ref, yk_ref, wts_ref, mod_ref, g_ref, o_ref):
    wts = wts_ref[...]
    his = [jnp.zeros((T_FIN, LANES), F32) for _ in range(N_PLANES)]
    los = [jnp.zeros((T_FIN, LANES), F32) for _ in range(N_PLANES)]
    for kk in range(TOP_K):
        wk = wts[:, kk:kk + 1]
        for j in range(N_PLANES):
            hi, lo = _unpack_plane(yk_ref[kk, j])
            his[j] = his[j] + wk * hi
            los[j] = los[j] + wk * lo
    routed = jnp.concatenate(his + los, axis=1)
    gt2 = mod_ref[0][:, 5 * D_MODEL:6 * D_MODEL]
    x = xa_ref[...] + gt2 * routed
    ms = jnp.mean(x * x, axis=-1, keepdims=True)
    o_ref[...] = x * lax.rsqrt(ms + EPS) * g_ref[...]


def _final_kernel_into(xa_ref, yk_ref, wts_ref, mod_ref, g_ref, prev_ref, o_ref):
    _final_kernel(xa_ref, yk_ref, wts_ref, mod_ref, g_ref, o_ref)


def _final(xa, yk, wts, mod3, g_final, tile_off, n_tok, seq_len, sid0, out_rows, out_tile_off, prev):
    tiles_per_seq = seq_len // T_FIN
    in_specs = [pl.BlockSpec((T_FIN, D_MODEL), lambda i: (i + tile_off, 0)),
                pl.BlockSpec((TOP_K, N_PLANES, T_FIN, LANES), lambda i: (0, 0, i + tile_off, 0)),
                pl.BlockSpec((T_FIN, LANES), lambda i: (i + tile_off, 0)),
                pl.BlockSpec((1, 1, 6 * D_MODEL), lambda i: (sid0 + i // tiles_per_seq, 0, 0)),
                pl.BlockSpec((1, D_MODEL), lambda i: (0, 0))]
    args = [xa, yk, wts, mod3, g_final]
    if prev is not None:
        in_specs.append(pl.BlockSpec(memory_space=pl.ANY))
        args.append(prev)
    return pl.pallas_call(
        _final_kernel if prev is None else _final_kernel_into,
        grid=(n_tok // T_FIN,),
        in_specs=in_specs,
        out_specs=pl.BlockSpec((T_FIN, D_MODEL), lambda i: (i + out_tile_off, 0)),
        out_shape=jax.ShapeDtypeStruct((out_rows, D_MODEL), F32),
        input_output_aliases={} if prev is None else {len(args) - 1: 0},
        compiler_params=pltpu.CompilerParams(dimension_semantics=("arbitrary",),
                                             vmem_limit_bytes=VMEM_LIMIT),
        name="final",
    )(*args)


def _token_groups(bp, sp, bs, ss):
    return [[(0, 0, bp, sp, 0), (1, 0, bs, ss, bp)]]


def kernel(x_prompt, x_sample, c_prompt, c_sample, w_ada, b_ada, g_norm1, w_in, w_pool, b_pool, pool_scale, attn_sink, w_out, g_norm2, w_router, router_bias, w_gate, w_up, w_down, ws_gate, ws_up, ws_down, g_final):
    assert w_ada.shape[0] == 1, "one layer"
    bp, sp, d = x_prompt.shape
    bs, ss, _ = x_sample.shape
    assert d == D_MODEL and bp + bs <= ADA_ROWS
    n_p, n_s = bp * sp, bs * ss
    xp = x_prompt.reshape(n_p, d)
    xs = x_sample.reshape(n_s, d)

    c_all = jnp.concatenate([c_prompt, c_sample, jnp.zeros((ADA_ROWS - bp - bs, d), F32)], axis=0)
    mod3 = _ada(c_all, w_ada[0], b_ada[0]).reshape(ADA_ROWS, 1, 6 * d)

    q0, q1 = POOL_WIDTH, POOL_WIDTH + ATTN_WIDTH
    wq = w_in[0][:, q0:q1].reshape(d, N_KV_HEADS, Q_PER_KV, HEAD_DIM).transpose(0, 2, 1, 3).reshape(d, ATTN_WIDTH)
    w_in_p = jnp.concatenate([w_in[0][:, :q0], wq, w_in[0][:, q1:]], axis=1).astype(BF16)
    wo = w_out[0][q0:].reshape(N_KV_HEADS, Q_PER_KV, HEAD_DIM, d).transpose(1, 0, 2, 3).reshape(ATTN_WIDTH, d)
    w_out_p = jnp.concatenate([w_out[0][:q0], wo], axis=0).astype(BF16)
    wsgu = jnp.concatenate([ws_gate[0], ws_up[0]], axis=1).astype(BF16)
    wr_hi = w_router[0].astype(BF16)
    wr_lo = (w_router[0] - wr_hi.astype(F32)).astype(BF16)
    wr3 = jnp.concatenate([wr_hi, wr_lo, wr_hi], axis=0).T
    g1 = g_norm1[0].reshape(1, d)
    mixer_weights = (
        w_pool[0].astype(BF16), b_pool[0].reshape(1, POOL_WIDTH), pool_scale[0].reshape(1, POOL_WIDTH),
        attn_sink[0].reshape(1, N_HEADS), w_out_p, g_norm2[0].reshape(1, d),
        wr3, router_bias[0].reshape(N_EXPERTS, 1), wsgu, ws_down[0].astype(BF16))
    ch = EXPERT_CHUNK

    groups = _token_groups(bp, sp, bs, ss)
    staged = []
    for segments in groups:
        n = _group_tokens(segments)
        u, q, k, v = _inproj(xp, xs, mod3, g1, w_in_p, segments)
        xa, h2p, idx_t, rank_t, wts, cnt = _mixer(xp, xs, mod3, u, q, k, v, *mixer_weights, segments)
        n_rows = (-(-n * TOP_K // ch) + N_EXPERTS) * ch
        counts = cnt[:, 0].astype(I32)
        padded = (counts + ch - 1) // ch * ch
        pstart = jnp.cumsum(padded) - padded
        didx = _dest(idx_t, rank_t, pstart.astype(F32).reshape(N_EXPERTS, 1), n_rows)
        didx = didx.reshape(TOP_K, N_PLANES * n)
        xs_rows = _sc_scatter_rows(h2p.reshape(N_PLANES * n, LANES), didx, N_PLANES * n_rows)
        staged.append((xa, wts, didx, xs_rows.reshape(N_PLANES, n_rows, LANES), padded // ch, pstart // ch, counts))

    ys = _experts([st[3] for st in staged], w_gate[0], w_up[0], w_down[0],
                  [st[4] for st in staged], [st[5] for st in staged], [st[6] for st in staged])

    gf = g_final.reshape(1, d)
    outs = [None, None]
    out_rows = (n_p, n_s)
    for segments, (xa, wts, didx, xs_rows, _, _, _), ys_rows in zip(groups, staged, ys):
        n = _group_tokens(segments)
        yk = _sc_gather_rows(ys_rows.reshape(N_PLANES * ys_rows.shape[1], LANES), didx)
        yk = yk.reshape(TOP_K, N_PLANES, n, LANES)
        tok = 0
        for src, tok0, n_seqs, seq_len, sid0 in segments:
            n_tok = n_seqs * seq_len
            outs[src] = _final(xa, yk, wts, mod3, gf, tok // T_FIN, n_tok, seq_len, sid0,
                               out_rows[src], tok0 // T_FIN, outs[src])
            tok += n_tok
    return outs[0].reshape(bp, sp, d), outs[1].reshape(bs, ss, d)
```

```python
import functools

import numpy as np
import jax
import jax.numpy as jnp
from jax import lax
from jax.experimental import pallas as pl
from jax.experimental.pallas import tpu as pltpu
from jax.experimental.pallas import tpu_sc as plsc

F32 = jnp.float32
BF16 = jnp.bfloat16
I32 = jnp.int32

D_MODEL = 1024
POOL_WINDOWS = (2, 4, 8, 16)
POOL_WIDTH = 512
POOL_GROUP = 128
N_HEADS = 8
N_KV_HEADS = 2
HEAD_DIM = 64
Q_PER_KV = N_HEADS // N_KV_HEADS
ATTN_WIDTH = N_HEADS * HEAD_DIM
KV_WIDTH = N_KV_HEADS * HEAD_DIM
D_IN_PROJ = POOL_WIDTH + ATTN_WIDTH + 2 * KV_WIDTH
WINDOW = 128
N_EXPERTS = 256
TOP_K = 8
N_EXPERT_GROUPS = 8
GROUP_SIZE = N_EXPERTS // N_EXPERT_GROUPS
TOPK_GROUPS = 4
D_EXPERT = 256
D_SHARED = 256
ROUTED_SCALE = 2.5
EPS = 1e-6
NEG_INF = -1e30

LANES = 128
HALO = 8
N_PLANES = 4
HALF = D_MODEL // 2

T_IN = 1024
T_MIX = 512
ATT_BLOCK = 128
T_DEST = 2048
T_DEST_SLAB = 512
EXPERT_CHUNK = 512
EXPERT_NBUF = 4
T_FIN = 256
SC_ROWS = 128
SC_WORKERS = 32
SC_NBUF = 4
SC_LAG = 2
ADA_ROWS = 16
VMEM_LIMIT = 48 * 1024 * 1024

LOG2E = 1.4426950408889634
ALIBI_SLOPES = tuple(float(2.0 ** (-8.0 * (h + 1) / N_HEADS)) for h in range(N_HEADS))


def _pack_planes(y):
    planes = []
    for j in range(N_PLANES):
        hi = y[:, j * LANES:(j + 1) * LANES].astype(BF16).astype(F32)
        lo = y[:, HALF + j * LANES:HALF + (j + 1) * LANES].astype(BF16).astype(F32)
        hb = lax.bitcast_convert_type(hi, jnp.uint32) & jnp.uint32(0xFFFF0000)
        lb = lax.bitcast_convert_type(lo, jnp.uint32) >> jnp.uint32(16)
        planes.append(lax.bitcast_convert_type(hb | lb, I32))
    return planes


def _unpack_plane(w):
    u = lax.bitcast_convert_type(w, jnp.uint32)
    hi = lax.bitcast_convert_type(u & jnp.uint32(0xFFFF0000), F32)
    lo = lax.bitcast_convert_type(u << jnp.uint32(16), F32)
    return hi, lo


def _ada_kernel(c_ref, w_ref, b_ref, o_ref):
    c = c_ref[...]
    s = c * jax.nn.sigmoid(c)
    o_ref[...] = jnp.dot(s, w_ref[...], precision=lax.Precision.HIGHEST,
                         preferred_element_type=F32) + b_ref[...]


def _ada(c_all, w_ada, b_ada):
    n_out = w_ada.shape[1]
    tn = 1024
    return pl.pallas_call(
        _ada_kernel,
        grid=(n_out // tn,),
        in_specs=[pl.BlockSpec((ADA_ROWS, D_MODEL), lambda j: (0, 0)),
                  pl.BlockSpec((D_MODEL, tn), lambda j: (0, j)),
                  pl.BlockSpec((1, tn), lambda j: (0, j))],
        out_specs=pl.BlockSpec((ADA_ROWS, tn), lambda j: (0, j)),
        out_shape=jax.ShapeDtypeStruct((ADA_ROWS, n_out), F32),
        name="ada",
    )(c_all, w_ada, b_ada.reshape(1, n_out))


def _rms_mod(x, g, scale, shift):
    ms = jnp.mean(x * x, axis=-1, keepdims=True)
    return (x * lax.rsqrt(ms + EPS) * g) * (1.0 + scale) + shift


def _inproj_kernel(meta_ref, xp_ref, xs_ref, mod_ref, g_ref, w_ref, u_ref, q_ref, k_ref, v_ref):
    x = jnp.where(meta_ref[7, pl.program_id(0)] == 0, xp_ref[...], xs_ref[...])
    mod = mod_ref[0]
    h = _rms_mod(x, g_ref[...], mod[:, D_MODEL:2 * D_MODEL], mod[:, 0:D_MODEL])
    z = jnp.dot(h.astype(BF16), w_ref[...], preferred_element_type=F32)
    u_ref[...] = z[:, :POOL_WIDTH]
    q_ref[...] = (z[:, POOL_WIDTH:POOL_WIDTH + ATTN_WIDTH] * (LOG2E * HEAD_DIM ** -0.5)).astype(BF16)
    k_ref[...] = z[:, POOL_WIDTH + ATTN_WIDTH:POOL_WIDTH + ATTN_WIDTH + KV_WIDTH].astype(BF16)
    v_ref[...] = z[:, POOL_WIDTH + ATTN_WIDTH + KV_WIDTH:].astype(BF16)


def _tile_meta(segments, tile):
    rows = []
    for src, tok0, n_seqs, seq_len, sid0 in segments:
        per = seq_len // tile
        for q in range(n_seqs):
            for t in range(per):
                blk = (tok0 + q * seq_len) // tile + t
                rows.append((sid0 + q, int(t == 0), int(t == per - 1), t * tile, seq_len,
                             blk if src == 0 else -1, blk if src == 1 else -1, src))
    meta = np.asarray(rows, np.int32)
    for col in (5, 6):
        known = meta[:, col] >= 0
        if not known.any():
            meta[:, col] = 0
            continue
        last = np.maximum.accumulate(np.where(known, np.arange(len(meta)), -1))
        last = np.where(last < 0, np.argmax(known), last)
        meta[:, col] = meta[last, col]
    return meta.T.copy()


def _group_tokens(segments):
    return sum(n_seqs * seq_len for _, _, n_seqs, seq_len, _ in segments)


def _inproj(xp, xs, mod3, g1, w_in_bf, segments):
    n = _group_tokens(segments)
    t_in = int(np.gcd.reduce([T_IN] + [seq_len for _, _, _, seq_len, _ in segments]))
    meta = jnp.asarray(_tile_meta(segments, t_in))
    tok = lambda w: pl.BlockSpec((t_in, w), lambda i, m: (i, 0))
    return pl.pallas_call(
        _inproj_kernel,
        grid_spec=pltpu.PrefetchScalarGridSpec(
            num_scalar_prefetch=1,
            grid=(n // t_in,),
            in_specs=[
                pl.BlockSpec((t_in, D_MODEL), lambda i, m: (m[5, i], 0)),
                pl.BlockSpec((t_in, D_MODEL), lambda i, m: (m[6, i], 0)),
                pl.BlockSpec((1, 1, 6 * D_MODEL), lambda i, m: (m[0, i], 0, 0)),
                pl.BlockSpec((1, D_MODEL), lambda i, m: (0, 0)),
                pl.BlockSpec((D_MODEL, D_IN_PROJ), lambda i, m: (0, 0)),
            ],
            out_specs=[tok(POOL_WIDTH), tok(ATTN_WIDTH), tok(KV_WIDTH), tok(KV_WIDTH)],
        ),
        out_shape=[jax.ShapeDtypeStruct((n, POOL_WIDTH), F32),
                   jax.ShapeDtypeStruct((n, ATTN_WIDTH), BF16),
                   jax.ShapeDtypeStruct((n, KV_WIDTH), BF16),
                   jax.ShapeDtypeStruct((n, KV_WIDTH), BF16)],
        compiler_params=pltpu.CompilerParams(dimension_semantics=("arbitrary",),
                                             vmem_limit_bytes=VMEM_LIMIT),
        name="inproj",
    )(meta, xp, xs, mod3, g1, w_in_bf)


def _pool_mixer(uext_ref, uc, pos, seq_len, wpool_ref, bpool_ref, pscale_ref):
    outs = []
    for gi, w in enumerate(POOL_WINDOWS):
        c0 = gi * POOL_GROUP
        half = w // 2
        acc = uext_ref[pl.ds(HALO - half, T_MIX), pl.ds(c0, POOL_GROUP)]
        for o in range(-half + 1, half):
            acc = acc + uext_ref[pl.ds(HALO + o, T_MIX), pl.ds(c0, POOL_GROUP)]
        lo = jnp.maximum(pos - half, 0)
        hi = jnp.minimum(pos + half, seq_len)
        cnt = (hi - lo).astype(F32)
        d = acc / cnt - uc[:, c0:c0 + POOL_GROUP]
        y = jnp.dot(d.astype(BF16), wpool_ref[gi], preferred_element_type=F32)
        y = (y + bpool_ref[:, c0:c0 + POOL_GROUP]) * pscale_ref[:, c0:c0 + POOL_GROUP]
        outs.append(y)
    return jnp.concatenate(outs, axis=1)


def _attention_bias():
    t = ATT_BLOCK
    r = np.arange(t)[:, None]
    c = np.arange(3 * t)[None, :]
    dist = np.abs(r - (c - t))
    band = dist <= WINDOW
    out = np.empty((3, N_HEADS, t, 3 * t), np.float32)
    for var, valid in enumerate((band, band & (c >= t), band & (c < 2 * t))):
        for h in range(N_HEADS):
            out[var, h] = np.where(valid, -ALIBI_SLOPES[h] * LOG2E * dist, NEG_INF)
    return out.reshape(3, N_HEADS * t, 3 * t)


def _attention_scores(q, kw):
    t = ATT_BLOCK
    low = lax.broadcasted_iota(I32, (t, LANES), 1) < HEAD_DIM
    zero = jnp.zeros((t, LANES), BF16)
    qs = []
    for h in range(N_HEADS):
        blk = q[:, (h % Q_PER_KV) * LANES:(h % Q_PER_KV + 1) * LANES]
        qs.append(jnp.where(low, blk, zero) if h < Q_PER_KV else jnp.where(low, zero, blk))
    return lax.dot_general(jnp.concatenate(qs, axis=0), kw, (((1,), (1,)), ((), ())),
                           preferred_element_type=F32)


def _attention_probs(s, variant, bias_ref, sink_ref):
    t = ATT_BLOCK
    sh = s + bias_ref[variant]
    sink = jnp.concatenate([jnp.broadcast_to(sink_ref[:, h:h + 1] * LOG2E, (t, 1)) for h in range(N_HEADS)], axis=0)
    m = jnp.maximum(jnp.max(sh, axis=-1, keepdims=True), sink)
    p = jnp.exp2(sh - m)
    den = jnp.sum(p, axis=-1, keepdims=True) + jnp.exp2(sink - m)
    return p.astype(BF16), den


def _attention_out(p, den, vw):
    t = ATT_BLOCK
    low = lax.broadcasted_iota(I32, (t, LANES), 1) < HEAD_DIM
    o = jnp.dot(p, vw, preferred_element_type=F32) / den
    return jnp.concatenate(
        [jnp.where(low, o[cb * t:(cb + 1) * t], o[(Q_PER_KV + cb) * t:(Q_PER_KV + cb + 1) * t])
         for cb in range(Q_PER_KV)], axis=1)


def _route(biased, scores):
    t = biased.shape[1]
    rowf = lax.broadcasted_iota(I32, (N_EXPERTS, t), 0).astype(F32)
    ninf = float("-inf")
    gs = []
    for g in range(N_EXPERT_GROUPS):
        blk = biased[g * GROUP_SIZE:(g + 1) * GROUP_SIZE, :]
        rf = rowf[g * GROUP_SIZE:(g + 1) * GROUP_SIZE, :]
        m1 = jnp.max(blk, axis=0, keepdims=True)
        i1 = jnp.min(jnp.where(blk == m1, rf, float(N_EXPERTS)), axis=0, keepdims=True)
        m2 = jnp.max(jnp.where(rf == i1, ninf, blk), axis=0, keepdims=True)
        gs.append(m1 + m2)
    keep = []
    for g in range(N_EXPERT_GROUPS):
        beat = jnp.zeros((1, t), F32)
        for g2 in range(N_EXPERT_GROUPS):
            if g2 == g:
                continue
            better = (gs[g2] >= gs[g]) if g2 < g else (gs[g2] > gs[g])
            beat = beat + better.astype(F32)
        keep.append(jnp.broadcast_to(beat < float(TOPK_GROUPS), (GROUP_SIZE, t)))
    emask = jnp.concatenate(keep, axis=0)
    masked = jnp.where(emask, biased, NEG_INF)
    idx_rows, w_rows = [], []
    sel_any = jnp.zeros((N_EXPERTS, t), F32)
    for _ in range(TOP_K):
        m = jnp.max(masked, axis=0, keepdims=True)
        ik = jnp.min(jnp.where(masked == m, rowf, float(N_EXPERTS)), axis=0, keepdims=True)
        sel = rowf == ik
        w_rows.append(jnp.sum(jnp.where(sel, scores, 0.0), axis=0, keepdims=True))
        idx_rows.append(ik)
        masked = jnp.where(sel, ninf, masked)
        sel_any = jnp.where(sel, 1.0, sel_any)
    return idx_rows, w_rows, sel_any, rowf


def _mixer_kernel(meta_ref, xp_ref, xs_ref, mod_ref, up_ref, uc_ref, un_ref, q_ref,
                  kp_ref, kc_ref, kn_ref, vp_ref, vc_ref, vn_ref,
                  wpool_ref, bpool_ref, pscale_ref, sink_ref, wout_ref, g2_ref,
                  wr3_ref, rbias_ref, wsgu_ref, wsd_ref, tri_ref, abias_ref,
                  xa_ref, h2p_ref, idx_ref, rank_ref, wts_ref, cnt_ref,
                  uext_ref, kext_ref, vext_ref, base_ref):
    i = pl.program_id(0)
    first = meta_ref[1, i] == 1
    last = meta_ref[2, i] == 1
    pos0 = meta_ref[3, i]
    seq_len = meta_ref[4, i]
    mod = mod_ref[0]
    gt1 = mod[:, 2 * D_MODEL:3 * D_MODEL]
    sh2 = mod[:, 3 * D_MODEL:4 * D_MODEL]
    sc2 = mod[:, 4 * D_MODEL:5 * D_MODEL]
    gt2 = mod[:, 5 * D_MODEL:6 * D_MODEL]

    @pl.when(i == 0)
    def _():
        base_ref[...] = jnp.zeros_like(base_ref)

    uc = uc_ref[...]
    uext_ref[pl.ds(0, HALO), :] = jnp.where(first, 0.0, up_ref[...])
    uext_ref[pl.ds(HALO, T_MIX), :] = uc
    uext_ref[pl.ds(HALO + T_MIX, HALO), :] = jnp.where(last, 0.0, un_ref[...])
    pos = pos0 + lax.broadcasted_iota(I32, (T_MIX, 1), 0)
    a_pool = _pool_mixer(uext_ref, uc, pos, seq_len, wpool_ref, bpool_ref, pscale_ref)

    t = ATT_BLOCK
    kext_ref[pl.ds(0, t), :] = kp_ref[...]
    kext_ref[pl.ds(t, T_MIX), :] = kc_ref[...]
    kext_ref[pl.ds(t + T_MIX, t), :] = kn_ref[...]
    vext_ref[pl.ds(0, t), :] = vp_ref[...]
    vext_ref[pl.ds(t, T_MIX), :] = vc_ref[...]
    vext_ref[pl.ds(t + T_MIX, t), :] = vn_ref[...]
    n_sub = T_MIX // t
    assert n_sub >= 2, "a query block is never both first and last in its sequence"
    variants = []
    for sub in range(n_sub):
        variant = 0
        if sub == 0:
            variant = jnp.where(first, 1, variant)
        if sub == n_sub - 1:
            variant = jnp.where(last, 2, variant)
        variants.append(variant)
    scores = [_attention_scores(q_ref[pl.ds(sub * t, t), :], kext_ref[pl.ds(sub * t, 3 * t), :])
              for sub in range(n_sub)]
    probs = [_attention_probs(scores[sub], variants[sub], abias_ref, sink_ref) for sub in range(n_sub)]
    attn = [_attention_out(probs[sub][0], probs[sub][1], vext_ref[pl.ds(sub * t, 3 * t), :])
            for sub in range(n_sub)]
    a_attn = jnp.concatenate(attn, axis=0)

    a = jnp.concatenate([a_pool, a_attn], axis=1).astype(BF16)
    mix = jnp.dot(a, wout_ref[...], preferred_element_type=F32)

    x = jnp.where(meta_ref[7, i] == 0, xp_ref[...], xs_ref[...])
    x1 = x + gt1 * mix
    h2 = _rms_mod(x1, g2_ref[...], sc2, sh2)
    h2b = h2.astype(BF16)
    planes = _pack_planes(h2)
    for j in range(N_PLANES):
        h2p_ref[j] = planes[j]
    gu = jnp.dot(h2b, wsgu_ref[...], preferred_element_type=F32)
    gate, up = gu[:, :D_SHARED], gu[:, D_SHARED:]
    act = (gate * jax.nn.sigmoid(gate)) * up
    shared = jnp.dot(act.astype(BF16), wsd_ref[...], preferred_element_type=F32)
    xa_ref[...] = x1 + gt2 * shared
    h2lo = (h2 - h2b.astype(F32)).astype(BF16)
    logits = lax.dot_general(wr3_ref[...], jnp.concatenate([h2b, h2b, h2lo], axis=1),
                             (((1,), (1,)), ((), ())), preferred_element_type=F32)
    scores = jax.nn.sigmoid(logits)
    biased = scores + rbias_ref[...]
    routed = [_route(biased[:, c * LANES:(c + 1) * LANES], scores[:, c * LANES:(c + 1) * LANES])
              for c in range(T_MIX // LANES)]
    for c, (idx_rows, w_rows, _, _) in enumerate(routed):
        wsum = w_rows[0]
        for wr in w_rows[1:]:
            wsum = wsum + wr
        wts_t = jnp.concatenate([wr / wsum * ROUTED_SCALE for wr in w_rows], axis=0)
        wpad = jnp.concatenate([wts_t, jnp.zeros((LANES - TOP_K, LANES), F32)], axis=0)
        wts_ref[pl.ds(c * LANES, LANES), :] = wpad.T
        idx_ref[:, pl.ds(c * LANES, LANES)] = jnp.concatenate(idx_rows, axis=0).astype(I32)
    sel_any = jnp.concatenate([r[2] for r in routed], axis=1)
    pref = jnp.dot(sel_any.astype(BF16), tri_ref[...], preferred_element_type=F32)
    before = base_ref[...] + pref[:, :T_MIX]
    for c, (idx_rows, _, _, rowf) in enumerate(routed):
        bc = before[:, c * LANES:(c + 1) * LANES]
        ranks = [jnp.sum(jnp.where(rowf == ik, bc, 0.0), axis=0, keepdims=True) for ik in idx_rows]
        rank_ref[:, pl.ds(c * LANES, LANES)] = jnp.concatenate(ranks, axis=0).astype(I32)
    new_base = base_ref[...] + pref[:, T_MIX:]
    base_ref[...] = new_base
    cnt_ref[...] = new_base


def _mixer(xp, xs, mod3, u, q, k, v, wpool_bf, bpool, pscale, sink, wout_bf, g2, wr3, rbias,
           wsgu_bf, wsd_bf, segments):
    n = _group_tokens(segments)
    t = T_MIX
    nt = n // t
    ab = ATT_BLOCK
    meta = jnp.asarray(_tile_meta(segments, t))
    tri = np.concatenate([np.triu(np.ones((t, t), np.float32), 1), np.ones((t, t), np.float32)], axis=1)
    tri = jnp.asarray(tri, BF16)
    abias = jnp.asarray(_attention_bias())
    cur = lambda w: pl.BlockSpec((t, w), lambda i, m: (i, 0))
    prev = lambda w: pl.BlockSpec((ab, w), lambda i, m: (jnp.maximum(i * (t // ab) - 1, 0), 0))
    nxt = lambda w: pl.BlockSpec((ab, w), lambda i, m: (jnp.minimum((i + 1) * (t // ab), n // ab - 1), 0))
    full = lambda a: pl.BlockSpec(a.shape, lambda i, m: (0,) * a.ndim)
    hb = t // HALO
    in_specs = [
        pl.BlockSpec((t, D_MODEL), lambda i, m: (m[5, i], 0)),
        pl.BlockSpec((t, D_MODEL), lambda i, m: (m[6, i], 0)),
        pl.BlockSpec((1, 1, 6 * D_MODEL), lambda i, m: (m[0, i], 0, 0)),
        pl.BlockSpec((HALO, POOL_WIDTH), lambda i, m: (jnp.maximum(i * hb - 1, 0), 0)),
        cur(POOL_WIDTH),
        pl.BlockSpec((HALO, POOL_WIDTH), lambda i, m: (jnp.minimum((i + 1) * hb, n // HALO - 1), 0)),
        cur(ATTN_WIDTH),
        prev(KV_WIDTH), cur(KV_WIDTH), nxt(KV_WIDTH),
        prev(KV_WIDTH), cur(KV_WIDTH), nxt(KV_WIDTH),
        full(wpool_bf), full(bpool), full(pscale), full(sink), full(wout_bf), full(g2),
        full(wr3), full(rbias), full(wsgu_bf), full(wsd_bf), full(tri), full(abias),
    ]
    out_specs = [
        cur(D_MODEL),
        pl.BlockSpec((N_PLANES, t, LANES), lambda i, m: (0, i, 0)),
        pl.BlockSpec((TOP_K, t), lambda i, m: (0, i)),
        pl.BlockSpec((TOP_K, t), lambda i, m: (0, i)),
        cur(LANES),
        pl.BlockSpec((N_EXPERTS, t), lambda i, m: (0, 0)),
    ]
    out_shape = [
        jax.ShapeDtypeStruct((n, D_MODEL), F32),
        jax.ShapeDtypeStruct((N_PLANES, n, LANES), I32),
        jax.ShapeDtypeStruct((TOP_K, n), I32),
        jax.ShapeDtypeStruct((TOP_K, n), I32),
        jax.ShapeDtypeStruct((n, LANES), F32),
        jax.ShapeDtypeStruct((N_EXPERTS, t), F32),
    ]
    return pl.pallas_call(
        _mixer_kernel,
        grid_spec=pltpu.PrefetchScalarGridSpec(
            num_scalar_prefetch=1, grid=(nt,), in_specs=in_specs, out_specs=out_specs,
            scratch_shapes=[pltpu.VMEM((t + 2 * HALO, POOL_WIDTH), F32),
                            pltpu.VMEM((t + 2 * ab, KV_WIDTH), BF16),
                            pltpu.VMEM((t + 2 * ab, KV_WIDTH), BF16),
                            pltpu.VMEM((N_EXPERTS, t), F32)]),
        out_shape=out_shape,
        compiler_params=pltpu.CompilerParams(dimension_semantics=("arbitrary",),
                                             vmem_limit_bytes=VMEM_LIMIT),
        name="mixer",
    )(meta, xp, xs, mod3, u, u, u, q, k, k, k, v, v, v, wpool_bf, bpool, pscale, sink, wout_bf, g2,
      wr3, rbias, wsgu_bf, wsd_bf, tri, abias)


def _dest_kernel(idx_ref, rank_ref, pstart_ref, o_ref, *, n_rows):
    slab = min(T_DEST_SLAB, idx_ref.shape[1])
    rowi = lax.broadcasted_iota(I32, (N_EXPERTS, slab), 0)
    pstart = pstart_ref[...]
    for c in range(idx_ref.shape[1] // slab):
        lanes = pl.ds(c * slab, slab)
        rows = []
        for kk in range(TOP_K):
            sel = rowi == idx_ref[kk:kk + 1, lanes]
            start = jnp.sum(jnp.where(sel, pstart, 0.0), axis=0, keepdims=True)
            rows.append(start.astype(I32) + rank_ref[kk:kk + 1, lanes])
        dest = jnp.concatenate(rows, axis=0)
        for j in range(N_PLANES):
            o_ref[:, j, lanes] = dest + j * n_rows


def _dest(idx_t, rank_t, pstart_col, n_rows):
    n = idx_t.shape[1]
    td = int(np.gcd(n, T_DEST))
    return pl.pallas_call(
        functools.partial(_dest_kernel, n_rows=n_rows),
        grid=(n // td,),
        in_specs=[pl.BlockSpec((TOP_K, td), lambda i: (0, i)),
                  pl.BlockSpec((TOP_K, td), lambda i: (0, i)),
                  pl.BlockSpec((N_EXPERTS, 1), lambda i: (0, 0))],
        out_specs=pl.BlockSpec((TOP_K, N_PLANES, td), lambda i: (0, 0, i)),
        out_shape=jax.ShapeDtypeStruct((TOP_K, N_PLANES, n), I32),
        name="dest",
    )(idx_t, rank_t, pstart_col)


def _sc_mesh():
    return plsc.VectorSubcoreMesh(core_axis_name="c", subcore_axis_name="s")


def _sc_worker():
    return lax.axis_index("s") * 2 + lax.axis_index("c")


def _sc_scatter_rows(table, didx, n_out_rows):
    m = table.shape[0]
    per_worker = m // SC_ROWS // SC_WORKERS
    assert per_worker % 2 == 0

    @functools.partial(
        pl.kernel, mesh=_sc_mesh(),
        out_type=jax.ShapeDtypeStruct((n_out_rows, LANES), I32),
        scratch_types=[pltpu.VMEM((2, SC_ROWS, LANES), I32), pltpu.VMEM((2, TOP_K, SC_ROWS), I32),
                       pltpu.SemaphoreType.DMA((2,)), pltpu.SemaphoreType.DMA((2,))])
    def run(table_hbm, didx_hbm, out_hbm, rows, idx, lsem, ssem):
        wid = _sc_worker()

        def loads(step, p):
            off = pl.multiple_of((wid * per_worker + step) * SC_ROWS, SC_ROWS)
            return (pltpu.make_async_copy(table_hbm.at[pl.ds(off, SC_ROWS)], rows.at[p], lsem.at[p]),
                    pltpu.make_async_copy(didx_hbm.at[:, pl.ds(off, SC_ROWS)], idx.at[p], lsem.at[p]))

        def scatter(p, kk):
            return pltpu.make_async_copy(rows.at[p], out_hbm.at[idx.at[p].at[kk]], ssem.at[p])

        for cp in loads(0, 0):
            cp.start()

        @pl.loop(0, per_worker, step=2)
        def _(s0):
            for p in range(2):
                step = s0 + p
                for cp in loads(step, p):
                    cp.wait()
                for kk in range(TOP_K):
                    scatter(p, kk).start()

                @pl.when(step > 0)
                def _():
                    for kk in range(TOP_K):
                        scatter(1 - p, kk).wait()

                @pl.when(step + 1 < per_worker)
                def _():
                    for cp in loads(step + 1, 1 - p):
                        cp.start()

        for kk in range(TOP_K):
            scatter(1, kk).wait()

    return run(table, didx)


def _sc_gather_rows(table, didx):
    m = didx.shape[1]
    per_worker = m // SC_ROWS // SC_WORKERS
    assert per_worker % 2 == 0 and (2 * TOP_K) % SC_NBUF == 0
    items = 2 * TOP_K

    @functools.partial(
        pl.kernel, mesh=_sc_mesh(),
        out_type=jax.ShapeDtypeStruct((TOP_K, m, LANES), I32),
        scratch_types=[pltpu.VMEM((SC_NBUF, SC_ROWS, LANES), I32), pltpu.VMEM((2, TOP_K, SC_ROWS), I32),
                       pltpu.SemaphoreType.DMA((SC_NBUF,)), pltpu.SemaphoreType.DMA((SC_NBUF,)),
                       pltpu.SemaphoreType.DMA((2,))])
    def run(table_hbm, didx_hbm, out_hbm, rows, idx, gsem, wsem, isem):
        wid = _sc_worker()

        def off_of(step):
            return pl.multiple_of((wid * per_worker + step) * SC_ROWS, SC_ROWS)

        def idx_copy(step, p):
            return pltpu.make_async_copy(didx_hbm.at[:, pl.ds(off_of(step), SC_ROWS)], idx.at[p], isem.at[p])

        def gather(j):
            r = j % SC_NBUF
            return pltpu.make_async_copy(table_hbm.at[idx.at[j // TOP_K].at[j % TOP_K]], rows.at[r], gsem.at[r])

        def write(s0, j):
            r = j % SC_NBUF
            return pltpu.make_async_copy(rows.at[r], out_hbm.at[j % TOP_K, pl.ds(off_of(s0 + j // TOP_K), SC_ROWS)],
                                         wsem.at[r])

        def retire_write(s0, j):
            if j >= 0:
                write(s0, j).wait()
            else:
                @pl.when(s0 > 0)
                def _():
                    write(s0 - 2, j + items).wait()

        def finish_read(s0, j):
            if j >= 0:
                gather(j).wait()
                write(s0, j).start()
            else:
                @pl.when(s0 > 0)
                def _():
                    gather(j + items).wait()
                    write(s0 - 2, j + items).start()

        idx_copy(0, 0).start()

        @pl.loop(0, per_worker, step=2)
        def _(s0):
            for j in range(items):
                p, kk = j // TOP_K, j % TOP_K
                if kk == 0:
                    idx_copy(s0 + p, p).wait()
                retire_write(s0, j - SC_NBUF)
                gather(j).start()
                finish_read(s0, j - SC_LAG)
                if kk == SC_LAG:
                    @pl.when(s0 + p + 1 < per_worker)
                    def _():
                        idx_copy(s0 + p + 1, 1 - p).start()

        last = per_worker - 2
        for j in range(items - SC_LAG, items):
            gather(j).wait()
            write(last, j).start()
        for j in range(items - SC_NBUF, items):
            write(last, j).wait()

    return run(table, didx)


def _experts_kernel(c0_ref, ntot_ref, tsrc_ref, trow_ref, tvalid_ref, total_ref, *refs, n_groups):
    x_hbms = refs[:n_groups]
    wg_ref, wu_ref, wd_ref = refs[n_groups:n_groups + 3]
    y_hbms = refs[n_groups + 3:2 * n_groups + 3]
    wgu_s, wd_s, xbuf, ybuf, xsem, ysem = refs[2 * n_groups + 3:]
    e = pl.program_id(0)
    nb = xbuf.shape[0]
    ch = xbuf.shape[2]
    total = total_ref[0]

    def x_copy(c, g, row):
        slot = lax.rem(c, nb)
        return pltpu.make_async_copy(x_hbms[g].at[:, pl.ds(row, ch), :], xbuf.at[slot], xsem.at[slot])

    def y_copy(c, g, row):
        slot = lax.rem(c, nb)
        return pltpu.make_async_copy(ybuf.at[slot], y_hbms[g].at[:, pl.ds(row, ch), :], ysem.at[slot])

    def start(copy, c):
        row = pl.multiple_of(trow_ref[c], ch)
        if n_groups == 1:
            copy(c, 0, row).start()
            return
        for g in range(n_groups):
            @pl.when(tsrc_ref[c] == g)
            def _():
                copy(c, g, row).start()

    def wait(copy, c):
        copy(c, 0, 0).wait()

    @pl.when(e == 0)
    def _():
        for c in range(nb - 1):
            @pl.when(c < total)
            def _():
                start(x_copy, c)

    for j in range(N_PLANES):
        hi = pl.ds(j * LANES, LANES)
        lo = pl.ds(HALF + j * LANES, LANES)
        r_hi = pl.ds(2 * j * LANES, LANES)
        r_lo = pl.ds((2 * j + 1) * LANES, LANES)
        wgu_s[r_hi, pl.ds(0, D_EXPERT)] = wg_ref[0, hi, :].astype(BF16)
        wgu_s[r_lo, pl.ds(0, D_EXPERT)] = wg_ref[0, lo, :].astype(BF16)
        wgu_s[r_hi, pl.ds(D_EXPERT, D_EXPERT)] = wu_ref[0, hi, :].astype(BF16)
        wgu_s[r_lo, pl.ds(D_EXPERT, D_EXPERT)] = wu_ref[0, lo, :].astype(BF16)
    wd_s[...] = wd_ref[0].astype(BF16)

    c0 = c0_ref[e]

    def chunk(i, carry):
        c = c0 + i
        slot = lax.rem(c, nb)

        @pl.when(c + nb - 1 < total)
        def _():
            start(x_copy, c + nb - 1)

        wait(x_copy, c)

        @pl.when(c >= nb)
        def _():
            wait(y_copy, c - nb)

        valid = tvalid_ref[c]

        def swiglu_rows(n_rows):
            rows = pl.ds(0, n_rows)
            live = lax.broadcasted_iota(I32, (n_rows, LANES), 0) < valid
            xs = []
            for j in range(N_PLANES):
                hi, lo = _unpack_plane(jnp.where(live, xbuf[slot, j, rows, :], 0))
                xs += [hi.astype(BF16), lo.astype(BF16)]
            acc = jnp.dot(jnp.concatenate(xs, axis=1), wgu_s[...], preferred_element_type=F32)
            gate, up = acc[:, :D_EXPERT], acc[:, D_EXPERT:]
            act = (gate * jax.nn.sigmoid(gate)) * up
            y = jnp.dot(act.astype(BF16), wd_s[...], preferred_element_type=F32)
            planes = _pack_planes(y)
            for j in range(N_PLANES):
                ybuf[slot, j, rows, :] = planes[j]
                if n_rows < ch:
                    ybuf[slot, j, pl.ds(n_rows, ch - n_rows), :] = jnp.zeros((ch - n_rows, LANES), I32)

        pl.when(valid > ch // 2)(lambda: swiglu_rows(ch))
        pl.when(valid <= ch // 2)(lambda: swiglu_rows(ch // 2))
        start(y_copy, c)
        return carry

    lax.fori_loop(0, ntot_ref[e], chunk, 0)

    @pl.when(e == pl.num_programs(0) - 1)
    def _():
        for back in range(nb, 0, -1):
            @pl.when(total >= back)
            def _():
                wait(y_copy, total - back)


def _chunk_tables(nchunks, cstarts, counts, max_chunks):
    ch = EXPERT_CHUNK
    ntot = sum(nchunks)
    cum = jnp.cumsum(ntot)
    c0 = cum - ntot
    c = jnp.arange(max_chunks, dtype=I32)
    e = jnp.minimum(jnp.sum((c[:, None] >= cum[None, :]).astype(I32), axis=1), N_EXPERTS - 1)
    own = e[:, None] == jnp.arange(N_EXPERTS, dtype=I32)[None, :]
    pick = lambda v: jnp.sum(jnp.where(own, v[None, :], 0), axis=1)
    i = c - pick(c0)
    src = jnp.zeros_like(c)
    row = jnp.zeros_like(c)
    valid = jnp.zeros_like(c)
    lo = jnp.zeros_like(c)
    for g in range(len(nchunks)):
        n_g = pick(nchunks[g])
        inside = (i >= lo) & (i < lo + n_g) & (c < cum[-1])
        loc = i - lo
        src = jnp.where(inside, g, src)
        row = jnp.where(inside, (pick(cstarts[g]) + loc) * ch, row)
        valid = jnp.where(inside, jnp.clip(pick(counts[g]) - loc * ch, 0, ch), valid)
        lo = lo + n_g
    return c0.astype(I32), ntot.astype(I32), src, row, valid, cum[-1:].astype(I32)


def _experts(xs_planes, w_gate, w_up, w_down, nchunks, cstarts, counts):
    ng = len(xs_planes)
    ch = EXPERT_CHUNK
    tables = _chunk_tables(nchunks, cstarts, counts, sum(x.shape[1] // ch for x in xs_planes))
    w_map = lambda e, *_: (e, 0, 0)
    any_space = pl.BlockSpec(memory_space=pl.ANY)
    return pl.pallas_call(
        functools.partial(_experts_kernel, n_groups=ng),
        grid_spec=pltpu.PrefetchScalarGridSpec(
            num_scalar_prefetch=6, grid=(N_EXPERTS,),
            in_specs=[any_space] * ng + [pl.BlockSpec((1, D_MODEL, D_EXPERT), w_map),
                                         pl.BlockSpec((1, D_MODEL, D_EXPERT), w_map),
                                         pl.BlockSpec((1, D_EXPERT, D_MODEL), w_map)],
            out_specs=[any_space] * ng,
            scratch_shapes=[pltpu.VMEM((D_MODEL, 2 * D_EXPERT), BF16),
                            pltpu.VMEM((D_EXPERT, D_MODEL), BF16),
                            pltpu.VMEM((EXPERT_NBUF, N_PLANES, ch, LANES), I32),
                            pltpu.VMEM((EXPERT_NBUF, N_PLANES, ch, LANES), I32),
                            pltpu.SemaphoreType.DMA((EXPERT_NBUF,)),
                            pltpu.SemaphoreType.DMA((EXPERT_NBUF,))]),
        out_shape=[jax.ShapeDtypeStruct(x.shape, I32) for x in xs_planes],
        compiler_params=pltpu.CompilerParams(dimension_semantics=("arbitrary",),
                                             vmem_limit_bytes=VMEM_LIMIT),
        name="experts",
    )(*tables, *xs_planes, w_gate, w_up, w_down)


def _final_kernel(xa_ref, yk_ref, wts_ref, mod_ref, g_ref, o_ref):
    wts = wts_ref[...]
    his = [jnp.zeros((T_FIN, LANES), F32) for _ in range(N_PLANES)]
    los = [jnp.zeros((T_FIN, LANES), F32) for _ in range(N_PLANES)]
    for kk in range(TOP_K):
        wk = wts[:, kk:kk + 1]
        for j in range(N_PLANES):
            hi, lo = _unpack_plane(yk_ref[kk, j])
            his[j] = his[j] + wk * hi
            los[j] = los[j] + wk * lo
    routed = jnp.concatenate(his + los, axis=1)
    gt2 = mod_ref[0][:, 5 * D_MODEL:6 * D_MODEL]
    x = xa_ref[...] + gt2 * routed
    ms = jnp.mean(x * x, axis=-1, keepdims=True)
    o_ref[...] = x * lax.rsqrt(ms + EPS) * g_ref[...]


def _final_kernel_into(xa_ref, yk_ref, wts_ref, mod_ref, g_ref, prev_ref, o_ref):
    _final_kernel(xa_ref, yk_ref, wts_ref, mod_ref, g_ref, o_ref)


def _final(xa, yk, wts, mod3, g_final, tile_off, n_tok, seq_len, sid0, out_rows, out_tile_off, prev):
    tiles_per_seq = seq_len // T_FIN
    in_specs = [pl.BlockSpec((T_FIN, D_MODEL), lambda i: (i + tile_off, 0)),
                pl.BlockSpec((TOP_K, N_PLANES, T_FIN, LANES), lambda i: (0, 0, i + tile_off, 0)),
                pl.BlockSpec((T_FIN, LANES), lambda i: (i + tile_off, 0)),
                pl.BlockSpec((1, 1, 6 * D_MODEL), lambda i: (sid0 + i // tiles_per_seq, 0, 0)),
                pl.BlockSpec((1, D_MODEL), lambda i: (0, 0))]
    args = [xa, yk, wts, mod3, g_final]
    if prev is not None:
        in_specs.append(pl.BlockSpec(memory_space=pl.ANY))
        args.append(prev)
    return pl.pallas_call(
        _final_kernel if prev is None else _final_kernel_into,
        grid=(n_tok // T_FIN,),
        in_specs=in_specs,
        out_specs=pl.BlockSpec((T_FIN, D_MODEL), lambda i: (i + out_tile_off, 0)),
        out_shape=jax.ShapeDtypeStruct((out_rows, D_MODEL), F32),
        input_output_aliases={} if prev is None else {len(args) - 1: 0},
        compiler_params=pltpu.CompilerParams(dimension_semantics=("arbitrary",),
                                             vmem_limit_bytes=VMEM_LIMIT),
        name="final",
    )(*args)


def _token_groups(bp, sp, bs, ss):
    return [[(0, 0, bp, sp, 0), (1, 0, bs, ss, bp)]]


def kernel(x_prompt, x_sample, c_prompt, c_sample, w_ada, b_ada, g_norm1, w_in, w_pool, b_pool, pool_scale, attn_sink, w_out, g_norm2, w_router, router_bias, w_gate, w_up, w_down, ws_gate, ws_up, ws_down, g_final):
    assert w_ada.shape[0] == 1, "one layer"
    bp, sp, d = x_prompt.shape
    bs, ss, _ = x_sample.shape
    assert d == D_MODEL and bp + bs <= ADA_ROWS
    n_p, n_s = bp * sp, bs * ss
    xp = x_prompt.reshape(n_p, d)
    xs = x_sample.reshape(n_s, d)

    c_all = jnp.concatenate([c_prompt, c_sample, jnp.zeros((ADA_ROWS - bp - bs, d), F32)], axis=0)
    mod3 = _ada(c_all, w_ada[0], b_ada[0]).reshape(ADA_ROWS, 1, 6 * d)

    q0, q1 = POOL_WIDTH, POOL_WIDTH + ATTN_WIDTH
    wq = w_in[0][:, q0:q1].reshape(d, N_KV_HEADS, Q_PER_KV, HEAD_DIM).transpose(0, 2, 1, 3).reshape(d, ATTN_WIDTH)
    w_in_p = jnp.concatenate([w_in[0][:, :q0], wq, w_in[0][:, q1:]], axis=1).astype(BF16)
    wo = w_out[0][q0:].reshape(N_KV_HEADS, Q_PER_KV, HEAD_DIM, d).transpose(1, 0, 2, 3).reshape(ATTN_WIDTH, d)
    w_out_p = jnp.concatenate([w_out[0][:q0], wo], axis=0).astype(BF16)
    wsgu = jnp.concatenate([ws_gate[0], ws_up[0]], axis=1).astype(BF16)
    wr_hi = w_router[0].astype(BF16)
    wr_lo = (w_router[0] - wr_hi.astype(F32)).astype(BF16)
    wr3 = jnp.concatenate([wr_hi, wr_lo, wr_hi], axis=0).T
    g1 = g_norm1[0].reshape(1, d)
    mixer_weights = (
        w_pool[0].astype(BF16), b_pool[0].reshape(1, POOL_WIDTH), pool_scale[0].reshape(1, POOL_WIDTH),
        attn_sink[0].reshape(1, N_HEADS), w_out_p, g_norm2[0].reshape(1, d),
        wr3, router_bias[0].reshape(N_EXPERTS, 1), wsgu, ws_down[0].astype(BF16))
    ch = EXPERT_CHUNK

    groups = _token_groups(bp, sp, bs, ss)
    staged = []
    for segments in groups:
        n = _group_tokens(segments)
        u, q, k, v = _inproj(xp, xs, mod3, g1, w_in_p, segments)
        xa, h2p, idx_t, rank_t, wts, cnt = _mixer(xp, xs, mod3, u, q, k, v, *mixer_weights, segments)
        n_rows = (-(-n * TOP_K // ch) + N_EXPERTS) * ch
        counts = cnt[:, 0].astype(I32)
        padded = (counts + ch - 1) // ch * ch
        pstart = jnp.cumsum(padded) - padded
        didx = _dest(idx_t, rank_t, pstart.astype(F32).reshape(N_EXPERTS, 1), n_rows)
        didx = didx.reshape(TOP_K, N_PLANES * n)
        xs_rows = _sc_scatter_rows(h2p.reshape(N_PLANES * n, LANES), didx, N_PLANES * n_rows)
        staged.append((xa, wts, didx, xs_rows.reshape(N_PLANES, n_rows, LANES), padded // ch, pstart // ch, counts))

    ys = _experts([st[3] for st in staged], w_gate[0], w_up[0], w_down[0],
                  [st[4] for st in staged], [st[5] for st in staged], [st[6] for st in staged])

    gf = g_final.reshape(1, d)
    outs = [None, None]
    out_rows = (n_p, n_s)
    for segments, (xa, wts, didx, xs_rows, _, _, _), ys_rows in zip(groups, staged, ys):
        n = _group_tokens(segments)
        yk = _sc_gather_rows(ys_rows.reshape(N_PLANES * ys_rows.shape[1], LANES), didx)
        yk = yk.reshape(TOP_K, N_PLANES, n, LANES)
        tok = 0
        for src, tok0, n_seqs, seq_len, sid0 in segments:
            n_tok = n_seqs * seq_len
            outs[src] = _final(xa, yk, wts, mod3, gf, tok // T_FIN, n_tok, seq_len, sid0,
                               out_rows[src], tok0 // T_FIN, outs[src])
            tok += n_tok
    return outs[0].reshape(bp, sp, d), outs[1].reshape(bs, ss, d)
```

```python
import functools

import numpy as np
import jax
import jax.numpy as jnp
from jax import lax
from jax.experimental import pallas as pl
from jax.experimental.pallas import tpu as pltpu
from jax.experimental.pallas import tpu_sc as plsc

F32 = jnp.float32
BF16 = jnp.bfloat16
I32 = jnp.int32

D_MODEL = 1024
POOL_WINDOWS = (2, 4, 8, 16)
POOL_WIDTH = 512
POOL_GROUP = 128
N_HEADS = 8
N_KV_HEADS = 2
HEAD_DIM = 64
Q_PER_KV = N_HEADS // N_KV_HEADS
ATTN_WIDTH = N_HEADS * HEAD_DIM
KV_WIDTH = N_KV_HEADS * HEAD_DIM
D_IN_PROJ = POOL_WIDTH + ATTN_WIDTH + 2 * KV_WIDTH
WINDOW = 128
N_EXPERTS = 256
TOP_K = 8
N_EXPERT_GROUPS = 8
GROUP_SIZE = N_EXPERTS // N_EXPERT_GROUPS
TOPK_GROUPS = 4
D_EXPERT = 256
D_SHARED = 256
ROUTED_SCALE = 2.5
EPS = 1e-6
NEG_INF = -1e30

LANES = 128
HALO = 8
N_PLANES = 4
HALF = D_MODEL // 2

T_IN = 1024
T_MIX = 512
ATT_BLOCK = 128
T_DEST = 2048
T_DEST_SLAB = 512
EXPERT_CHUNK = 512
EXPERT_NBUF = 4
T_FIN = 512
SC_ROWS = 128
SC_WORKERS = 32
SC_NBUF = 4
SC_LAG = 2
ADA_ROWS = 16
VMEM_LIMIT = 48 * 1024 * 1024

LOG2E = 1.4426950408889634
ALIBI_SLOPES = tuple(float(2.0 ** (-8.0 * (h + 1) / N_HEADS)) for h in range(N_HEADS))


def _pack_planes(y):
    planes = []
    for j in range(N_PLANES):
        hi = y[:, j * LANES:(j + 1) * LANES].astype(BF16).astype(F32)
        lo = y[:, HALF + j * LANES:HALF + (j + 1) * LANES].astype(BF16).astype(F32)
        hb = lax.bitcast_convert_type(hi, jnp.uint32) & jnp.uint32(0xFFFF0000)
        lb = lax.bitcast_convert_type(lo, jnp.uint32) >> jnp.uint32(16)
        planes.append(lax.bitcast_convert_type(hb | lb, I32))
    return planes


def _unpack_plane(w):
    u = lax.bitcast_convert_type(w, jnp.uint32)
    hi = lax.bitcast_convert_type(u & jnp.uint32(0xFFFF0000), F32)
    lo = lax.bitcast_convert_type(u << jnp.uint32(16), F32)
    return hi, lo


def _ada_kernel(c_ref, w_ref, b_ref, o_ref):
    c = c_ref[...]
    s = c * jax.nn.sigmoid(c)
    o_ref[...] = jnp.dot(s, w_ref[...], precision=lax.Precision.HIGHEST,
                         preferred_element_type=F32) + b_ref[...]


def _ada(c_all, w_ada, b_ada):
    n_out = w_ada.shape[1]
    tn = 1024
    return pl.pallas_call(
        _ada_kernel,
        grid=(n_out // tn,),
        in_specs=[pl.BlockSpec((ADA_ROWS, D_MODEL), lambda j: (0, 0)),
                  pl.BlockSpec((D_MODEL, tn), lambda j: (0, j)),
                  pl.BlockSpec((1, tn), lambda j: (0, j))],
        out_specs=pl.BlockSpec((ADA_ROWS, tn), lambda j: (0, j)),
        out_shape=jax.ShapeDtypeStruct((ADA_ROWS, n_out), F32),
        name="ada",
    )(c_all, w_ada, b_ada.reshape(1, n_out))


def _rms_mod(x, g, scale, shift):
    ms = jnp.mean(x * x, axis=-1, keepdims=True)
    return (x * lax.rsqrt(ms + EPS) * g) * (1.0 + scale) + shift


def _inproj_kernel(meta_ref, xp_ref, xs_ref, mod_ref, g_ref, w_ref, u_ref, q_ref, k_ref, v_ref):
    x = jnp.where(meta_ref[7, pl.program_id(0)] == 0, xp_ref[...], xs_ref[...])
    mod = mod_ref[0]
    h = _rms_mod(x, g_ref[...], mod[:, D_MODEL:2 * D_MODEL], mod[:, 0:D_MODEL])
    z = jnp.dot(h.astype(BF16), w_ref[...], preferred_element_type=F32)
    u_ref[...] = z[:, :POOL_WIDTH]
    q_ref[...] = (z[:, POOL_WIDTH:POOL_WIDTH + ATTN_WIDTH] * (LOG2E * HEAD_DIM ** -0.5)).astype(BF16)
    k_ref[...] = z[:, POOL_WIDTH + ATTN_WIDTH:POOL_WIDTH + ATTN_WIDTH + KV_WIDTH].astype(BF16)
    v_ref[...] = z[:, POOL_WIDTH + ATTN_WIDTH + KV_WIDTH:].astype(BF16)


def _tile_meta(segments, tile):
    rows = []
    for src, tok0, n_seqs, seq_len, sid0 in segments:
        per = seq_len // tile
        for q in range(n_seqs):
            for t in range(per):
                blk = (tok0 + q * seq_len) // tile + t
                rows.append((sid0 + q, int(t == 0), int(t == per - 1), t * tile, seq_len,
                             blk if src == 0 else -1, blk if src == 1 else -1, src))
    meta = np.asarray(rows, np.int32)
    for col in (5, 6):
        known = meta[:, col] >= 0
        if not known.any():
            meta[:, col] = 0
            continue
        last = np.maximum.accumulate(np.where(known, np.arange(len(meta)), -1))
        last = np.where(last < 0, np.argmax(known), last)
        meta[:, col] = meta[last, col]
    return meta.T.copy()


def _group_tokens(segments):
    return sum(n_seqs * seq_len for _, _, n_seqs, seq_len, _ in segments)


def _inproj(xp, xs, mod3, g1, w_in_bf, segments):
    n = _group_tokens(segments)
    t_in = int(np.gcd.reduce([T_IN] + [seq_len for _, _, _, seq_len, _ in segments]))
    meta = jnp.asarray(_tile_meta(segments, t_in))
    tok = lambda w: pl.BlockSpec((t_in, w), lambda i, m: (i, 0))
    return pl.pallas_call(
        _inproj_kernel,
        grid_spec=pltpu.PrefetchScalarGridSpec(
            num_scalar_prefetch=1,
            grid=(n // t_in,),
            in_specs=[
                pl.BlockSpec((t_in, D_MODEL), lambda i, m: (m[5, i], 0)),
                pl.BlockSpec((t_in, D_MODEL), lambda i, m: (m[6, i], 0)),
                pl.BlockSpec((1, 1, 6 * D_MODEL), lambda i, m: (m[0, i], 0, 0)),
                pl.BlockSpec((1, D_MODEL), lambda i, m: (0, 0)),
                pl.BlockSpec((D_MODEL, D_IN_PROJ), lambda i, m: (0, 0)),
            ],
            out_specs=[tok(POOL_WIDTH), tok(ATTN_WIDTH), tok(KV_WIDTH), tok(KV_WIDTH)],
        ),
        out_shape=[jax.ShapeDtypeStruct((n, POOL_WIDTH), F32),
                   jax.ShapeDtypeStruct((n, ATTN_WIDTH), BF16),
                   jax.ShapeDtypeStruct((n, KV_WIDTH), BF16),
                   jax.ShapeDtypeStruct((n, KV_WIDTH), BF16)],
        compiler_params=pltpu.CompilerParams(dimension_semantics=("arbitrary",),
                                             vmem_limit_bytes=VMEM_LIMIT),
        name="inproj",
    )(meta, xp, xs, mod3, g1, w_in_bf)


def _pool_mixer(uext_ref, uc, pos, seq_len, wpool_ref, bpool_ref, pscale_ref):
    outs = []
    for gi, w in enumerate(POOL_WINDOWS):
        c0 = gi * POOL_GROUP
        half = w // 2
        acc = uext_ref[pl.ds(HALO - half, T_MIX), pl.ds(c0, POOL_GROUP)]
        for o in range(-half + 1, half):
            acc = acc + uext_ref[pl.ds(HALO + o, T_MIX), pl.ds(c0, POOL_GROUP)]
        lo = jnp.maximum(pos - half, 0)
        hi = jnp.minimum(pos + half, seq_len)
        cnt = (hi - lo).astype(F32)
        d = acc / cnt - uc[:, c0:c0 + POOL_GROUP]
        y = jnp.dot(d.astype(BF16), wpool_ref[gi], preferred_element_type=F32)
        y = (y + bpool_ref[:, c0:c0 + POOL_GROUP]) * pscale_ref[:, c0:c0 + POOL_GROUP]
        outs.append(y)
    return jnp.concatenate(outs, axis=1)


def _attention_bias():
    t = ATT_BLOCK
    r = np.arange(t)[:, None]
    c = np.arange(3 * t)[None, :]
    dist = np.abs(r - (c - t))
    band = dist <= WINDOW
    out = np.empty((3, N_HEADS, t, 3 * t), np.float32)
    for var, valid in enumerate((band, band & (c >= t), band & (c < 2 * t))):
        for h in range(N_HEADS):
            out[var, h] = np.where(valid, -ALIBI_SLOPES[h] * LOG2E * dist, NEG_INF)
    return out.reshape(3, N_HEADS * t, 3 * t)


def _attention_scores(q, kw):
    t = ATT_BLOCK
    low = lax.broadcasted_iota(I32, (t, LANES), 1) < HEAD_DIM
    zero = jnp.zeros((t, LANES), BF16)
    qs = []
    for h in range(N_HEADS):
        blk = q[:, (h % Q_PER_KV) * LANES:(h % Q_PER_KV + 1) * LANES]
        qs.append(jnp.where(low, blk, zero) if h < Q_PER_KV else jnp.where(low, zero, blk))
    return lax.dot_general(jnp.concatenate(qs, axis=0), kw, (((1,), (1,)), ((), ())),
                           preferred_element_type=F32)


def _attention_probs(s, variant, bias_ref, sink_ref):
    t = ATT_BLOCK
    sh = s + bias_ref[variant]
    sink = jnp.concatenate([jnp.broadcast_to(sink_ref[:, h:h + 1] * LOG2E, (t, 1)) for h in range(N_HEADS)], axis=0)
    m = jnp.maximum(jnp.max(sh, axis=-1, keepdims=True), sink)
    p = jnp.exp2(sh - m)
    den = jnp.sum(p, axis=-1, keepdims=True) + jnp.exp2(sink - m)
    return p.astype(BF16), den


def _attention_out(p, den, vw):
    t = ATT_BLOCK
    low = lax.broadcasted_iota(I32, (t, LANES), 1) < HEAD_DIM
    o = jnp.dot(p, vw, preferred_element_type=F32) / den
    return jnp.concatenate(
        [jnp.where(low, o[cb * t:(cb + 1) * t], o[(Q_PER_KV + cb) * t:(Q_PER_KV + cb + 1) * t])
         for cb in range(Q_PER_KV)], axis=1)


def _route(biased, scores):
    t = biased.shape[1]
    rowf = lax.broadcasted_iota(I32, (N_EXPERTS, t), 0).astype(F32)
    ninf = float("-inf")
    gs = []
    for g in range(N_EXPERT_GROUPS):
        blk = biased[g * GROUP_SIZE:(g + 1) * GROUP_SIZE, :]
        rf = rowf[g * GROUP_SIZE:(g + 1) * GROUP_SIZE, :]
        m1 = jnp.max(blk, axis=0, keepdims=True)
        i1 = jnp.min(jnp.where(blk == m1, rf, float(N_EXPERTS)), axis=0, keepdims=True)
        m2 = jnp.max(jnp.where(rf == i1, ninf, blk), axis=0, keepdims=True)
        gs.append(m1 + m2)
    keep = []
    for g in range(N_EXPERT_GROUPS):
        beat = jnp.zeros((1, t), F32)
        for g2 in range(N_EXPERT_GROUPS):
            if g2 == g:
                continue
            better = (gs[g2] >= gs[g]) if g2 < g else (gs[g2] > gs[g])
            beat = beat + better.astype(F32)
        keep.append(jnp.broadcast_to(beat < float(TOPK_GROUPS), (GROUP_SIZE, t)))
    emask = jnp.concatenate(keep, axis=0)
    masked = jnp.where(emask, biased, NEG_INF)
    idx_rows, w_rows = [], []
    sel_any = jnp.zeros((N_EXPERTS, t), F32)
    for _ in range(TOP_K):
        m = jnp.max(masked, axis=0, keepdims=True)
        ik = jnp.min(jnp.where(masked == m, rowf, float(N_EXPERTS)), axis=0, keepdims=True)
        sel = rowf == ik
        w_rows.append(jnp.sum(jnp.where(sel, scores, 0.0), axis=0, keepdims=True))
        idx_rows.append(ik)
        masked = jnp.where(sel, ninf, masked)
        sel_any = jnp.where(sel, 1.0, sel_any)
    return idx_rows, w_rows, sel_any, rowf


def _mixer_kernel(meta_ref, xp_ref, xs_ref, mod_ref, up_ref, uc_ref, un_ref, q_ref,
                  kp_ref, kc_ref, kn_ref, vp_ref, vc_ref, vn_ref,
                  wpool_ref, bpool_ref, pscale_ref, sink_ref, wout_ref, g2_ref,
                  wr3_ref, rbias_ref, wsgu_ref, wsd_ref, tri_ref, abias_ref,
                  xa_ref, h2p_ref, idx_ref, rank_ref, wts_ref, cnt_ref,
                  uext_ref, kext_ref, vext_ref, base_ref):
    i = pl.program_id(0)
    first = meta_ref[1, i] == 1
    last = meta_ref[2, i] == 1
    pos0 = meta_ref[3, i]
    seq_len = meta_ref[4, i]
    mod = mod_ref[0]
    gt1 = mod[:, 2 * D_MODEL:3 * D_MODEL]
    sh2 = mod[:, 3 * D_MODEL:4 * D_MODEL]
    sc2 = mod[:, 4 * D_MODEL:5 * D_MODEL]
    gt2 = mod[:, 5 * D_MODEL:6 * D_MODEL]

    @pl.when(i == 0)
    def _():
        base_ref[...] = jnp.zeros_like(base_ref)

    uc = uc_ref[...]
    uext_ref[pl.ds(0, HALO), :] = jnp.where(first, 0.0, up_ref[...])
    uext_ref[pl.ds(HALO, T_MIX), :] = uc
    uext_ref[pl.ds(HALO + T_MIX, HALO), :] = jnp.where(last, 0.0, un_ref[...])
    pos = pos0 + lax.broadcasted_iota(I32, (T_MIX, 1), 0)
    a_pool = _pool_mixer(uext_ref, uc, pos, seq_len, wpool_ref, bpool_ref, pscale_ref)

    t = ATT_BLOCK
    kext_ref[pl.ds(0, t), :] = kp_ref[...]
    kext_ref[pl.ds(t, T_MIX), :] = kc_ref[...]
    kext_ref[pl.ds(t + T_MIX, t), :] = kn_ref[...]
    vext_ref[pl.ds(0, t), :] = vp_ref[...]
    vext_ref[pl.ds(t, T_MIX), :] = vc_ref[...]
    vext_ref[pl.ds(t + T_MIX, t), :] = vn_ref[...]
    n_sub = T_MIX // t
    assert n_sub >= 2, "a query block is never both first and last in its sequence"
    variants = []
    for sub in range(n_sub):
        variant = 0
        if sub == 0:
            variant = jnp.where(first, 1, variant)
        if sub == n_sub - 1:
            variant = jnp.where(last, 2, variant)
        variants.append(variant)
    scores = [_attention_scores(q_ref[pl.ds(sub * t, t), :], kext_ref[pl.ds(sub * t, 3 * t), :])
              for sub in range(n_sub)]
    probs = [_attention_probs(scores[sub], variants[sub], abias_ref, sink_ref) for sub in range(n_sub)]
    attn = [_attention_out(probs[sub][0], probs[sub][1], vext_ref[pl.ds(sub * t, 3 * t), :])
            for sub in range(n_sub)]
    a_attn = jnp.concatenate(attn, axis=0)

    a = jnp.concatenate([a_pool, a_attn], axis=1).astype(BF16)
    mix = jnp.dot(a, wout_ref[...], preferred_element_type=F32)

    x = jnp.where(meta_ref[7, i] == 0, xp_ref[...], xs_ref[...])
    x1 = x + gt1 * mix
    h2 = _rms_mod(x1, g2_ref[...], sc2, sh2)
    h2b = h2.astype(BF16)
    planes = _pack_planes(h2)
    for j in range(N_PLANES):
        h2p_ref[j] = planes[j]
    gu = jnp.dot(h2b, wsgu_ref[...], preferred_element_type=F32)
    gate, up = gu[:, :D_SHARED], gu[:, D_SHARED:]
    act = (gate * jax.nn.sigmoid(gate)) * up
    shared = jnp.dot(act.astype(BF16), wsd_ref[...], preferred_element_type=F32)
    xa_ref[...] = x1 + gt2 * shared
    h2lo = (h2 - h2b.astype(F32)).astype(BF16)
    logits = lax.dot_general(wr3_ref[...], jnp.concatenate([h2b, h2b, h2lo], axis=1),
                             (((1,), (1,)), ((), ())), preferred_element_type=F32)
    scores = jax.nn.sigmoid(logits)
    biased = scores + rbias_ref[...]
    routed = [_route(biased[:, c * LANES:(c + 1) * LANES], scores[:, c * LANES:(c + 1) * LANES])
              for c in range(T_MIX // LANES)]
    for c, (idx_rows, w_rows, _, _) in enumerate(routed):
        wsum = w_rows[0]
        for wr in w_rows[1:]:
            wsum = wsum + wr
        wts_t = jnp.concatenate([wr / wsum * ROUTED_SCALE for wr in w_rows], axis=0)
        wpad = jnp.concatenate([wts_t, jnp.zeros((LANES - TOP_K, LANES), F32)], axis=0)
        wts_ref[pl.ds(c * LANES, LANES), :] = wpad.T
        idx_ref[:, pl.ds(c * LANES, LANES)] = jnp.concatenate(idx_rows, axis=0).astype(I32)
    sel_any = jnp.concatenate([r[2] for r in routed], axis=1)
    pref = jnp.dot(sel_any.astype(BF16), tri_ref[...], preferred_element_type=F32)
    before = base_ref[...] + pref[:, :T_MIX]
    for c, (idx_rows, _, _, rowf) in enumerate(routed):
        bc = before[:, c * LANES:(c + 1) * LANES]
        ranks = [jnp.sum(jnp.where(rowf == ik, bc, 0.0), axis=0, keepdims=True) for ik in idx_rows]
        rank_ref[:, pl.ds(c * LANES, LANES)] = jnp.concatenate(ranks, axis=0).astype(I32)
    new_base = base_ref[...] + pref[:, T_MIX:]
    base_ref[...] = new_base
    cnt_ref[...] = new_base


def _mixer(xp, xs, mod3, u, q, k, v, wpool_bf, bpool, pscale, sink, wout_bf, g2, wr3, rbias,
           wsgu_bf, wsd_bf, segments):
    n = _group_tokens(segments)
    t = T_MIX
    nt = n // t
    ab = ATT_BLOCK
    meta = jnp.asarray(_tile_meta(segments, t))
    tri = np.concatenate([np.triu(np.ones((t, t), np.float32), 1), np.ones((t, t), np.float32)], axis=1)
    tri = jnp.asarray(tri, BF16)
    abias = jnp.asarray(_attention_bias())
    cur = lambda w: pl.BlockSpec((t, w), lambda i, m: (i, 0))
    prev = lambda w: pl.BlockSpec((ab, w), lambda i, m: (jnp.maximum(i * (t // ab) - 1, 0), 0))
    nxt = lambda w: pl.BlockSpec((ab, w), lambda i, m: (jnp.minimum((i + 1) * (t // ab), n // ab - 1), 0))
    full = lambda a: pl.BlockSpec(a.shape, lambda i, m: (0,) * a.ndim)
    hb = t // HALO
    in_specs = [
        pl.BlockSpec((t, D_MODEL), lambda i, m: (m[5, i], 0)),
        pl.BlockSpec((t, D_MODEL), lambda i, m: (m[6, i], 0)),
        pl.BlockSpec((1, 1, 6 * D_MODEL), lambda i, m: (m[0, i], 0, 0)),
        pl.BlockSpec((HALO, POOL_WIDTH), lambda i, m: (jnp.maximum(i * hb - 1, 0), 0)),
        cur(POOL_WIDTH),
        pl.BlockSpec((HALO, POOL_WIDTH), lambda i, m: (jnp.minimum((i + 1) * hb, n // HALO - 1), 0)),
        cur(ATTN_WIDTH),
        prev(KV_WIDTH), cur(KV_WIDTH), nxt(KV_WIDTH),
        prev(KV_WIDTH), cur(KV_WIDTH), nxt(KV_WIDTH),
        full(wpool_bf), full(bpool), full(pscale), full(sink), full(wout_bf), full(g2),
        full(wr3), full(rbias), full(wsgu_bf), full(wsd_bf), full(tri), full(abias),
    ]
    out_specs = [
        cur(D_MODEL),
        pl.BlockSpec((N_PLANES, t, LANES), lambda i, m: (0, i, 0)),
        pl.BlockSpec((TOP_K, t), lambda i, m: (0, i)),
        pl.BlockSpec((TOP_K, t), lambda i, m: (0, i)),
        cur(LANES),
        pl.BlockSpec((N_EXPERTS, t), lambda i, m: (0, 0)),
    ]
    out_shape = [
        jax.ShapeDtypeStruct((n, D_MODEL), F32),
        jax.ShapeDtypeStruct((N_PLANES, n, LANES), I32),
        jax.ShapeDtypeStruct((TOP_K, n), I32),
        jax.ShapeDtypeStruct((TOP_K, n), I32),
        jax.ShapeDtypeStruct((n, LANES), F32),
        jax.ShapeDtypeStruct((N_EXPERTS, t), F32),
    ]
    return pl.pallas_call(
        _mixer_kernel,
        grid_spec=pltpu.PrefetchScalarGridSpec(
            num_scalar_prefetch=1, grid=(nt,), in_specs=in_specs, out_specs=out_specs,
            scratch_shapes=[pltpu.VMEM((t + 2 * HALO, POOL_WIDTH), F32),
                            pltpu.VMEM((t + 2 * ab, KV_WIDTH), BF16),
                            pltpu.VMEM((t + 2 * ab, KV_WIDTH), BF16),
                            pltpu.VMEM((N_EXPERTS, t), F32)]),
        out_shape=out_shape,
        compiler_params=pltpu.CompilerParams(dimension_semantics=("arbitrary",),
                                             vmem_limit_bytes=VMEM_LIMIT),
        name="mixer",
    )(meta, xp, xs, mod3, u, u, u, q, k, k, k, v, v, v, wpool_bf, bpool, pscale, sink, wout_bf, g2,
      wr3, rbias, wsgu_bf, wsd_bf, tri, abias)


def _dest_kernel(idx_ref, rank_ref, pstart_ref, o_ref, *, n_rows):
    slab = min(T_DEST_SLAB, idx_ref.shape[1])
    rowi = lax.broadcasted_iota(I32, (N_EXPERTS, slab), 0)
    pstart = pstart_ref[...]
    for c in range(idx_ref.shape[1] // slab):
        lanes = pl.ds(c * slab, slab)
        rows = []
        for kk in range(TOP_K):
            sel = rowi == idx_ref[kk:kk + 1, lanes]
            start = jnp.sum(jnp.where(sel, pstart, 0.0), axis=0, keepdims=True)
            rows.append(start.astype(I32) + rank_ref[kk:kk + 1, lanes])
        dest = jnp.concatenate(rows, axis=0)
        for j in range(N_PLANES):
            o_ref[:, j, lanes] = dest + j * n_rows


def _dest(idx_t, rank_t, pstart_col, n_rows):
    n = idx_t.shape[1]
    td = int(np.gcd(n, T_DEST))
    return pl.pallas_call(
        functools.partial(_dest_kernel, n_rows=n_rows),
        grid=(n // td,),
        in_specs=[pl.BlockSpec((TOP_K, td), lambda i: (0, i)),
                  pl.BlockSpec((TOP_K, td), lambda i: (0, i)),
                  pl.BlockSpec((N_EXPERTS, 1), lambda i: (0, 0))],
        out_specs=pl.BlockSpec((TOP_K, N_PLANES, td), lambda i: (0, 0, i)),
        out_shape=jax.ShapeDtypeStruct((TOP_K, N_PLANES, n), I32),
        name="dest",
    )(idx_t, rank_t, pstart_col)


def _sc_mesh():
    return plsc.VectorSubcoreMesh(core_axis_name="c", subcore_axis_name="s")


def _sc_worker():
    return lax.axis_index("s") * 2 + lax.axis_index("c")


def _sc_scatter_rows(table, didx, n_out_rows):
    m = table.shape[0]
    per_worker = m // SC_ROWS // SC_WORKERS
    assert per_worker % 2 == 0

    @functools.partial(
        pl.kernel, mesh=_sc_mesh(),
        out_type=jax.ShapeDtypeStruct((n_out_rows, LANES), I32),
        scratch_types=[pltpu.VMEM((2, SC_ROWS, LANES), I32), pltpu.VMEM((2, TOP_K, SC_ROWS), I32),
                       pltpu.SemaphoreType.DMA((2,)), pltpu.SemaphoreType.DMA((2,))])
    def run(table_hbm, didx_hbm, out_hbm, rows, idx, lsem, ssem):
        wid = _sc_worker()

        def loads(step, p):
            off = pl.multiple_of((wid * per_worker + step) * SC_ROWS, SC_ROWS)
            return (pltpu.make_async_copy(table_hbm.at[pl.ds(off, SC_ROWS)], rows.at[p], lsem.at[p]),
                    pltpu.make_async_copy(didx_hbm.at[:, pl.ds(off, SC_ROWS)], idx.at[p], lsem.at[p]))

        def scatter(p, kk):
            return pltpu.make_async_copy(rows.at[p], out_hbm.at[idx.at[p].at[kk]], ssem.at[p])

        for cp in loads(0, 0):
            cp.start()

        @pl.loop(0, per_worker, step=2)
        def _(s0):
            for p in range(2):
                step = s0 + p
                for cp in loads(step, p):
                    cp.wait()
                for kk in range(TOP_K):
                    scatter(p, kk).start()

                @pl.when(step > 0)
                def _():
                    for kk in range(TOP_K):
                        scatter(1 - p, kk).wait()

                @pl.when(step + 1 < per_worker)
                def _():
                    for cp in loads(step + 1, 1 - p):
                        cp.start()

        for kk in range(TOP_K):
            scatter(1, kk).wait()

    return run(table, didx)


def _sc_gather_rows(table, didx):
    m = didx.shape[1]
    per_worker = m // SC_ROWS // SC_WORKERS
    assert per_worker % 2 == 0 and (2 * TOP_K) % SC_NBUF == 0
    items = 2 * TOP_K

    @functools.partial(
        pl.kernel, mesh=_sc_mesh(),
        out_type=jax.ShapeDtypeStruct((TOP_K, m, LANES), I32),
        scratch_types=[pltpu.VMEM((SC_NBUF, SC_ROWS, LANES), I32), pltpu.VMEM((2, TOP_K, SC_ROWS), I32),
                       pltpu.SemaphoreType.DMA((SC_NBUF,)), pltpu.SemaphoreType.DMA((SC_NBUF,)),
                       pltpu.SemaphoreType.DMA((2,))])
    def run(table_hbm, didx_hbm, out_hbm, rows, idx, gsem, wsem, isem):
        wid = _sc_worker()

        def off_of(step):
            return pl.multiple_of((wid * per_worker + step) * SC_ROWS, SC_ROWS)

        def idx_copy(step, p):
            return pltpu.make_async_copy(didx_hbm.at[:, pl.ds(off_of(step), SC_ROWS)], idx.at[p], isem.at[p])

        def gather(j):
            r = j % SC_NBUF
            return pltpu.make_async_copy(table_hbm.at[idx.at[j // TOP_K].at[j % TOP_K]], rows.at[r], gsem.at[r])

        def write(s0, j):
            r = j % SC_NBUF
            return pltpu.make_async_copy(rows.at[r], out_hbm.at[j % TOP_K, pl.ds(off_of(s0 + j // TOP_K), SC_ROWS)],
                                         wsem.at[r])

        def retire_write(s0, j):
            if j >= 0:
                write(s0, j).wait()
            else:
                @pl.when(s0 > 0)
                def _():
                    write(s0 - 2, j + items).wait()

        def finish_read(s0, j):
            if j >= 0:
                gather(j).wait()
                write(s0, j).start()
            else:
                @pl.when(s0 > 0)
                def _():
                    gather(j + items).wait()
                    write(s0 - 2, j + items).start()

        idx_copy(0, 0).start()

        @pl.loop(0, per_worker, step=2)
        def _(s0):
            for j in range(items):
                p, kk = j // TOP_K, j % TOP_K
                if kk == 0:
                    idx_copy(s0 + p, p).wait()
                retire_write(s0, j - SC_NBUF)
                gather(j).start()
                finish_read(s0, j - SC_LAG)
                if kk == SC_LAG:
                    @pl.when(s0 + p + 1 < per_worker)
                    def _():
                        idx_copy(s0 + p + 1, 1 - p).start()

        last = per_worker - 2
        for j in range(items - SC_LAG, items):
            gather(j).wait()
            write(last, j).start()
        for j in range(items - SC_NBUF, items):
            write(last, j).wait()

    return run(table, didx)


def _experts_kernel(c0_ref, ntot_ref, tsrc_ref, trow_ref, tvalid_ref, total_ref, *refs, n_groups):
    x_hbms = refs[:n_groups]
    wg_ref, wu_ref, wd_ref = refs[n_groups:n_groups + 3]
    y_hbms = refs[n_groups + 3:2 * n_groups + 3]
    wgu_s, wd_s, xbuf, ybuf, xsem, ysem = refs[2 * n_groups + 3:]
    e = pl.program_id(0)
    nb = xbuf.shape[0]
    ch = xbuf.shape[2]
    total = total_ref[0]

    def x_copy(c, g, row):
        slot = lax.rem(c, nb)
        return pltpu.make_async_copy(x_hbms[g].at[:, pl.ds(row, ch), :], xbuf.at[slot], xsem.at[slot])

    def y_copy(c, g, row):
        slot = lax.rem(c, nb)
        return pltpu.make_async_copy(ybuf.at[slot], y_hbms[g].at[:, pl.ds(row, ch), :], ysem.at[slot])

    def start(copy, c):
        row = pl.multiple_of(trow_ref[c], ch)
        if n_groups == 1:
            copy(c, 0, row).start()
            return
        for g in range(n_groups):
            @pl.when(tsrc_ref[c] == g)
            def _():
                copy(c, g, row).start()

    def wait(copy, c):
        copy(c, 0, 0).wait()

    @pl.when(e == 0)
    def _():
        for c in range(nb - 1):
            @pl.when(c < total)
            def _():
                start(x_copy, c)

    for j in range(N_PLANES):
        hi = pl.ds(j * LANES, LANES)
        lo = pl.ds(HALF + j * LANES, LANES)
        r_hi = pl.ds(2 * j * LANES, LANES)
        r_lo = pl.ds((2 * j + 1) * LANES, LANES)
        wgu_s[r_hi, pl.ds(0, D_EXPERT)] = wg_ref[0, hi, :].astype(BF16)
        wgu_s[r_lo, pl.ds(0, D_EXPERT)] = wg_ref[0, lo, :].astype(BF16)
        wgu_s[r_hi, pl.ds(D_EXPERT, D_EXPERT)] = wu_ref[0, hi, :].astype(BF16)
        wgu_s[r_lo, pl.ds(D_EXPERT, D_EXPERT)] = wu_ref[0, lo, :].astype(BF16)
    wd_s[...] = wd_ref[0].astype(BF16)

    c0 = c0_ref[e]

    def chunk(i, carry):
        c = c0 + i
        slot = lax.rem(c, nb)

        @pl.when(c + nb - 1 < total)
        def _():
            start(x_copy, c + nb - 1)

        wait(x_copy, c)

        @pl.when(c >= nb)
        def _():
            wait(y_copy, c - nb)

        valid = tvalid_ref[c]

        def swiglu_rows(n_rows):
            rows = pl.ds(0, n_rows)
            live = lax.broadcasted_iota(I32, (n_rows, LANES), 0) < valid
            xs = []
            for j in range(N_PLANES):
                hi, lo = _unpack_plane(jnp.where(live, xbuf[slot, j, rows, :], 0))
                xs += [hi.astype(BF16), lo.astype(BF16)]
            acc = jnp.dot(jnp.concatenate(xs, axis=1), wgu_s[...], preferred_element_type=F32)
            gate, up = acc[:, :D_EXPERT], acc[:, D_EXPERT:]
            act = (gate * jax.nn.sigmoid(gate)) * up
            y = jnp.dot(act.astype(BF16), wd_s[...], preferred_element_type=F32)
            planes = _pack_planes(y)
            for j in range(N_PLANES):
                ybuf[slot, j, rows, :] = planes[j]
                if n_rows < ch:
                    ybuf[slot, j, pl.ds(n_rows, ch - n_rows), :] = jnp.zeros((ch - n_rows, LANES), I32)

        pl.when(valid > ch // 2)(lambda: swiglu_rows(ch))
        pl.when(valid <= ch // 2)(lambda: swiglu_rows(ch // 2))
        start(y_copy, c)
        return carry

    lax.fori_loop(0, ntot_ref[e], chunk, 0)

    @pl.when(e == pl.num_programs(0) - 1)
    def _():
        for back in range(nb, 0, -1):
            @pl.when(total >= back)
            def _():
                wait(y_copy, total - back)


def _chunk_tables(nchunks, cstarts, counts, max_chunks):
    ch = EXPERT_CHUNK
    ntot = sum(nchunks)
    cum = jnp.cumsum(ntot)
    c0 = cum - ntot
    c = jnp.arange(max_chunks, dtype=I32)
    e = jnp.minimum(jnp.sum((c[:, None] >= cum[None, :]).astype(I32), axis=1), N_EXPERTS - 1)
    own = e[:, None] == jnp.arange(N_EXPERTS, dtype=I32)[None, :]
    pick = lambda v: jnp.sum(jnp.where(own, v[None, :], 0), axis=1)
    i = c - pick(c0)
    src = jnp.zeros_like(c)
    row = jnp.zeros_like(c)
    valid = jnp.zeros_like(c)
    lo = jnp.zeros_like(c)
    for g in range(len(nchunks)):
        n_g = pick(nchunks[g])
        inside = (i >= lo) & (i < lo + n_g) & (c < cum[-1])
        loc = i - lo
        src = jnp.where(inside, g, src)
        row = jnp.where(inside, (pick(cstarts[g]) + loc) * ch, row)
        valid = jnp.where(inside, jnp.clip(pick(counts[g]) - loc * ch, 0, ch), valid)
        lo = lo + n_g
    return c0.astype(I32), ntot.astype(I32), src, row, valid, cum[-1:].astype(I32)


def _experts(xs_planes, w_gate, w_up, w_down, nchunks, cstarts, counts):
    ng = len(xs_planes)
    ch = EXPERT_CHUNK
    tables = _chunk_tables(nchunks, cstarts, counts, sum(x.shape[1] // ch for x in xs_planes))
    w_map = lambda e, *_: (e, 0, 0)
    any_space = pl.BlockSpec(memory_space=pl.ANY)
    return pl.pallas_call(
        functools.partial(_experts_kernel, n_groups=ng),
        grid_spec=pltpu.PrefetchScalarGridSpec(
            num_scalar_prefetch=6, grid=(N_EXPERTS,),
            in_specs=[any_space] * ng + [pl.BlockSpec((1, D_MODEL, D_EXPERT), w_map),
                                         pl.BlockSpec((1, D_MODEL, D_EXPERT), w_map),
                                         pl.BlockSpec((1, D_EXPERT, D_MODEL), w_map)],
            out_specs=[any_space] * ng,
            scratch_shapes=[pltpu.VMEM((D_MODEL, 2 * D_EXPERT), BF16),
                            pltpu.VMEM((D_EXPERT, D_MODEL), BF16),
                            pltpu.VMEM((EXPERT_NBUF, N_PLANES, ch, LANES), I32),
                            pltpu.VMEM((EXPERT_NBUF, N_PLANES, ch, LANES), I32),
                            pltpu.SemaphoreType.DMA((EXPERT_NBUF,)),
                            pltpu.SemaphoreType.DMA((EXPERT_NBUF,))]),
        out_shape=[jax.ShapeDtypeStruct(x.shape, I32) for x in xs_planes],
        compiler_params=pltpu.CompilerParams(dimension_semantics=("arbitrary",),
                                             vmem_limit_bytes=VMEM_LIMIT),
        name="experts",
    )(*tables, *xs_planes, w_gate, w_up, w_down)


def _final_kernel(xa_ref, yk_ref, wts_ref, mod_ref, g_ref, o_ref):
    wts = wts_ref[...]
    his = [jnp.zeros((T_FIN, LANES), F32) for _ in range(N_PLANES)]
    los = [jnp.zeros((T_FIN, LANES), F32) for _ in range(N_PLANES)]
    for kk in range(TOP_K):
        wk = wts[:, kk:kk + 1]
        for j in range(N_PLANES):
            hi, lo = _unpack_plane(yk_ref[kk, j])
            his[j] = his[j] + wk * hi
            los[j] = los[j] + wk * lo
    routed = jnp.concatenate(his + los, axis=1)
    gt2 = mod_ref[0][:, 5 * D_MODEL:6 * D_MODEL]
    x = xa_ref[...] + gt2 * routed
    ms = jnp.mean(x * x, axis=-1, keepdims=True)
    o_ref[...] = x * lax.rsqrt(ms + EPS) * g_ref[...]


def _final_kernel_into(xa_ref, yk_ref, wts_ref, mod_ref, g_ref, prev_ref, o_ref):
    _final_kernel(xa_ref, yk_ref, wts_ref, mod_ref, g_ref, o_ref)


def _final(xa, yk, wts, mod3, g_final, tile_off, n_tok, seq_len, sid0, out_rows, out_tile_off, prev):
    tiles_per_seq = seq_len // T_FIN
    in_specs = [pl.BlockSpec((T_FIN, D_MODEL), lambda i: (i + tile_off, 0)),
                pl.BlockSpec((TOP_K, N_PLANES, T_FIN, LANES), lambda i: (0, 0, i + tile_off, 0)),
                pl.BlockSpec((T_FIN, LANES), lambda i: (i + tile_off, 0)),
                pl.BlockSpec((1, 1, 6 * D_MODEL), lambda i: (sid0 + i // tiles_per_seq, 0, 0)),
                pl.BlockSpec((1, D_MODEL), lambda i: (0, 0))]
    args = [xa, yk, wts, mod3, g_final]
    if prev is not None:
        in_specs.append(pl.BlockSpec(memory_space=pl.ANY))
        args.append(prev)
    return pl.pallas_call(
        _final_kernel if prev is None else _final_kernel_into,
        grid=(n_tok // T_FIN,),
        in_specs=in_specs,
        out_specs=pl.BlockSpec((T_FIN, D_MODEL), lambda i: (i + out_tile_off, 0)),
        out_shape=jax.ShapeDtypeStruct((out_rows, D_MODEL), F32),
        input_output_aliases={} if prev is None else {len(args) - 1: 0},
        compiler_params=pltpu.CompilerParams(dimension_semantics=("arbitrary",),
                                             vmem_limit_bytes=VMEM_LIMIT),
        name="final",
    )(*args)


def _token_groups(bp, sp, bs, ss):
    return [[(0, 0, bp, sp, 0), (1, 0, bs, ss, bp)]]


def kernel(x_prompt, x_sample, c_prompt, c_sample, w_ada, b_ada, g_norm1, w_in, w_pool, b_pool, pool_scale, attn_sink, w_out, g_norm2, w_router, router_bias, w_gate, w_up, w_down, ws_gate, ws_up, ws_down, g_final):
    assert w_ada.shape[0] == 1, "one layer"
    bp, sp, d = x_prompt.shape
    bs, ss, _ = x_sample.shape
    assert d == D_MODEL and bp + bs <= ADA_ROWS
    n_p, n_s = bp * sp, bs * ss
    xp = x_prompt.reshape(n_p, d)
    xs = x_sample.reshape(n_s, d)

    c_all = jnp.concatenate([c_prompt, c_sample, jnp.zeros((ADA_ROWS - bp - bs, d), F32)], axis=0)
    mod3 = _ada(c_all, w_ada[0], b_ada[0]).reshape(ADA_ROWS, 1, 6 * d)

    q0, q1 = POOL_WIDTH, POOL_WIDTH + ATTN_WIDTH
    wq = w_in[0][:, q0:q1].reshape(d, N_KV_HEADS, Q_PER_KV, HEAD_DIM).transpose(0, 2, 1, 3).reshape(d, ATTN_WIDTH)
    w_in_p = jnp.concatenate([w_in[0][:, :q0], wq, w_in[0][:, q1:]], axis=1).astype(BF16)
    wo = w_out[0][q0:].reshape(N_KV_HEADS, Q_PER_KV, HEAD_DIM, d).transpose(1, 0, 2, 3).reshape(ATTN_WIDTH, d)
    w_out_p = jnp.concatenate([w_out[0][:q0], wo], axis=0).astype(BF16)
    wsgu = jnp.concatenate([ws_gate[0], ws_up[0]], axis=1).astype(BF16)
    wr_hi = w_router[0].astype(BF16)
    wr_lo = (w_router[0] - wr_hi.astype(F32)).astype(BF16)
    wr3 = jnp.concatenate([wr_hi, wr_lo, wr_hi], axis=0).T
    g1 = g_norm1[0].reshape(1, d)
    mixer_weights = (
        w_pool[0].astype(BF16), b_pool[0].reshape(1, POOL_WIDTH), pool_scale[0].reshape(1, POOL_WIDTH),
        attn_sink[0].reshape(1, N_HEADS), w_out_p, g_norm2[0].reshape(1, d),
        wr3, router_bias[0].reshape(N_EXPERTS, 1), wsgu, ws_down[0].astype(BF16))
    ch = EXPERT_CHUNK

    groups = _token_groups(bp, sp, bs, ss)
    staged = []
    for segments in groups:
        n = _group_tokens(segments)
        u, q, k, v = _inproj(xp, xs, mod3, g1, w_in_p, segments)
        xa, h2p, idx_t, rank_t, wts, cnt = _mixer(xp, xs, mod3, u, q, k, v, *mixer_weights, segments)
        n_rows = (-(-n * TOP_K // ch) + N_EXPERTS) * ch
        counts = cnt[:, 0].astype(I32)
        padded = (counts + ch - 1) // ch * ch
        pstart = jnp.cumsum(padded) - padded
        didx = _dest(idx_t, rank_t, pstart.astype(F32).reshape(N_EXPERTS, 1), n_rows)
        didx = didx.reshape(TOP_K, N_PLANES * n)
        xs_rows = _sc_scatter_rows(h2p.reshape(N_PLANES * n, LANES), didx, N_PLANES * n_rows)
        staged.append((xa, wts, didx, xs_rows.reshape(N_PLANES, n_rows, LANES), padded // ch, pstart // ch, counts))

    ys = _experts([st[3] for st in staged], w_gate[0], w_up[0], w_down[0],
                  [st[4] for st in staged], [st[5] for st in staged], [st[6] for st in staged])

    gf = g_final.reshape(1, d)
    outs = [None, None]
    out_rows = (n_p, n_s)
    for segments, (xa, wts, didx, xs_rows, _, _, _), ys_rows in zip(groups, staged, ys):
        n = _group_tokens(segments)
        yk = _sc_gather_rows(ys_rows.reshape(N_PLANES * ys_rows.shape[1], LANES), didx)
        yk = yk.reshape(TOP_K, N_PLANES, n, LANES)
        tok = 0
        for src, tok0, n_seqs, seq_len, sid0 in segments:
            n_tok = n_seqs * seq_len
            outs[src] = _final(xa, yk, wts, mod3, gf, tok // T_FIN, n_tok, seq_len, sid0,
                               out_rows[src], tok0 // T_FIN, outs[src])
            tok += n_tok
    return outs[0].reshape(bp, sp, d), outs[1].reshape(bs, ss, d)
```

```python
import functools

import numpy as np
import jax
import jax.numpy as jnp
from jax import lax
from jax.experimental import pallas as pl
from jax.experimental.pallas import tpu as pltpu
from jax.experimental.pallas import tpu_sc as plsc

F32 = jnp.float32
BF16 = jnp.bfloat16
I32 = jnp.int32

D_MODEL = 1024
POOL_WINDOWS = (2, 4, 8, 16)
POOL_WIDTH = 512
POOL_GROUP = 128
N_HEADS = 8
N_KV_HEADS = 2
HEAD_DIM = 64
Q_PER_KV = N_HEADS // N_KV_HEADS
ATTN_WIDTH = N_HEADS * HEAD_DIM
KV_WIDTH = N_KV_HEADS * HEAD_DIM
D_IN_PROJ = POOL_WIDTH + ATTN_WIDTH + 2 * KV_WIDTH
WINDOW = 128
N_EXPERTS = 256
TOP_K = 8
N_EXPERT_GROUPS = 8
GROUP_SIZE = N_EXPERTS // N_EXPERT_GROUPS
TOPK_GROUPS = 4
D_EXPERT = 256
D_SHARED = 256
ROUTED_SCALE = 2.5
EPS = 1e-6
NEG_INF = -1e30

LANES = 128
HALO = 8
N_PLANES = 4
HALF = D_MODEL // 2

T_IN = 1024
T_MIX = 512
ATT_BLOCK = 128
T_DEST = 4096
T_DEST_SLAB = 512
EXPERT_CHUNK = 512
EXPERT_NBUF = 4
T_FIN = 512
SC_ROWS = 128
SC_WORKERS = 32
SC_NBUF = 4
SC_LAG = 2
ADA_ROWS = 16
VMEM_LIMIT = 48 * 1024 * 1024

LOG2E = 1.4426950408889634
ALIBI_SLOPES = tuple(float(2.0 ** (-8.0 * (h + 1) / N_HEADS)) for h in range(N_HEADS))


def _pack_planes(y):
    planes = []
    for j in range(N_PLANES):
        hi = y[:, j * LANES:(j + 1) * LANES].astype(BF16).astype(F32)
        lo = y[:, HALF + j * LANES:HALF + (j + 1) * LANES].astype(BF16).astype(F32)
        hb = lax.bitcast_convert_type(hi, jnp.uint32) & jnp.uint32(0xFFFF0000)
        lb = lax.bitcast_convert_type(lo, jnp.uint32) >> jnp.uint32(16)
        planes.append(lax.bitcast_convert_type(hb | lb, I32))
    return planes


def _unpack_plane(w):
    u = lax.bitcast_convert_type(w, jnp.uint32)
    hi = lax.bitcast_convert_type(u & jnp.uint32(0xFFFF0000), F32)
    lo = lax.bitcast_convert_type(u << jnp.uint32(16), F32)
    return hi, lo


def _ada_kernel(c_ref, w_ref, b_ref, o_ref):
    c = c_ref[...]
    s = c * jax.nn.sigmoid(c)
    o_ref[...] = jnp.dot(s, w_ref[...], precision=lax.Precision.HIGHEST,
                         preferred_element_type=F32) + b_ref[...]


def _ada(c_all, w_ada, b_ada):
    n_out = w_ada.shape[1]
    tn = 1024
    return pl.pallas_call(
        _ada_kernel,
        grid=(n_out // tn,),
        in_specs=[pl.BlockSpec((ADA_ROWS, D_MODEL), lambda j: (0, 0)),
                  pl.BlockSpec((D_MODEL, tn), lambda j: (0, j)),
                  pl.BlockSpec((1, tn), lambda j: (0, j))],
        out_specs=pl.BlockSpec((ADA_ROWS, tn), lambda j: (0, j)),
        out_shape=jax.ShapeDtypeStruct((ADA_ROWS, n_out), F32),
        name="ada",
    )(c_all, w_ada, b_ada.reshape(1, n_out))


def _rms_mod(x, g, scale, shift):
    ms = jnp.mean(x * x, axis=-1, keepdims=True)
    return (x * lax.rsqrt(ms + EPS) * g) * (1.0 + scale) + shift


def _inproj_kernel(meta_ref, xp_ref, xs_ref, mod_ref, g_ref, w_ref, u_ref, q_ref, k_ref, v_ref):
    x = jnp.where(meta_ref[7, pl.program_id(0)] == 0, xp_ref[...], xs_ref[...])
    mod = mod_ref[0]
    h = _rms_mod(x, g_ref[...], mod[:, D_MODEL:2 * D_MODEL], mod[:, 0:D_MODEL])
    z = jnp.dot(h.astype(BF16), w_ref[...], preferred_element_type=F32)
    u_ref[...] = z[:, :POOL_WIDTH]
    q_ref[...] = (z[:, POOL_WIDTH:POOL_WIDTH + ATTN_WIDTH] * (LOG2E * HEAD_DIM ** -0.5)).astype(BF16)
    k_ref[...] = z[:, POOL_WIDTH + ATTN_WIDTH:POOL_WIDTH + ATTN_WIDTH + KV_WIDTH].astype(BF16)
    v_ref[...] = z[:, POOL_WIDTH + ATTN_WIDTH + KV_WIDTH:].astype(BF16)


def _tile_meta(segments, tile):
    rows = []
    for src, tok0, n_seqs, seq_len, sid0 in segments:
        per = seq_len // tile
        for q in range(n_seqs):
            for t in range(per):
                blk = (tok0 + q * seq_len) // tile + t
                rows.append((sid0 + q, int(t == 0), int(t == per - 1), t * tile, seq_len,
                             blk if src == 0 else -1, blk if src == 1 else -1, src))
    meta = np.asarray(rows, np.int32)
    for col in (5, 6):
        known = meta[:, col] >= 0
        if not known.any():
            meta[:, col] = 0
            continue
        last = np.maximum.accumulate(np.where(known, np.arange(len(meta)), -1))
        last = np.where(last < 0, np.argmax(known), last)
        meta[:, col] = meta[last, col]
    return meta.T.copy()


def _group_tokens(segments):
    return sum(n_seqs * seq_len for _, _, n_seqs, seq_len, _ in segments)


def _inproj(xp, xs, mod3, g1, w_in_bf, segments):
    n = _group_tokens(segments)
    t_in = int(np.gcd.reduce([T_IN] + [seq_len for _, _, _, seq_len, _ in segments]))
    meta = jnp.asarray(_tile_meta(segments, t_in))
    tok = lambda w: pl.BlockSpec((t_in, w), lambda i, m: (i, 0))
    return pl.pallas_call(
        _inproj_kernel,
        grid_spec=pltpu.PrefetchScalarGridSpec(
            num_scalar_prefetch=1,
            grid=(n // t_in,),
            in_specs=[
                pl.BlockSpec((t_in, D_MODEL), lambda i, m: (m[5, i], 0)),
                pl.BlockSpec((t_in, D_MODEL), lambda i, m: (m[6, i], 0)),
                pl.BlockSpec((1, 1, 6 * D_MODEL), lambda i, m: (m[0, i], 0, 0)),
                pl.BlockSpec((1, D_MODEL), lambda i, m: (0, 0)),
                pl.BlockSpec((D_MODEL, D_IN_PROJ), lambda i, m: (0, 0)),
            ],
            out_specs=[tok(POOL_WIDTH), tok(ATTN_WIDTH), tok(KV_WIDTH), tok(KV_WIDTH)],
        ),
        out_shape=[jax.ShapeDtypeStruct((n, POOL_WIDTH), F32),
                   jax.ShapeDtypeStruct((n, ATTN_WIDTH), BF16),
                   jax.ShapeDtypeStruct((n, KV_WIDTH), BF16),
                   jax.ShapeDtypeStruct((n, KV_WIDTH), BF16)],
        compiler_params=pltpu.CompilerParams(dimension_semantics=("arbitrary",),
                                             vmem_limit_bytes=VMEM_LIMIT),
        name="inproj",
    )(meta, xp, xs, mod3, g1, w_in_bf)


def _pool_mixer(uext_ref, uc, pos, seq_len, wpool_ref, bpool_ref, pscale_ref):
    outs = []
    for gi, w in enumerate(POOL_WINDOWS):
        c0 = gi * POOL_GROUP
        half = w // 2
        acc = uext_ref[pl.ds(HALO - half, T_MIX), pl.ds(c0, POOL_GROUP)]
        for o in range(-half + 1, half):
            acc = acc + uext_ref[pl.ds(HALO + o, T_MIX), pl.ds(c0, POOL_GROUP)]
        lo = jnp.maximum(pos - half, 0)
        hi = jnp.minimum(pos + half, seq_len)
        cnt = (hi - lo).astype(F32)
        d = acc / cnt - uc[:, c0:c0 + POOL_GROUP]
        y = jnp.dot(d.astype(BF16), wpool_ref[gi], preferred_element_type=F32)
        y = (y + bpool_ref[:, c0:c0 + POOL_GROUP]) * pscale_ref[:, c0:c0 + POOL_GROUP]
        outs.append(y)
    return jnp.concatenate(outs, axis=1)


def _attention_bias():
    t = ATT_BLOCK
    r = np.arange(t)[:, None]
    c = np.arange(3 * t)[None, :]
    dist = np.abs(r - (c - t))
    band = dist <= WINDOW
    out = np.empty((3, N_HEADS, t, 3 * t), np.float32)
    for var, valid in enumerate((band, band & (c >= t), band & (c < 2 * t))):
        for h in range(N_HEADS):
            out[var, h] = np.where(valid, -ALIBI_SLOPES[h] * LOG2E * dist, NEG_INF)
    return out.reshape(3, N_HEADS * t, 3 * t)


def _attention_scores(q, kw):
    t = ATT_BLOCK
    low = lax.broadcasted_iota(I32, (t, LANES), 1) < HEAD_DIM
    zero = jnp.zeros((t, LANES), BF16)
    qs = []
    for h in range(N_HEADS):
        blk = q[:, (h % Q_PER_KV) * LANES:(h % Q_PER_KV + 1) * LANES]
        qs.append(jnp.where(low, blk, zero) if h < Q_PER_KV else jnp.where(low, zero, blk))
    return lax.dot_general(jnp.concatenate(qs, axis=0), kw, (((1,), (1,)), ((), ())),
                           preferred_element_type=F32)


def _attention_probs(s, variant, bias_ref, sink_ref):
    t = ATT_BLOCK
    sh = s + bias_ref[variant]
    sink = jnp.concatenate([jnp.broadcast_to(sink_ref[:, h:h + 1] * LOG2E, (t, 1)) for h in range(N_HEADS)], axis=0)
    m = jnp.maximum(jnp.max(sh, axis=-1, keepdims=True), sink)
    p = jnp.exp2(sh - m)
    den = jnp.sum(p, axis=-1, keepdims=True) + jnp.exp2(sink - m)
    return p.astype(BF16), den


def _attention_out(p, den, vw):
    t = ATT_BLOCK
    low = lax.broadcasted_iota(I32, (t, LANES), 1) < HEAD_DIM
    o = jnp.dot(p, vw, preferred_element_type=F32) / den
    return jnp.concatenate(
        [jnp.where(low, o[cb * t:(cb + 1) * t], o[(Q_PER_KV + cb) * t:(Q_PER_KV + cb + 1) * t])
         for cb in range(Q_PER_KV)], axis=1)


def _route(biased, scores):
    t = biased.shape[1]
    rowf = lax.broadcasted_iota(I32, (N_EXPERTS, t), 0).astype(F32)
    ninf = float("-inf")
    gs = []
    for g in range(N_EXPERT_GROUPS):
        blk = biased[g * GROUP_SIZE:(g + 1) * GROUP_SIZE, :]
        rf = rowf[g * GROUP_SIZE:(g + 1) * GROUP_SIZE, :]
        m1 = jnp.max(blk, axis=0, keepdims=True)
        i1 = jnp.min(jnp.where(blk == m1, rf, float(N_EXPERTS)), axis=0, keepdims=True)
        m2 = jnp.max(jnp.where(rf == i1, ninf, blk), axis=0, keepdims=True)
        gs.append(m1 + m2)
    keep = []
    for g in range(N_EXPERT_GROUPS):
        beat = jnp.zeros((1, t), F32)
        for g2 in range(N_EXPERT_GROUPS):
            if g2 == g:
                continue
            better = (gs[g2] >= gs[g]) if g2 < g else (gs[g2] > gs[g])
            beat = beat + better.astype(F32)
        keep.append(jnp.broadcast_to(beat < float(TOPK_GROUPS), (GROUP_SIZE, t)))
    emask = jnp.concatenate(keep, axis=0)
    masked = jnp.where(emask, biased, NEG_INF)
    idx_rows, w_rows = [], []
    sel_any = jnp.zeros((N_EXPERTS, t), F32)
    for _ in range(TOP_K):
        m = jnp.max(masked, axis=0, keepdims=True)
        ik = jnp.min(jnp.where(masked == m, rowf, float(N_EXPERTS)), axis=0, keepdims=True)
        sel = rowf == ik
        w_rows.append(jnp.sum(jnp.where(sel, scores, 0.0), axis=0, keepdims=True))
        idx_rows.append(ik)
        masked = jnp.where(sel, ninf, masked)
        sel_any = jnp.where(sel, 1.0, sel_any)
    return idx_rows, w_rows, sel_any, rowf


def _mixer_kernel(meta_ref, xp_ref, xs_ref, mod_ref, up_ref, uc_ref, un_ref, q_ref,
                  kp_ref, kc_ref, kn_ref, vp_ref, vc_ref, vn_ref,
                  wpool_ref, bpool_ref, pscale_ref, sink_ref, wout_ref, g2_ref,
                  wr3_ref, rbias_ref, wsgu_ref, wsd_ref, tri_ref, abias_ref,
                  xa_ref, h2p_ref, idx_ref, rank_ref, wts_ref, cnt_ref,
                  uext_ref, kext_ref, vext_ref, base_ref):
    i = pl.program_id(0)
    first = meta_ref[1, i] == 1
    last = meta_ref[2, i] == 1
    pos0 = meta_ref[3, i]
    seq_len = meta_ref[4, i]
    mod = mod_ref[0]
    gt1 = mod[:, 2 * D_MODEL:3 * D_MODEL]
    sh2 = mod[:, 3 * D_MODEL:4 * D_MODEL]
    sc2 = mod[:, 4 * D_MODEL:5 * D_MODEL]
    gt2 = mod[:, 5 * D_MODEL:6 * D_MODEL]

    @pl.when(i == 0)
    def _():
        base_ref[...] = jnp.zeros_like(base_ref)

    uc = uc_ref[...]
    uext_ref[pl.ds(0, HALO), :] = jnp.where(first, 0.0, up_ref[...])
    uext_ref[pl.ds(HALO, T_MIX), :] = uc
    uext_ref[pl.ds(HALO + T_MIX, HALO), :] = jnp.where(last, 0.0, un_ref[...])
    pos = pos0 + lax.broadcasted_iota(I32, (T_MIX, 1), 0)
    a_pool = _pool_mixer(uext_ref, uc, pos, seq_len, wpool_ref, bpool_ref, pscale_ref)

    t = ATT_BLOCK
    kext_ref[pl.ds(0, t), :] = kp_ref[...]
    kext_ref[pl.ds(t, T_MIX), :] = kc_ref[...]
    kext_ref[pl.ds(t + T_MIX, t), :] = kn_ref[...]
    vext_ref[pl.ds(0, t), :] = vp_ref[...]
    vext_ref[pl.ds(t, T_MIX), :] = vc_ref[...]
    vext_ref[pl.ds(t + T_MIX, t), :] = vn_ref[...]
    n_sub = T_MIX // t
    assert n_sub >= 2, "a query block is never both first and last in its sequence"
    variants = []
    for sub in range(n_sub):
        variant = 0
        if sub == 0:
            variant = jnp.where(first, 1, variant)
        if sub == n_sub - 1:
            variant = jnp.where(last, 2, variant)
        variants.append(variant)
    scores = [_attention_scores(q_ref[pl.ds(sub * t, t), :], kext_ref[pl.ds(sub * t, 3 * t), :])
              for sub in range(n_sub)]
    probs = [_attention_probs(scores[sub], variants[sub], abias_ref, sink_ref) for sub in range(n_sub)]
    attn = [_attention_out(probs[sub][0], probs[sub][1], vext_ref[pl.ds(sub * t, 3 * t), :])
            for sub in range(n_sub)]
    a_attn = jnp.concatenate(attn, axis=0)

    a = jnp.concatenate([a_pool, a_attn], axis=1).astype(BF16)
    mix = jnp.dot(a, wout_ref[...], preferred_element_type=F32)

    x = jnp.where(meta_ref[7, i] == 0, xp_ref[...], xs_ref[...])
    x1 = x + gt1 * mix
    h2 = _rms_mod(x1, g2_ref[...], sc2, sh2)
    h2b = h2.astype(BF16)
    planes = _pack_planes(h2)
    for j in range(N_PLANES):
        h2p_ref[j] = planes[j]
    gu = jnp.dot(h2b, wsgu_ref[...], preferred_element_type=F32)
    gate, up = gu[:, :D_SHARED], gu[:, D_SHARED:]
    act = (gate * jax.nn.sigmoid(gate)) * up
    shared = jnp.dot(act.astype(BF16), wsd_ref[...], preferred_element_type=F32)
    xa_ref[...] = x1 + gt2 * shared
    h2lo = (h2 - h2b.astype(F32)).astype(BF16)
    logits = lax.dot_general(wr3_ref[...], jnp.concatenate([h2b, h2b, h2lo], axis=1),
                             (((1,), (1,)), ((), ())), preferred_element_type=F32)
    scores = jax.nn.sigmoid(logits)
    biased = scores + rbias_ref[...]
    routed = [_route(biased[:, c * LANES:(c + 1) * LANES], scores[:, c * LANES:(c + 1) * LANES])
              for c in range(T_MIX // LANES)]
    for c, (idx_rows, w_rows, _, _) in enumerate(routed):
        wsum = w_rows[0]
        for wr in w_rows[1:]:
            wsum = wsum + wr
        wts_t = jnp.concatenate([wr / wsum * ROUTED_SCALE for wr in w_rows], axis=0)
        wpad = jnp.concatenate([wts_t, jnp.zeros((LANES - TOP_K, LANES), F32)], axis=0)
        wts_ref[pl.ds(c * LANES, LANES), :] = wpad.T
        idx_ref[:, pl.ds(c * LANES, LANES)] = jnp.concatenate(idx_rows, axis=0).astype(I32)
    sel_any = jnp.concatenate([r[2] for r in routed], axis=1)
    pref = jnp.dot(sel_any.astype(BF16), tri_ref[...], preferred_element_type=F32)
    before = base_ref[...] + pref[:, :T_MIX]
    for c, (idx_rows, _, _, rowf) in enumerate(routed):
        bc = before[:, c * LANES:(c + 1) * LANES]
        ranks = [jnp.sum(jnp.where(rowf == ik, bc, 0.0), axis=0, keepdims=True) for ik in idx_rows]
        rank_ref[:, pl.ds(c * LANES, LANES)] = jnp.concatenate(ranks, axis=0).astype(I32)
    new_base = base_ref[...] + pref[:, T_MIX:]
    base_ref[...] = new_base
    cnt_ref[...] = new_base


def _mixer(xp, xs, mod3, u, q, k, v, wpool_bf, bpool, pscale, sink, wout_bf, g2, wr3, rbias,
           wsgu_bf, wsd_bf, segments):
    n = _group_tokens(segments)
    t = T_MIX
    nt = n // t
    ab = ATT_BLOCK
    meta = jnp.asarray(_tile_meta(segments, t))
    tri = np.concatenate([np.triu(np.ones((t, t), np.float32), 1), np.ones((t, t), np.float32)], axis=1)
    tri = jnp.asarray(tri, BF16)
    abias = jnp.asarray(_attention_bias())
    cur = lambda w: pl.BlockSpec((t, w), lambda i, m: (i, 0))
    prev = lambda w: pl.BlockSpec((ab, w), lambda i, m: (jnp.maximum(i * (t // ab) - 1, 0), 0))
    nxt = lambda w: pl.BlockSpec((ab, w), lambda i, m: (jnp.minimum((i + 1) * (t // ab), n // ab - 1), 0))
    full = lambda a: pl.BlockSpec(a.shape, lambda i, m: (0,) * a.ndim)
    hb = t // HALO
    in_specs = [
        pl.BlockSpec((t, D_MODEL), lambda i, m: (m[5, i], 0)),
        pl.BlockSpec((t, D_MODEL), lambda i, m: (m[6, i], 0)),
        pl.BlockSpec((1, 1, 6 * D_MODEL), lambda i, m: (m[0, i], 0, 0)),
        pl.BlockSpec((HALO, POOL_WIDTH), lambda i, m: (jnp.maximum(i * hb - 1, 0), 0)),
        cur(POOL_WIDTH),
        pl.BlockSpec((HALO, POOL_WIDTH), lambda i, m: (jnp.minimum((i + 1) * hb, n // HALO - 1), 0)),
        cur(ATTN_WIDTH),
        prev(KV_WIDTH), cur(KV_WIDTH), nxt(KV_WIDTH),
        prev(KV_WIDTH), cur(KV_WIDTH), nxt(KV_WIDTH),
        full(wpool_bf), full(bpool), full(pscale), full(sink), full(wout_bf), full(g2),
        full(wr3), full(rbias), full(wsgu_bf), full(wsd_bf), full(tri), full(abias),
    ]
    out_specs = [
        cur(D_MODEL),
        pl.BlockSpec((N_PLANES, t, LANES), lambda i, m: (0, i, 0)),
        pl.BlockSpec((TOP_K, t), lambda i, m: (0, i)),
        pl.BlockSpec((TOP_K, t), lambda i, m: (0, i)),
        cur(LANES),
        pl.BlockSpec((N_EXPERTS, t), lambda i, m: (0, 0)),
    ]
    out_shape = [
        jax.ShapeDtypeStruct((n, D_MODEL), F32),
        jax.ShapeDtypeStruct((N_PLANES, n, LANES), I32),
        jax.ShapeDtypeStruct((TOP_K, n), I32),
        jax.ShapeDtypeStruct((TOP_K, n), I32),
        jax.ShapeDtypeStruct((n, LANES), F32),
        jax.ShapeDtypeStruct((N_EXPERTS, t), F32),
    ]
    return pl.pallas_call(
        _mixer_kernel,
        grid_spec=pltpu.PrefetchScalarGridSpec(
            num_scalar_prefetch=1, grid=(nt,), in_specs=in_specs, out_specs=out_specs,
            scratch_shapes=[pltpu.VMEM((t + 2 * HALO, POOL_WIDTH), F32),
                            pltpu.VMEM((t + 2 * ab, KV_WIDTH), BF16),
                            pltpu.VMEM((t + 2 * ab, KV_WIDTH), BF16),
                            pltpu.VMEM((N_EXPERTS, t), F32)]),
        out_shape=out_shape,
        compiler_params=pltpu.CompilerParams(dimension_semantics=("arbitrary",),
                                             vmem_limit_bytes=VMEM_LIMIT),
        name="mixer",
    )(meta, xp, xs, mod3, u, u, u, q, k, k, k, v, v, v, wpool_bf, bpool, pscale, sink, wout_bf, g2,
      wr3, rbias, wsgu_bf, wsd_bf, tri, abias)


def _dest_kernel(idx_ref, rank_ref, pstart_ref, o_ref, *, n_rows):
    slab = min(T_DEST_SLAB, idx_ref.shape[1])
    rowi = lax.broadcasted_iota(I32, (N_EXPERTS, slab), 0)
    pstart = pstart_ref[...]
    for c in range(idx_ref.shape[1] // slab):
        lanes = pl.ds(c * slab, slab)
        rows = []
        for kk in range(TOP_K):
            sel = rowi == idx_ref[kk:kk + 1, lanes]
            start = jnp.sum(jnp.where(sel, pstart, 0.0), axis=0, keepdims=True)
            rows.append(start.astype(I32) + rank_ref[kk:kk + 1, lanes])
        dest = jnp.concatenate(rows, axis=0)
        for j in range(N_PLANES):
            o_ref[:, j, lanes] = dest + j * n_rows


def _dest(idx_t, rank_t, pstart_col, n_rows):
    n = idx_t.shape[1]
    td = int(np.gcd(n, T_DEST))
    return pl.pallas_call(
        functools.partial(_dest_kernel, n_rows=n_rows),
        grid=(n // td,),
        in_specs=[pl.BlockSpec((TOP_K, td), lambda i: (0, i)),
                  pl.BlockSpec((TOP_K, td), lambda i: (0, i)),
                  pl.BlockSpec((N_EXPERTS, 1), lambda i: (0, 0))],
        out_specs=pl.BlockSpec((TOP_K, N_PLANES, td), lambda i: (0, 0, i)),
        out_shape=jax.ShapeDtypeStruct((TOP_K, N_PLANES, n), I32),
        name="dest",
    )(idx_t, rank_t, pstart_col)


def _sc_mesh():
    return plsc.VectorSubcoreMesh(core_axis_name="c", subcore_axis_name="s")


def _sc_worker():
    return lax.axis_index("s") * 2 + lax.axis_index("c")


def _sc_scatter_rows(table, didx, n_out_rows):
    m = table.shape[0]
    per_worker = m // SC_ROWS // SC_WORKERS
    assert per_worker % 2 == 0

    @functools.partial(
        pl.kernel, mesh=_sc_mesh(),
        out_type=jax.ShapeDtypeStruct((n_out_rows, LANES), I32),
        scratch_types=[pltpu.VMEM((2, SC_ROWS, LANES), I32), pltpu.VMEM((2, TOP_K, SC_ROWS), I32),
                       pltpu.SemaphoreType.DMA((2,)), pltpu.SemaphoreType.DMA((2,))])
    def run(table_hbm, didx_hbm, out_hbm, rows, idx, lsem, ssem):
        wid = _sc_worker()

        def loads(step, p):
            off = pl.multiple_of((wid * per_worker + step) * SC_ROWS, SC_ROWS)
            return (pltpu.make_async_copy(table_hbm.at[pl.ds(off, SC_ROWS)], rows.at[p], lsem.at[p]),
                    pltpu.make_async_copy(didx_hbm.at[:, pl.ds(off, SC_ROWS)], idx.at[p], lsem.at[p]))

        def scatter(p, kk):
            return pltpu.make_async_copy(rows.at[p], out_hbm.at[idx.at[p].at[kk]], ssem.at[p])

        for cp in loads(0, 0):
            cp.start()

        @pl.loop(0, per_worker, step=2)
        def _(s0):
            for p in range(2):
                step = s0 + p
                for cp in loads(step, p):
                    cp.wait()
                for kk in range(TOP_K):
                    scatter(p, kk).start()

                @pl.when(step > 0)
                def _():
                    for kk in range(TOP_K):
                        scatter(1 - p, kk).wait()

                @pl.when(step + 1 < per_worker)
                def _():
                    for cp in loads(step + 1, 1 - p):
                        cp.start()

        for kk in range(TOP_K):
            scatter(1, kk).wait()

    return run(table, didx)


def _sc_gather_rows(table, didx):
    m = didx.shape[1]
    per_worker = m // SC_ROWS // SC_WORKERS
    assert per_worker % 2 == 0 and (2 * TOP_K) % SC_NBUF == 0
    items = 2 * TOP_K

    @functools.partial(
        pl.kernel, mesh=_sc_mesh(),
        out_type=jax.ShapeDtypeStruct((TOP_K, m, LANES), I32),
        scratch_types=[pltpu.VMEM((SC_NBUF, SC_ROWS, LANES), I32), pltpu.VMEM((2, TOP_K, SC_ROWS), I32),
                       pltpu.SemaphoreType.DMA((SC_NBUF,)), pltpu.SemaphoreType.DMA((SC_NBUF,)),
                       pltpu.SemaphoreType.DMA((2,))])
    def run(table_hbm, didx_hbm, out_hbm, rows, idx, gsem, wsem, isem):
        wid = _sc_worker()

        def off_of(step):
            return pl.multiple_of((wid * per_worker + step) * SC_ROWS, SC_ROWS)

        def idx_copy(step, p):
            return pltpu.make_async_copy(didx_hbm.at[:, pl.ds(off_of(step), SC_ROWS)], idx.at[p], isem.at[p])

        def gather(j):
            r = j % SC_NBUF
            return pltpu.make_async_copy(table_hbm.at[idx.at[j // TOP_K].at[j % TOP_K]], rows.at[r], gsem.at[r])

        def write(s0, j):
            r = j % SC_NBUF
            return pltpu.make_async_copy(rows.at[r], out_hbm.at[j % TOP_K, pl.ds(off_of(s0 + j // TOP_K), SC_ROWS)],
                                         wsem.at[r])

        def retire_write(s0, j):
            if j >= 0:
                write(s0, j).wait()
            else:
                @pl.when(s0 > 0)
                def _():
                    write(s0 - 2, j + items).wait()

        def finish_read(s0, j):
            if j >= 0:
                gather(j).wait()
                write(s0, j).start()
            else:
                @pl.when(s0 > 0)
                def _():
                    gather(j + items).wait()
                    write(s0 - 2, j + items).start()

        idx_copy(0, 0).start()

        @pl.loop(0, per_worker, step=2)
        def _(s0):
            for j in range(items):
                p, kk = j // TOP_K, j % TOP_K
                if kk == 0:
                    idx_copy(s0 + p, p).wait()
                retire_write(s0, j - SC_NBUF)
                gather(j).start()
                finish_read(s0, j - SC_LAG)
                if kk == SC_LAG:
                    @pl.when(s0 + p + 1 < per_worker)
                    def _():
                        idx_copy(s0 + p + 1, 1 - p).start()

        last = per_worker - 2
        for j in range(items - SC_LAG, items):
            gather(j).wait()
            write(last, j).start()
        for j in range(items - SC_NBUF, items):
            write(last, j).wait()

    return run(table, didx)


def _experts_kernel(c0_ref, ntot_ref, tsrc_ref, trow_ref, tvalid_ref, total_ref, *refs, n_groups):
    x_hbms = refs[:n_groups]
    wg_ref, wu_ref, wd_ref = refs[n_groups:n_groups + 3]
    y_hbms = refs[n_groups + 3:2 * n_groups + 3]
    wgu_s, wd_s, xbuf, ybuf, xsem, ysem = refs[2 * n_groups + 3:]
    e = pl.program_id(0)
    nb = xbuf.shape[0]
    ch = xbuf.shape[2]
    total = total_ref[0]

    def x_copy(c, g, row):
        slot = lax.rem(c, nb)
        return pltpu.make_async_copy(x_hbms[g].at[:, pl.ds(row, ch), :], xbuf.at[slot], xsem.at[slot])

    def y_copy(c, g, row):
        slot = lax.rem(c, nb)
        return pltpu.make_async_copy(ybuf.at[slot], y_hbms[g].at[:, pl.ds(row, ch), :], ysem.at[slot])

    def start(copy, c):
        row = pl.multiple_of(trow_ref[c], ch)
        if n_groups == 1:
            copy(c, 0, row).start()
            return
        for g in range(n_groups):
            @pl.when(tsrc_ref[c] == g)
            def _():
                copy(c, g, row).start()

    def wait(copy, c):
        copy(c, 0, 0).wait()

    @pl.when(e == 0)
    def _():
        for c in range(nb - 1):
            @pl.when(c < total)
            def _():
                start(x_copy, c)

    for j in range(N_PLANES):
        hi = pl.ds(j * LANES, LANES)
        lo = pl.ds(HALF + j * LANES, LANES)
        r_hi = pl.ds(2 * j * LANES, LANES)
        r_lo = pl.ds((2 * j + 1) * LANES, LANES)
        wgu_s[r_hi, pl.ds(0, D_EXPERT)] = wg_ref[0, hi, :].astype(BF16)
        wgu_s[r_lo, pl.ds(0, D_EXPERT)] = wg_ref[0, lo, :].astype(BF16)
        wgu_s[r_hi, pl.ds(D_EXPERT, D_EXPERT)] = wu_ref[0, hi, :].astype(BF16)
        wgu_s[r_lo, pl.ds(D_EXPERT, D_EXPERT)] = wu_ref[0, lo, :].astype(BF16)
    wd_s[...] = wd_ref[0].astype(BF16)

    c0 = c0_ref[e]

    def chunk(i, carry):
        c = c0 + i
        slot = lax.rem(c, nb)

        @pl.when(c + nb - 1 < total)
        def _():
            start(x_copy, c + nb - 1)

        wait(x_copy, c)

        @pl.when(c >= nb)
        def _():
            wait(y_copy, c - nb)

        valid = tvalid_ref[c]

        def swiglu_rows(n_rows):
            rows = pl.ds(0, n_rows)
            live = lax.broadcasted_iota(I32, (n_rows, LANES), 0) < valid
            xs = []
            for j in range(N_PLANES):
                hi, lo = _unpack_plane(jnp.where(live, xbuf[slot, j, rows, :], 0))
                xs += [hi.astype(BF16), lo.astype(BF16)]
            acc = jnp.dot(jnp.concatenate(xs, axis=1), wgu_s[...], preferred_element_type=F32)
            gate, up = acc[:, :D_EXPERT], acc[:, D_EXPERT:]
            act = (gate * jax.nn.sigmoid(gate)) * up
            y = jnp.dot(act.astype(BF16), wd_s[...], preferred_element_type=F32)
            planes = _pack_planes(y)
            for j in range(N_PLANES):
                ybuf[slot, j, rows, :] = planes[j]
                if n_rows < ch:
                    ybuf[slot, j, pl.ds(n_rows, ch - n_rows), :] = jnp.zeros((ch - n_rows, LANES), I32)

        pl.when(valid > ch // 2)(lambda: swiglu_rows(ch))
        pl.when((valid > ch // 4) & (valid <= ch // 2))(lambda: swiglu_rows(ch // 2))
        pl.when(valid <= ch // 4)(lambda: swiglu_rows(ch // 4))
        start(y_copy, c)
        return carry

    lax.fori_loop(0, ntot_ref[e], chunk, 0)

    @pl.when(e == pl.num_programs(0) - 1)
    def _():
        for back in range(nb, 0, -1):
            @pl.when(total >= back)
            def _():
                wait(y_copy, total - back)


def _chunk_tables(nchunks, cstarts, counts, max_chunks):
    ch = EXPERT_CHUNK
    ntot = sum(nchunks)
    cum = jnp.cumsum(ntot)
    c0 = cum - ntot
    c = jnp.arange(max_chunks, dtype=I32)
    e = jnp.minimum(jnp.sum((c[:, None] >= cum[None, :]).astype(I32), axis=1), N_EXPERTS - 1)
    own = e[:, None] == jnp.arange(N_EXPERTS, dtype=I32)[None, :]
    pick = lambda v: jnp.sum(jnp.where(own, v[None, :], 0), axis=1)
    i = c - pick(c0)
    src = jnp.zeros_like(c)
    row = jnp.zeros_like(c)
    valid = jnp.zeros_like(c)
    lo = jnp.zeros_like(c)
    for g in range(len(nchunks)):
        n_g = pick(nchunks[g])
        inside = (i >= lo) & (i < lo + n_g) & (c < cum[-1])
        loc = i - lo
        src = jnp.where(inside, g, src)
        row = jnp.where(inside, (pick(cstarts[g]) + loc) * ch, row)
        valid = jnp.where(inside, jnp.clip(pick(counts[g]) - loc * ch, 0, ch), valid)
        lo = lo + n_g
    return c0.astype(I32), ntot.astype(I32), src, row, valid, cum[-1:].astype(I32)


def _experts(xs_planes, w_gate, w_up, w_down, nchunks, cstarts, counts):
    ng = len(xs_planes)
    ch = EXPERT_CHUNK
    tables = _chunk_tables(nchunks, cstarts, counts, sum(x.shape[1] // ch for x in xs_planes))
    w_map = lambda e, *_: (e, 0, 0)
    any_space = pl.BlockSpec(memory_space=pl.ANY)
    return pl.pallas_call(
        functools.partial(_experts_kernel, n_groups=ng),
        grid_spec=pltpu.PrefetchScalarGridSpec(
            num_scalar_prefetch=6, grid=(N_EXPERTS,),
            in_specs=[any_space] * ng + [pl.BlockSpec((1, D_MODEL, D_EXPERT), w_map),
                                         pl.BlockSpec((1, D_MODEL, D_EXPERT), w_map),
                                         pl.BlockSpec((1, D_EXPERT, D_MODEL), w_map)],
            out_specs=[any_space] * ng,
            scratch_shapes=[pltpu.VMEM((D_MODEL, 2 * D_EXPERT), BF16),
                            pltpu.VMEM((D_EXPERT, D_MODEL), BF16),
                            pltpu.VMEM((EXPERT_NBUF, N_PLANES, ch, LANES), I32),
                            pltpu.VMEM((EXPERT_NBUF, N_PLANES, ch, LANES), I32),
                            pltpu.SemaphoreType.DMA((EXPERT_NBUF,)),
                            pltpu.SemaphoreType.DMA((EXPERT_NBUF,))]),
        out_shape=[jax.ShapeDtypeStruct(x.shape, I32) for x in xs_planes],
        compiler_params=pltpu.CompilerParams(dimension_semantics=("arbitrary",),
                                             vmem_limit_bytes=VMEM_LIMIT),
        name="experts",
    )(*tables, *xs_planes, w_gate, w_up, w_down)


def _final_kernel(xa_ref, yk_ref, wts_ref, mod_ref, g_ref, o_ref):
    wts = wts_ref[...]
    his = [jnp.zeros((T_FIN, LANES), F32) for _ in range(N_PLANES)]
    los = [jnp.zeros((T_FIN, LANES), F32) for _ in range(N_PLANES)]
    for kk in range(TOP_K):
        wk = wts[:, kk:kk + 1]
        for j in range(N_PLANES):
            hi, lo = _unpack_plane(yk_ref[kk, j])
            his[j] = his[j] + wk * hi
            los[j] = los[j] + wk * lo
    routed = jnp.concatenate(his + los, axis=1)
    gt2 = mod_ref[0][:, 5 * D_MODEL:6 * D_MODEL]
    x = xa_ref[...] + gt2 * routed
    ms = jnp.mean(x * x, axis=-1, keepdims=True)
    o_ref[...] = x * lax.rsqrt(ms + EPS) * g_ref[...]


def _final_kernel_into(xa_ref, yk_ref, wts_ref, mod_ref, g_ref, prev_ref, o_ref):
    _final_kernel(xa_ref, yk_ref, wts_ref, mod_ref, g_ref, o_ref)


def _final(xa, yk, wts, mod3, g_final, tile_off, n_tok, seq_len, sid0, out_rows, out_tile_off, prev):
    tiles_per_seq = seq_len // T_FIN
    in_specs = [pl.BlockSpec((T_FIN, D_MODEL), lambda i: (i + tile_off, 0)),
                pl.BlockSpec((TOP_K, N_PLANES, T_FIN, LANES), lambda i: (0, 0, i + tile_off, 0)),
                pl.BlockSpec((T_FIN, LANES), lambda i: (i + tile_off, 0)),
                pl.BlockSpec((1, 1, 6 * D_MODEL), lambda i: (sid0 + i // tiles_per_seq, 0, 0)),
                pl.BlockSpec((1, D_MODEL), lambda i: (0, 0))]
    args = [xa, yk, wts, mod3, g_final]
    if prev is not None:
        in_specs.append(pl.BlockSpec(memory_space=pl.ANY))
        args.append(prev)
    return pl.pallas_call(
        _final_kernel if prev is None else _final_kernel_into,
        grid=(n_tok // T_FIN,),
        in_specs=in_specs,
        out_specs=pl.BlockSpec((T_FIN, D_MODEL), lambda i: (i + out_tile_off, 0)),
        out_shape=jax.ShapeDtypeStruct((out_rows, D_MODEL), F32),
        input_output_aliases={} if prev is None else {len(args) - 1: 0},
        compiler_params=pltpu.CompilerParams(dimension_semantics=("arbitrary",),
                                             vmem_limit_bytes=VMEM_LIMIT),
        name="final",
    )(*args)


def _token_groups(bp, sp, bs, ss):
    return [[(0, 0, bp, sp, 0), (1, 0, bs, ss, bp)]]


def kernel(x_prompt, x_sample, c_prompt, c_sample, w_ada, b_ada, g_norm1, w_in, w_pool, b_pool, pool_scale, attn_sink, w_out, g_norm2, w_router, router_bias, w_gate, w_up, w_down, ws_gate, ws_up, ws_down, g_final):
    assert w_ada.shape[0] == 1, "one layer"
    bp, sp, d = x_prompt.shape
    bs, ss, _ = x_sample.shape
    assert d == D_MODEL and bp + bs <= ADA_ROWS
    n_p, n_s = bp * sp, bs * ss
    xp = x_prompt.reshape(n_p, d)
    xs = x_sample.reshape(n_s, d)

    c_all = jnp.concatenate([c_prompt, c_sample, jnp.zeros((ADA_ROWS - bp - bs, d), F32)], axis=0)
    mod3 = _ada(c_all, w_ada[0], b_ada[0]).reshape(ADA_ROWS, 1, 6 * d)

    q0, q1 = POOL_WIDTH, POOL_WIDTH + ATTN_WIDTH
    wq = w_in[0][:, q0:q1].reshape(d, N_KV_HEADS, Q_PER_KV, HEAD_DIM).transpose(0, 2, 1, 3).reshape(d, ATTN_WIDTH)
    w_in_p = jnp.concatenate([w_in[0][:, :q0], wq, w_in[0][:, q1:]], axis=1).astype(BF16)
    wo = w_out[0][q0:].reshape(N_KV_HEADS, Q_PER_KV, HEAD_DIM, d).transpose(1, 0, 2, 3).reshape(ATTN_WIDTH, d)
    w_out_p = jnp.concatenate([w_out[0][:q0], wo], axis=0).astype(BF16)
    wsgu = jnp.concatenate([ws_gate[0], ws_up[0]], axis=1).astype(BF16)
    wr_hi = w_router[0].astype(BF16)
    wr_lo = (w_router[0] - wr_hi.astype(F32)).astype(BF16)
    wr3 = jnp.concatenate([wr_hi, wr_lo, wr_hi], axis=0).T
    g1 = g_norm1[0].reshape(1, d)
    mixer_weights = (
        w_pool[0].astype(BF16), b_pool[0].reshape(1, POOL_WIDTH), pool_scale[0].reshape(1, POOL_WIDTH),
        attn_sink[0].reshape(1, N_HEADS), w_out_p, g_norm2[0].reshape(1, d),
        wr3, router_bias[0].reshape(N_EXPERTS, 1), wsgu, ws_down[0].astype(BF16))
    ch = EXPERT_CHUNK

    groups = _token_groups(bp, sp, bs, ss)
    staged = []
    for segments in groups:
        n = _group_tokens(segments)
        u, q, k, v = _inproj(xp, xs, mod3, g1, w_in_p, segments)
        xa, h2p, idx_t, rank_t, wts, cnt = _mixer(xp, xs, mod3, u, q, k, v, *mixer_weights, segments)
        n_rows = (-(-n * TOP_K // ch) + N_EXPERTS) * ch
        counts = cnt[:, 0].astype(I32)
        padded = (counts + ch - 1) // ch * ch
        pstart = jnp.cumsum(padded) - padded
        didx = _dest(idx_t, rank_t, pstart.astype(F32).reshape(N_EXPERTS, 1), n_rows)
        didx = didx.reshape(TOP_K, N_PLANES * n)
        xs_rows = _sc_scatter_rows(h2p.reshape(N_PLANES * n, LANES), didx, N_PLANES * n_rows)
        staged.append((xa, wts, didx, xs_rows.reshape(N_PLANES, n_rows, LANES), padded // ch, pstart // ch, counts))

    ys = _experts([st[3] for st in staged], w_gate[0], w_up[0], w_down[0],
                  [st[4] for st in staged], [st[5] for st in staged], [st[6] for st in staged])

    gf = g_final.reshape(1, d)
    outs = [None, None]
    out_rows = (n_p, n_s)
    for segments, (xa, wts, didx, xs_rows, _, _, _), ys_rows in zip(groups, staged, ys):
        n = _group_tokens(segments)
        yk = _sc_gather_rows(ys_rows.reshape(N_PLANES * ys_rows.shape[1], LANES), didx)
        yk = yk.reshape(TOP_K, N_PLANES, n, LANES)
        tok = 0
        for src, tok0, n_seqs, seq_len, sid0 in segments:
            n_tok = n_seqs * seq_len
            outs[src] = _final(xa, yk, wts, mod3, gf, tok // T_FIN, n_tok, seq_len, sid0,
                               out_rows[src], tok0 // T_FIN, outs[src])
            tok += n_tok
    return outs[0].reshape(bp, sp, d), outs[1].reshape(bs, ss, d)
```
